```python
import jax
import jax.numpy as jnp
from jax import lax
import numpy as np

D_MODEL = 1024
BATCH = 8
SEQ = 2048
DEPTH = 2

ATT_HEADS = 8
ATT_HEAD_DIM = 64
ATT_WIDTH = ATT_HEADS * ATT_HEAD_DIM
MOBA_BLOCK = 256
MOBA_TOPK = 3
MOBA_QCHUNK = 32
HGRN_HEADS = 4
HGRN_EXPAND = 128
HGRN_VDIM = 128
HGRN_KEY_WIDTH = HGRN_HEADS * HGRN_EXPAND
HGRN_WIDTH = HGRN_HEADS * HGRN_VDIM
HGRN_CHUNK = 64
CONV_CH = 512
CONV_WIDTH = 31
N_BRANCH = 3
IN_COLS = 3 * ATT_WIDTH + 2 * HGRN_KEY_WIDTH + 2 * HGRN_WIDTH + 2 * CONV_CH + N_BRANCH * D_MODEL
N_GROUPS = 4
EXPERTS_PER_GROUP = 8
N_EXPERTS = N_GROUPS * EXPERTS_PER_GROUP
EXPERT_TOPK = 2
EXPERT_FF = 512

RMS_EPS = 1e-6
LN_EPS = 1e-5
NEG_INF = -1e30

kernel_name = 'hybrid_moba_hgrn2_conformer_hmoe'


def rms_norm(x, gain):
    xf = x.astype(jnp.float32)
    y = xf * lax.rsqrt(jnp.mean(xf * xf, axis=-1, keepdims=True) + RMS_EPS)
    return (y * gain.astype(jnp.float32)).astype(x.dtype)


def moba_attention(q, k, v):
    B, S, H, dh = q.shape
    n_blk = -(-S // MOBA_BLOCK)
    s_pad = n_blk * MOBA_BLOCK
    n_sel = min(MOBA_TOPK, n_blk)
    n_qc = S // MOBA_QCHUNK
    scale = dh ** -0.5
    pad = ((0, 0), (0, s_pad - S), (0, 0), (0, 0))
    qh = q.transpose(0, 2, 1, 3)
    kb = jnp.pad(k, pad).transpose(0, 2, 1, 3).reshape(B, H, n_blk, MOBA_BLOCK, dh)
    vb = jnp.pad(v, pad).transpose(0, 2, 1, 3).reshape(B, H, n_blk, MOBA_BLOCK, dh)
    k_mean = jnp.mean(kb.astype(jnp.float32), axis=3)
    q_blk = jnp.arange(S) // MOBA_BLOCK
    gate = jnp.einsum('bhsd,bhnd->bhsn', qh.astype(jnp.float32), k_mean)
    fully_past = jnp.arange(n_blk)[None, :] < q_blk[:, None]
    gate = jnp.where(fully_past, gate, NEG_INF)
    _, sel = lax.top_k(gate, n_sel)
    sel_valid = jnp.arange(n_sel)[None, :] < jnp.minimum(n_sel, q_blk)[:, None]

    def to_chunks(a):
        a = a.reshape(B, H, n_qc, MOBA_QCHUNK, a.shape[-1])
        return jnp.moveaxis(a, 2, 0)

    b_idx = jnp.arange(B)[:, None, None, None]
    h_idx = jnp.arange(H)[None, :, None, None]

    def attend_chunk(args):
        c, qc, sc, vc = args
        blk = (c * MOBA_QCHUNK) // MOBA_BLOCK
        qp = c * MOBA_QCHUNK + jnp.arange(MOBA_QCHUNK)
        kp = blk * MOBA_BLOCK + jnp.arange(MOBA_BLOCK)
        k_own = lax.dynamic_index_in_dim(kb, blk, axis=2, keepdims=False)
        v_own = lax.dynamic_index_in_dim(vb, blk, axis=2, keepdims=False)
        s_own = jnp.einsum('bhqd,bhkd->bhqk', qc, k_own).astype(jnp.float32) * scale
        s_own = jnp.where(kp[None, :] <= qp[:, None], s_own, NEG_INF)
        k_sel = kb[b_idx, h_idx, sc].reshape(B, H, MOBA_QCHUNK, n_sel * MOBA_BLOCK, dh)
        v_sel = vb[b_idx, h_idx, sc].reshape(B, H, MOBA_QCHUNK, n_sel * MOBA_BLOCK, dh)
        s_sel = jnp.einsum('bhqd,bhqkd->bhqk', qc, k_sel).astype(jnp.float32) * scale
        s_sel = jnp.where(jnp.repeat(vc, MOBA_BLOCK, axis=-1), s_sel, NEG_INF)
        p = jax.nn.softmax(jnp.concatenate([s_own, s_sel], axis=-1), axis=-1).astype(v.dtype)
        return (jnp.einsum('bhqk,bhkd->bhqd', p[..., :MOBA_BLOCK], v_own)
                + jnp.einsum('bhqk,bhqkd->bhqd', p[..., MOBA_BLOCK:], v_sel))

    o = lax.map(attend_chunk, (jnp.arange(n_qc), to_chunks(qh), to_chunks(sel),
                               sel_valid.reshape(n_qc, MOBA_QCHUNK, n_sel)))
    return jnp.moveaxis(o, 0, 2).reshape(B, H, S, dh).transpose(0, 2, 1, 3).reshape(B, S, H * dh)


def hgrn2_recurrence(q, f_raw, i, lower_bound):
    B, S, H, dk = q.shape
    dv = i.shape[-1]
    C = HGRN_CHUNK
    n_c = S // C
    fr = f_raw.astype(jnp.float32)
    lb = lower_bound.astype(jnp.float32)
    sig = jax.nn.sigmoid(fr)
    log_f = jnp.log(lb + (1.0 - lb) * sig)
    k_in = (1.0 - lb) * (1.0 - sig)
    q_act = jax.nn.silu(q.astype(jnp.float32))

    def to_chunks(a):
        return a.reshape(B, n_c, C, H, a.shape[-1]).transpose(1, 0, 3, 2, 4)

    causal = jnp.tril(jnp.ones((C, C), dtype=bool))[:, :, None]

    def chunk_step(state, xs):
        qc, kc, vc, gc = xs
        b = jnp.cumsum(gc, axis=2)
        o_inter = jnp.einsum('bhtk,bhkv->bhtv', qc * jnp.exp(b), state)
        rel = jnp.where(causal, b[:, :, :, None, :] - b[:, :, None, :, :], 0.0)
        decay = jnp.where(causal, jnp.exp(rel), 0.0)
        scores = jnp.einsum('bhtk,bhtsk,bhsk->bhts', qc, decay, kc)
        o = o_inter + jnp.einsum('bhts,bhsv->bhtv', scores, vc)
        b_end = b[:, :, -1:, :]
        state = (jnp.exp(b_end[:, :, 0, :])[..., None] * state
                 + jnp.einsum('bhsk,bhsv->bhkv', kc * jnp.exp(b_end - b), vc))
        return state, o

    state0 = jnp.zeros((B, H, dk, dv), jnp.float32)
    _, o = lax.scan(chunk_step, state0, (to_chunks(q_act), to_chunks(k_in),
                                         to_chunks(i.astype(jnp.float32)), to_chunks(log_f)))
    return o.transpose(1, 0, 3, 2, 4).reshape(B, S, H, dv)


def conformer_conv(a, b, w_dw, b_dw, ln_g, ln_b):
    u = a * jax.nn.sigmoid(b)
    u = lax.conv_general_dilated(u, w_dw[:, None, :].astype(u.dtype), window_strides=(1,),
                                 padding=[(CONV_WIDTH - 1, 0)],
                                 dimension_numbers=('NWC', 'WIO', 'NWC'),
                                 feature_group_count=CONV_CH)
    uf = u.astype(jnp.float32) + b_dw.astype(jnp.float32)
    mu = jnp.mean(uf, axis=-1, keepdims=True)
    var = jnp.mean(jnp.square(uf - mu), axis=-1, keepdims=True)
    uf = (uf - mu) * lax.rsqrt(var + LN_EPS) * ln_g.astype(jnp.float32) + ln_b.astype(jnp.float32)
    return jax.nn.silu(uf).astype(a.dtype)


def hybrid_mixer(h, w_in, b_gate, q_norm_g, k_norm_g, lower_bound, hgrn_norm_g, conv_w, conv_b,
                 conv_ln_g, conv_ln_b, w_att_o, w_hgrn_o, w_conv_o, w_out):
    B, S, _ = h.shape
    widths = [ATT_WIDTH] * 3 + [HGRN_KEY_WIDTH] * 2 + [HGRN_WIDTH] * 2 + [CONV_CH] * 2
    offsets = np.cumsum(widths).tolist()
    proj = h @ w_in
    aq, ak, av, hq, hf, hi, hg, ca, cb, gl = jnp.split(proj, offsets, axis=-1)
    att_shape = (B, S, ATT_HEADS, ATT_HEAD_DIM)
    q = rms_norm(aq.reshape(att_shape), q_norm_g)
    k = rms_norm(ak.reshape(att_shape), k_norm_g)
    y_att = moba_attention(q, k, av.reshape(att_shape)) @ w_att_o
    o = hgrn2_recurrence(hq.reshape(B, S, HGRN_HEADS, HGRN_EXPAND),
                         hf.reshape(B, S, HGRN_HEADS, HGRN_EXPAND),
                         hi.reshape(B, S, HGRN_HEADS, HGRN_VDIM),
                         lower_bound.reshape(HGRN_HEADS, HGRN_EXPAND))
    o = rms_norm(o, hgrn_norm_g) * jax.nn.silu(hg.reshape(B, S, HGRN_HEADS, HGRN_VDIM).astype(jnp.float32))
    y_rec = o.reshape(B, S, HGRN_WIDTH).astype(h.dtype) @ w_hgrn_o
    y_conv = conformer_conv(ca, cb, conv_w, conv_b, conv_ln_g, conv_ln_b) @ w_conv_o
    gates = jax.nn.sigmoid(gl + b_gate).reshape(B, S, N_BRANCH, D_MODEL)
    merged = gates[:, :, 0] * y_att + gates[:, :, 1] * y_rec + gates[:, :, 2] * y_conv
    return merged @ w_out


def hierarchical_moe(h, w_coarse, b_coarse, w_fine, b_fine, w_exp_in, w_exp_out):
    B, S, D = h.shape
    t = h.reshape(-1, D)
    p_grp = jax.nn.softmax((t @ w_coarse + b_coarse).astype(jnp.float32), axis=-1)
    p_top, grp = lax.top_k(p_grp, 1)
    fine = (t @ w_fine + b_fine).astype(jnp.float32).reshape(-1, N_GROUPS, EXPERTS_PER_GROUP)
    fine_sel = jnp.take_along_axis(fine, grp[:, :, None], axis=1)[:, 0]
    top_logit, local = lax.top_k(fine_sel, EXPERT_TOPK)
    w = jax.nn.softmax(top_logit, axis=-1) * p_top
    expert_id = grp * EXPERTS_PER_GROUP + local
    combine = jnp.sum(jax.nn.one_hot(expert_id, N_EXPERTS, dtype=jnp.float32) * w[..., None], axis=1)
    y = jnp.zeros(t.shape, jnp.float32)
    for e in range(N_EXPERTS):
        gate, up = jnp.split(t @ w_exp_in[e], 2, axis=-1)
        y = y + combine[:, e:e + 1] * ((jax.nn.silu(gate) * up) @ w_exp_out[e]).astype(jnp.float32)
    return y.astype(h.dtype).reshape(B, S, D)


def setup_inputs(seed: int = 0) -> dict:
    key = jax.random.key(seed)
    ks = jax.random.split(key, 23)
    L, D = DEPTH, D_MODEL

    def nrm(k, shape, scale):
        return jax.random.normal(k, shape, jnp.float32) * scale

    return {
        'x': nrm(ks[0], (BATCH, SEQ, D), 1.0),
        'mix_norm_g': 1.0 + nrm(ks[1], (L, D), 0.02),
        'w_in': nrm(ks[2], (L, D, IN_COLS), D ** -0.5),
        'b_gate': nrm(ks[3], (L, N_BRANCH * D), 0.02),
        'q_norm_g': 1.0 + nrm(ks[4], (L, ATT_HEAD_DIM), 0.02),
        'k_norm_g': 1.0 + nrm(ks[5], (L, ATT_HEAD_DIM), 0.02),
        'lb_logits': nrm(ks[6], (L, HGRN_KEY_WIDTH), 0.1),
        'hgrn_norm_g': 1.0 + nrm(ks[7], (L, HGRN_VDIM), 0.02),
        'conv_w': nrm(ks[8], (L, CONV_WIDTH, CONV_CH), CONV_WIDTH ** -0.5),
        'conv_b': nrm(ks[9], (L, CONV_CH), 0.02),
        'conv_ln_g': 1.0 + nrm(ks[10], (L, CONV_CH), 0.02),
        'conv_ln_b': nrm(ks[11], (L, CONV_CH), 0.02),
        'w_att_o': nrm(ks[12], (L, ATT_WIDTH, D), ATT_WIDTH ** -0.5),
        'w_hgrn_o': nrm(ks[13], (L, HGRN_WIDTH, D), HGRN_WIDTH ** -0.5),
        'w_conv_o': nrm(ks[14], (L, CONV_CH, D), CONV_CH ** -0.5),
        'w_out': nrm(ks[15], (L, D, D), D ** -0.5),
        'ffn_norm_g': 1.0 + nrm(ks[16], (L, D), 0.02),
        'w_coarse': nrm(ks[17], (L, D, N_GROUPS), D ** -0.5),
        'b_coarse': nrm(ks[18], (L, N_GROUPS), 0.01),
        'w_fine': nrm(ks[19], (L, D, N_EXPERTS), D ** -0.5),
        'b_fine': nrm(ks[20], (L, N_EXPERTS), 0.01),
        'w_exp_in': nrm(ks[21], (L, N_EXPERTS, D, 2 * EXPERT_FF), D ** -0.5),
        'w_exp_out': nrm(ks[22], (L, N_EXPERTS, EXPERT_FF, D), EXPERT_FF ** -0.5),
    }


def reference(x, mix_norm_g, w_in, b_gate, q_norm_g, k_norm_g, lb_logits, hgrn_norm_g, conv_w,
              conv_b, conv_ln_g, conv_ln_b, w_att_o, w_hgrn_o, w_conv_o, w_out, ffn_norm_g,
              w_coarse, b_coarse, w_fine, b_fine, w_exp_in, w_exp_out):
    lb_p = jax.nn.softmax(lb_logits.astype(jnp.float32), axis=0)
    lb_all = jnp.maximum(jnp.cumsum(lb_p, axis=0) - lb_p[0:1], 0.0)
    for l in range(DEPTH):
        h = rms_norm(x, mix_norm_g[l])
        x = x + hybrid_mixer(h, w_in[l], b_gate[l], q_norm_g[l], k_norm_g[l], lb_all[l],
                             hgrn_norm_g[l], conv_w[l], conv_b[l], conv_ln_g[l], conv_ln_b[l],
                             w_att_o[l], w_hgrn_o[l], w_conv_o[l], w_out[l])
        h = rms_norm(x, ffn_norm_g[l])
        x = x + hierarchical_moe(h, w_coarse[l], b_coarse[l], w_fine[l], b_fine[l],
                                 w_exp_in[l], w_exp_out[l])
    return x
```

```python
import functools

import jax
import jax.numpy as jnp
from jax import lax
from jax.experimental import pallas as pl
from jax.experimental.pallas import tpu as pltpu

F32 = jnp.float32
BF16 = jnp.bfloat16

LANES = 128
RMS_EPS = 1e-6
LN_EPS = 1e-5
NEG_INF = -1e30

ATT_HEADS = 8
ATT_HEAD_DIM = 64
MOBA_BLOCK = 256
MOBA_TOPK = 3
HGRN_HEADS = 4
HGRN_DIM = 128
HGRN_ROWS = 256
CONV_WIDTH = 31
CONV_ROWS = 256
CONV_HALO = 32
N_GROUPS = 4
EXPERTS_PER_GROUP = 8
EXPERT_TILE = 256
TOKEN_TILE = 256
VMEM_LIMIT = 56 * 1024 * 1024


def _params(n_axes, vmem=None):
    return pltpu.CompilerParams(dimension_semantics=("arbitrary",) * n_axes,
                                vmem_limit_bytes=vmem)


def _dot(a, b):
    return jnp.dot(a, b, preferred_element_type=F32)


def _dot_nt(a, b):
    return lax.dot_general(a, b, (((1,), (1,)), ((), ())), preferred_element_type=F32)


def _dot_tn(a, b):
    return lax.dot_general(a, b, (((0,), (0,)), ((), ())), preferred_element_type=F32)


def _split3(x):
    x1 = x.astype(BF16)
    r1 = x - x1.astype(F32)
    x2 = r1.astype(BF16)
    x3 = (r1 - x2.astype(F32)).astype(BF16)
    return x1, x2, x3


def _sigmoid(x):
    return 1.0 / (1.0 + jnp.exp(-x))


def _silu(x):
    return x * _sigmoid(x)


def _inproj_body(x_ref, g_ref, w_ref, qkv_ref, hf_ref, hqig_ref, cv_ref, gl_ref, *, segs, chunk):
    x = x_ref[...]
    ms = jnp.mean(x * x, axis=-1, keepdims=True)
    h = (x * lax.rsqrt(ms + RMS_EPS) * g_ref[...]).astype(BF16)
    refs = (qkv_ref, hf_ref, hqig_ref, cv_ref, gl_ref)
    for ridx, dst0, src0, width in segs:
        ref = refs[ridx]
        for c in range(0, width, chunk):
            y = _dot(h, w_ref[:, src0 + c:src0 + c + chunk])
            ref[:, dst0 + c:dst0 + c + chunk] = y.astype(ref.dtype)


def _inproj(x2d, norm_g, w_in_bf, layer):
    T, D = x2d.shape
    A = ATT_HEADS * ATT_HEAD_DIM
    K = HGRN_HEADS * HGRN_DIM
    C = A
    segs = ((0, 0, 0, 3 * A),
            (2, 0, 3 * A, K),
            (1, 0, 3 * A + K, K),
            (2, K, 3 * A + 2 * K, K),
            (2, 2 * K, 3 * A + 3 * K, K),
            (3, 0, 3 * A + 4 * K, 2 * C),
            (4, 0, 3 * A + 4 * K + 2 * C, 3 * D))
    ncols = w_in_bf.shape[-1]
    tm = TOKEN_TILE
    row = lambda i: (i, 0)
    return pl.pallas_call(
        functools.partial(_inproj_body, segs=segs, chunk=512),
        grid=(T // tm,),
        in_specs=[pl.BlockSpec((tm, D), row),
                  pl.BlockSpec((1, D), lambda i: (0, 0)),
                  pl.BlockSpec((None, D, ncols), lambda i: (layer, 0, 0))],
        out_specs=[pl.BlockSpec((tm, 3 * A), row), pl.BlockSpec((tm, K), row),
                   pl.BlockSpec((tm, 3 * K), row), pl.BlockSpec((tm, 2 * C), row),
                   pl.BlockSpec((tm, 3 * D), row)],
        out_shape=[jax.ShapeDtypeStruct((T, 3 * A), BF16), jax.ShapeDtypeStruct((T, K), F32),
                   jax.ShapeDtypeStruct((T, 3 * K), BF16), jax.ShapeDtypeStruct((T, 2 * C), BF16),
                   jax.ShapeDtypeStruct((T, 3 * D), BF16)],
        compiler_params=_params(1, VMEM_LIMIT),
        name="inproj",
    )(x2d, norm_g.reshape(1, D), w_in_bf)


def _attn_body(q_ref, k_ref, v_ref, gq_ref, gk_ref, o_ref, kb_ref, *, seq, blk, topk):
    nb = seq // blk
    dh = ATT_HEAD_DIM
    dh_sh = dh.bit_length() - 1
    scale = dh ** -0.5
    lane = lax.broadcasted_iota(jnp.int32, (1, LANES), 1)
    hr = lax.broadcasted_iota(jnp.int32, (LANES, LANES), 0) >> dh_sh
    hc = lax.broadcasted_iota(jnp.int32, (LANES, LANES), 1) >> dh_sh
    same_head = jnp.where(hr == hc, 1.0, 0.0).astype(BF16)

    def head_norm(x, g):
        x2 = x * x
        hi = x2.astype(BF16)
        lo = (x2 - hi.astype(F32)).astype(BF16)
        ssq = _dot(hi, same_head) + _dot(lo, same_head)
        return x * lax.rsqrt(ssq * (1.0 / dh) + RMS_EPS) * g

    kmeans = []
    for i in range(nb):
        kn = head_norm(k_ref[i * blk:(i + 1) * blk, :].astype(F32), gk_ref[...])
        kb_ref[i * blk:(i + 1) * blk, :] = kn.astype(BF16)
        kmeans.append(jnp.mean(kn, axis=0, keepdims=True))
    kmean = jnp.concatenate(kmeans, axis=0)
    km_rep = jnp.concatenate([kmean] * (LANES // nb), axis=0)
    km_hi = km_rep.astype(BF16)
    km_lo = (km_rep - km_hi.astype(F32)).astype(BF16)
    lane_blk = lane & (nb - 1)
    first_head = lane < dh

    row_l = lax.broadcasted_iota(jnp.int32, (2 * blk, blk), 0) & (blk - 1)
    col_l = lax.broadcasted_iota(jnp.int32, (2 * blk, blk), 1)
    causal = col_l <= row_l

    for i in range(nb):
        rows = head_norm(q_ref[i * blk:(i + 1) * blk, :].astype(F32), gq_ref[...])
        qi = jnp.concatenate([jnp.where(first_head, rows, 0.0), jnp.where(first_head, 0.0, rows)], axis=0)
        qs = (qi * scale).astype(BF16)

        s = jnp.where(causal, _dot_nt(qs, kb_ref[i * blk:(i + 1) * blk, :]), NEG_INF)
        m = jnp.max(s, axis=-1, keepdims=True)
        p = jnp.exp(s - m)
        l = jnp.sum(p, axis=-1, keepdims=True)
        acc = _dot(p.astype(BF16), v_ref[i * blk:(i + 1) * blk, :])

        sel = None
        if i > topk:
            q_hi = qi.astype(BF16)
            q_lo = (qi - q_hi.astype(F32)).astype(BF16)
            gate = _dot_nt(q_hi, km_hi) + _dot_nt(q_hi, km_lo) + _dot_nt(q_lo, km_hi)
            valid = lane_blk < i
            gm = jnp.where(valid, gate, NEG_INF)
            rank = jnp.zeros(gm.shape, F32)
            for r in range(1, nb):
                gr = pltpu.roll(gm, r, 1)
                ahead = (gr > gm) | ((gr == gm) & (lane_blk >= r))
                rank = rank + jnp.where(ahead, 1.0, 0.0)
            sel = jnp.where(valid & (rank < topk), 1.0, 0.0)

        for j in range(i):
            s = _dot_nt(qs, kb_ref[j * blk:(j + 1) * blk, :])
            if sel is not None:
                s = jnp.where(sel[:, j:j + 1] > 0.5, s, NEG_INF)
            m_new = jnp.maximum(m, jnp.max(s, axis=-1, keepdims=True))
            alpha = jnp.exp(m - m_new)
            p = jnp.exp(s - m_new)
            l = alpha * l + jnp.sum(p, axis=-1, keepdims=True)
            acc = alpha * acc + _dot(p.astype(BF16), v_ref[j * blk:(j + 1) * blk, :])
            m = m_new

        o = acc / l
        o_ref[i * blk:(i + 1) * blk, :] = jnp.where(first_head, o[:blk], o[blk:]).astype(o_ref.dtype)


def _attention(qkv, q_norm_g, k_norm_g, batch, seq):
    T = qkv.shape[0]
    A = ATT_HEADS * ATT_HEAD_DIM
    npair = A // LANES
    reps = LANES // ATT_HEAD_DIM
    gq = jnp.tile(q_norm_g.astype(F32), reps).reshape(1, LANES)
    gk = jnp.tile(k_norm_g.astype(F32), reps).reshape(1, LANES)
    blk_spec = lambda off: pl.BlockSpec((seq, LANES), lambda b, p: (b, off + p))
    vec = pl.BlockSpec((1, LANES), lambda b, p: (0, 0))
    return pl.pallas_call(
        functools.partial(_attn_body, seq=seq, blk=MOBA_BLOCK, topk=MOBA_TOPK),
        grid=(batch, npair),
        in_specs=[blk_spec(0), blk_spec(npair), blk_spec(2 * npair), vec, vec],
        out_specs=pl.BlockSpec((seq, LANES), lambda b, p: (b, p)),
        out_shape=jax.ShapeDtypeStruct((T, A), BF16),
        scratch_shapes=[pltpu.VMEM((seq, LANES), BF16)],
        compiler_params=_params(2, VMEM_LIMIT),
        name="moba_attn",
    )(qkv, qkv, qkv, gq, gk)


def _hgrn_body(q_ref, f_ref, i_ref, g_ref, lbl_ref, ng_ref, o_ref, st_ref, *, rows, layer):
    R = rows
    c = pl.program_id(2)

    @pl.when(c == 0)
    def _():
        st_ref[...] = jnp.zeros(st_ref.shape, F32)

    lbl = lbl_ref[...]
    e = jnp.exp(lbl - jnp.max(lbl, axis=0, keepdims=True))
    p = e / jnp.sum(e, axis=0, keepdims=True)
    lb = jnp.maximum(jnp.sum(p[0:layer + 1], axis=0, keepdims=True) - p[0:1], 0.0)

    sig = _sigmoid(f_ref[...])
    g = jnp.log(lb + (1.0 - lb) * sig)
    kin = (1.0 - lb) * (1.0 - sig)
    qa = _silu(q_ref[...].astype(F32))
    vb = i_ref[...]

    t_col = lax.broadcasted_iota(jnp.int32, (R, 1), 0)
    t_row = lax.broadcasted_iota(jnp.int32, (R, R), 0)
    s_col = lax.broadcasted_iota(jnp.int32, (R, R), 1)

    tri = jnp.where(s_col <= t_row, 1.0, 0.0).astype(BF16)
    g1, g2, g3 = _split3(g)
    b = _dot(tri, g1) + _dot(tri, g2) + _dot(tri, g3)

    scores = None
    m = R // 2
    while m >= 1:
        w = 2 * m
        if w >= 8:
            b3 = b.reshape(R // w, w, LANES)
            bref = jnp.broadcast_to(b3[:, m - 1:m, :], b3.shape).reshape(R, LANES)
        else:
            tl = t_col & (w - 1)
            bref = None
            for resid in range(w):
                shift = resid - (m - 1)
                cand = b if shift == 0 else pltpu.roll(b, shift % R, 0)
                bref = cand if bref is None else jnp.where(tl == resid, cand, bref)
        upper = (t_col & (w - 1)) >= m
        qt = jnp.where(upper, qa * jnp.exp(jnp.minimum(b - bref, 0.0)), 0.0).astype(BF16)
        kt = jnp.where(upper, 0.0, kin * jnp.exp(jnp.minimum(bref - b, 0.0))).astype(BF16)
        part = _dot_nt(qt, kt)
        if w < R:
            sh = w.bit_length() - 1
            part = jnp.where((t_row >> sh) == (s_col >> sh), part, 0.0)
        scores = part if scores is None else scores + part
        m //= 2

    st = st_ref[...]
    o = _dot_nt((qa * jnp.exp(b)).astype(BF16), st.astype(BF16))
    o = o + _dot(scores.astype(BF16), vb)
    o = o + jnp.sum(qa * kin, axis=-1, keepdims=True) * vb.astype(F32)

    b_end = b[R - 1:R, :]
    k_end = (kin * jnp.exp(b_end - b)).astype(BF16)
    st_ref[...] = st * jnp.exp(b_end) + _dot_tn(vb, k_end)

    on = o * lax.rsqrt(jnp.mean(o * o, axis=-1, keepdims=True) + RMS_EPS) * ng_ref[...]
    o_ref[...] = (on * _silu(g_ref[...].astype(F32))).astype(o_ref.dtype)


def _hgrn(hqig, hf, lb_logits, norm_g, layer, batch, seq):
    T = hf.shape[0]
    H = HGRN_HEADS
    R = HGRN_ROWS
    nc = seq // R
    L = lb_logits.shape[0]
    blk = lambda off: pl.BlockSpec((R, LANES), lambda b, h, c: (b * nc + c, off + h))
    return pl.pallas_call(
        functools.partial(_hgrn_body, rows=R, layer=layer),
        grid=(batch, H, nc),
        in_specs=[blk(0), blk(0), blk(H), blk(2 * H),
                  pl.BlockSpec((L, LANES), lambda b, h, c: (0, h)),
                  pl.BlockSpec((1, LANES), lambda b, h, c: (0, 0))],
        out_specs=blk(0),
        out_shape=jax.ShapeDtypeStruct((T, H * HGRN_DIM), BF16),
        scratch_shapes=[pltpu.VMEM((HGRN_DIM, HGRN_DIM), F32)],
        compiler_params=_params(3, VMEM_LIMIT),
        name="hgrn2",
    )(hqig, hf, hqig, hqig, lb_logits.astype(F32), norm_g.astype(F32).reshape(1, LANES))


def _conv_body(ab_ref, w_ref, bdw_ref, lg_ref, lb_ref, o_ref, ubuf, cbuf, *, rows, width, chans):
    R, W, C, H = rows, width, chans, CONV_HALO
    c = pl.program_id(1)

    @pl.when(c == 0)
    def _():
        ubuf[0:H, :] = jnp.zeros((H, C), F32)

    a = ab_ref[:, 0:C].astype(F32)
    gate = ab_ref[:, C:2 * C].astype(F32)
    ubuf[H:H + R, :] = a * _sigmoid(gate)

    rsub = 64
    for lb in range(0, C, LANES):
        for rb in range(0, R, rsub):
            acc = jnp.zeros((rsub, LANES), F32)
            for j in range(W):
                start = H - (W - 1) + j + rb
                acc = acc + w_ref[j:j + 1, lb:lb + LANES] * ubuf[start:start + rsub, lb:lb + LANES]
            cbuf[rb:rb + rsub, lb:lb + LANES] = acc

    uf = cbuf[...] + bdw_ref[...]
    mu = jnp.mean(uf, axis=-1, keepdims=True)
    d = uf - mu
    var = jnp.mean(d * d, axis=-1, keepdims=True)
    y = d * lax.rsqrt(var + LN_EPS) * lg_ref[...] + lb_ref[...]
    o_ref[...] = _silu(y).astype(o_ref.dtype)
    ubuf[0:H, :] = ubuf[R:R + H, :]


def _conv(cab, conv_w, conv_b, ln_g, ln_b, batch, seq):
    T = cab.shape[0]
    C = cab.shape[1] // 2
    W = conv_w.shape[0]
    R = CONV_ROWS
    nc = seq // R
    vec = pl.BlockSpec((1, C), lambda b, c: (0, 0))
    return pl.pallas_call(
        functools.partial(_conv_body, rows=R, width=W, chans=C),
        grid=(batch, nc),
        in_specs=[pl.BlockSpec((R, 2 * C), lambda b, c: (b * nc + c, 0)),
                  pl.BlockSpec((W, C), lambda b, c: (0, 0)), vec, vec, vec],
        out_specs=pl.BlockSpec((R, C), lambda b, c: (b * nc + c, 0)),
        out_shape=jax.ShapeDtypeStruct((T, C), BF16),
        scratch_shapes=[pltpu.VMEM((R + CONV_HALO, C), F32), pltpu.VMEM((R, C), F32)],
        compiler_params=_params(2, VMEM_LIMIT),
        name="conformer_conv",
    )(cab, conv_w.astype(F32), conv_b.astype(F32).reshape(1, C), ln_g.astype(F32).reshape(1, C),
      ln_b.astype(F32).reshape(1, C))


def _merge_body(att_ref, rec_ref, cv_ref, gl_ref, bg_ref, x_ref, wa_ref, wh_ref, wc_ref, wo_ref,
                fg_ref, wr_ref, br_ref, xo_ref, route_ref, mg_ref, *, n_groups, per_group):
    D = x_ref.shape[1]
    att, rec, cv = att_ref[...], rec_ref[...], cv_ref[...]
    cw = 256
    for c in range(0, D, cw):
        def gate(k):
            return _sigmoid(gl_ref[:, k * D + c:k * D + c + cw].astype(F32) + bg_ref[:, k * D + c:k * D + c + cw])
        merged = (gate(0) * _dot(att, wa_ref[:, c:c + cw]) + gate(1) * _dot(rec, wh_ref[:, c:c + cw])
                  + gate(2) * _dot(cv, wc_ref[:, c:c + cw]))
        mg_ref[:, c:c + cw] = merged.astype(BF16)
    xn = x_ref[...] + _dot(mg_ref[...], wo_ref[...])
    xo_ref[...] = xn

    h = xn * lax.rsqrt(jnp.mean(xn * xn, axis=-1, keepdims=True) + RMS_EPS) * fg_ref[...]
    h_hi = h.astype(BF16)
    h_lo = (h - h_hi.astype(F32)).astype(BF16)
    wr = wr_ref[...]
    w_hi = wr.astype(BF16)
    w_lo = (wr - w_hi.astype(F32)).astype(BF16)
    logits = _dot(h_hi, w_hi) + _dot(h_hi, w_lo) + _dot(h_lo, w_hi) + br_ref[...]

    G, E = n_groups, per_group
    lane = lax.broadcasted_iota(jnp.int32, logits.shape, 1)
    big = jnp.int32(LANES)
    is_c = lane < G
    cl = jnp.where(is_c, logits, NEG_INF)
    cm = jnp.max(cl, axis=-1, keepdims=True)
    grp = jnp.min(jnp.where(cl == cm, lane, big), axis=-1, keepdims=True)
    se = jnp.sum(jnp.exp(jnp.where(is_c, logits - cm, NEG_INF)), axis=-1, keepdims=True)
    p_top = 1.0 / se
    lo = G + grp * E
    fl = jnp.where((lane >= lo) & (lane < lo + E), logits, NEG_INF)
    m1 = jnp.max(fl, axis=-1, keepdims=True)
    i1 = jnp.min(jnp.where(fl == m1, lane, big), axis=-1, keepdims=True)
    fl2 = jnp.where(lane == i1, NEG_INF, fl)
    m2 = jnp.max(fl2, axis=-1, keepdims=True)
    i2 = jnp.min(jnp.where(fl2 == m2, lane, big), axis=-1, keepdims=True)
    t = jnp.exp(m2 - m1)
    w1 = p_top / (1.0 + t)
    w2 = p_top * t / (1.0 + t)
    e1 = (i1 - G).astype(F32)
    e2 = (i2 - G).astype(F32)
    route_ref[...] = jnp.where(lane == 0, e1, jnp.where(lane == 1, e2,
                               jnp.where(lane == 2, w1, jnp.where(lane == 3, w2, 0.0))))


def _merge(att, rec, cv, gl, b_gate, x2d, wa, wh, wc, wo, ffn_g, w_route, b_route, layer):
    T, D = x2d.shape
    tm = TOKEN_TILE
    row = lambda i: (i, 0)
    const2 = lambda i: (0, 0)
    wsel = lambda i: (layer, 0, 0)
    W = att.shape[1]
    return pl.pallas_call(
        functools.partial(_merge_body, n_groups=N_GROUPS, per_group=EXPERTS_PER_GROUP),
        grid=(T // tm,),
        in_specs=[pl.BlockSpec((tm, W), row), pl.BlockSpec((tm, W), row), pl.BlockSpec((tm, W), row),
                  pl.BlockSpec((tm, 3 * D), row), pl.BlockSpec((1, 3 * D), const2),
                  pl.BlockSpec((tm, D), row),
                  pl.BlockSpec((None, W, D), wsel), pl.BlockSpec((None, W, D), wsel),
                  pl.BlockSpec((None, W, D), wsel), pl.BlockSpec((None, D, D), wsel),
                  pl.BlockSpec((1, D), const2), pl.BlockSpec((D, LANES), const2),
                  pl.BlockSpec((1, LANES), const2)],
        out_specs=[pl.BlockSpec((tm, D), row), pl.BlockSpec((tm, LANES), row)],
        out_shape=[jax.ShapeDtypeStruct((T, D), F32), jax.ShapeDtypeStruct((T, LANES), F32)],
        scratch_shapes=[pltpu.VMEM((tm, D), BF16)],
        compiler_params=_params(1, VMEM_LIMIT),
        name="merge_route",
    )(att, rec, cv, gl, b_gate.astype(F32).reshape(1, 3 * D), x2d, wa, wh, wc, wo,
      ffn_g.astype(F32).reshape(1, D), w_route, b_route)


def _expert_body(te_ref, nu_ref, src_ref, x_hbm, fg_ref, win_ref, wout_ref, y_ref, xbuf, sem, *, tile):
    i = pl.program_id(0)
    ff = wout_ref.shape[0]

    @pl.when(i < nu_ref[0])
    def _():
        def row_copy(r):
            return pltpu.make_async_copy(x_hbm.at[pl.ds(src_ref[0, 0, r], 1), :],
                                         xbuf.at[pl.ds(r, 1), :], sem)

        def issue(r, carry):
            row_copy(r).start()
            return carry

        lax.fori_loop(0, tile, issue, 0)

        def drain(r, carry):
            row_copy(r).wait()
            return carry

        lax.fori_loop(0, tile, drain, 0)

        x = xbuf[...]
        h = (x * lax.rsqrt(jnp.mean(x * x, axis=-1, keepdims=True) + RMS_EPS) * fg_ref[...]).astype(BF16)
        gu = _dot(h, win_ref[...])
        act = (_silu(gu[:, 0:ff]) * gu[:, ff:2 * ff]).astype(BF16)
        y_ref[...] = _dot(act, wout_ref[...])

    @pl.when(i >= nu_ref[0])
    def _():
        y_ref[...] = jnp.zeros(y_ref.shape, F32)


def _experts(x2d, ffn_g, w_in_bf, w_out_bf, tile_expert, n_used, src_tok, layer, cap):
    T, D = x2d.shape
    tile = EXPERT_TILE
    nt = cap // tile
    ff = w_out_bf.shape[2]
    grid_spec = pltpu.PrefetchScalarGridSpec(
        num_scalar_prefetch=2,
        grid=(nt,),
        in_specs=[pl.BlockSpec((1, 1, tile), lambda i, te, nu: (i, 0, 0), memory_space=pltpu.SMEM),
                  pl.BlockSpec(memory_space=pl.ANY),
                  pl.BlockSpec((1, D), lambda i, te, nu: (0, 0)),
                  pl.BlockSpec((None, None, D, 2 * ff), lambda i, te, nu: (layer, te[i], 0, 0)),
                  pl.BlockSpec((None, None, ff, D), lambda i, te, nu: (layer, te[i], 0, 0))],
        out_specs=pl.BlockSpec((tile, D), lambda i, te, nu: (i, 0)),
        scratch_shapes=[pltpu.VMEM((tile, D), F32), pltpu.SemaphoreType.DMA(())],
    )
    return pl.pallas_call(
        functools.partial(_expert_body, tile=tile),
        grid_spec=grid_spec,
        out_shape=jax.ShapeDtypeStruct((cap, D), F32),
        compiler_params=_params(1, VMEM_LIMIT),
        name="expert_ffn",
    )(tile_expert, n_used, src_tok.reshape(nt, 1, tile), x2d, ffn_g.astype(F32).reshape(1, D),
      w_in_bf, w_out_bf)


def _combine_body(pos_ref, y_hbm, x_ref, route_ref, o_ref, ybuf, sem, *, tile):
    def row_copy(r):
        return pltpu.make_async_copy(y_hbm.at[pl.ds(pos_ref[0, 0, r], 1), :],
                                     ybuf.at[pl.ds(r, 1), :], sem)

    def issue(r, carry):
        row_copy(r).start()
        return carry

    lax.fori_loop(0, 2 * tile, issue, 0)

    def drain(r, carry):
        row_copy(r).wait()
        return carry

    lax.fori_loop(0, 2 * tile, drain, 0)

    route = route_ref[...]
    o_ref[...] = (x_ref[...] + route[:, 2:3] * ybuf[0:tile, :] + route[:, 3:4] * ybuf[tile:2 * tile, :])


def _combine(x2d, route, y_sorted, pos_tiles):
    T, D = x2d.shape
    tile = TOKEN_TILE
    nt = T // tile
    return pl.pallas_call(
        functools.partial(_combine_body, tile=tile),
        grid=(nt,),
        in_specs=[pl.BlockSpec((1, 1, 2 * tile), lambda i: (i, 0, 0), memory_space=pltpu.SMEM),
                  pl.BlockSpec(memory_space=pl.ANY),
                  pl.BlockSpec((tile, D), lambda i: (i, 0)),
                  pl.BlockSpec((tile, LANES), lambda i: (i, 0))],
        out_specs=pl.BlockSpec((tile, D), lambda i: (i, 0)),
        out_shape=jax.ShapeDtypeStruct((T, D), F32),
        scratch_shapes=[pltpu.VMEM((2 * tile, D), F32), pltpu.SemaphoreType.DMA(())],
        compiler_params=_params(1, VMEM_LIMIT),
        name="moe_combine",
    )(pos_tiles, y_sorted, x2d, route)


def _dispatch_plan(route, n_experts, tile, cap):
    T = route.shape[0]
    e = jnp.concatenate([route[:, 0], route[:, 1]]).astype(jnp.int32)
    onehot = (e[:, None] == jnp.arange(n_experts, dtype=jnp.int32)[None, :]).astype(jnp.int32)
    csum = jnp.cumsum(onehot, axis=0)
    rank = jnp.take_along_axis(csum, e[:, None], axis=1)[:, 0] - 1
    counts = csum[-1]
    padded = ((counts + tile - 1) // tile) * tile
    ends = jnp.cumsum(padded)
    pos = (ends - padded)[e] + rank
    nt = cap // tile
    n_used = (ends[-1] // tile).astype(jnp.int32)
    tile_ids = jnp.arange(nt, dtype=jnp.int32)
    te = jnp.searchsorted(ends, jnp.minimum(tile_ids, n_used - 1) * tile, side="right").astype(jnp.int32)
    te = jnp.minimum(te, n_experts - 1)
    tok = jnp.tile(jnp.arange(T, dtype=jnp.int32), 2)
    src = jnp.zeros((cap,), jnp.int32).at[pos].set(tok)
    ttile = TOKEN_TILE
    pos_tiles = jnp.concatenate([pos[:T].reshape(T // ttile, 1, ttile), pos[T:].reshape(T // ttile, 1, ttile)],
                                axis=2)
    return te, n_used.reshape(1), src, pos_tiles


def kernel(x, mix_norm_g, w_in, b_gate, q_norm_g, k_norm_g, lb_logits, hgrn_norm_g, conv_w, conv_b,
           conv_ln_g, conv_ln_b, w_att_o, w_hgrn_o, w_conv_o, w_out, ffn_norm_g, w_coarse, b_coarse,
           w_fine, b_fine, w_exp_in, w_exp_out):
    B, S, D = x.shape
    L = w_in.shape[0]
    T = B * S
    n_experts = w_exp_in.shape[1]
    cap = 2 * T + n_experts * EXPERT_TILE

    w_in_bf = w_in.astype(BF16)
    wa_bf, wh_bf, wc_bf, wo_bf = (w.astype(BF16) for w in (w_att_o, w_hgrn_o, w_conv_o, w_out))
    wei_bf = w_exp_in.astype(BF16)
    weo_bf = w_exp_out.astype(BF16)
    pad = LANES - N_GROUPS - n_experts
    w_route = jnp.concatenate([w_coarse, w_fine, jnp.zeros((L, D, pad), F32)], axis=-1).astype(F32)
    b_route = jnp.concatenate([b_coarse, b_fine, jnp.zeros((L, pad), F32)], axis=-1).astype(F32)

    x2d = x.reshape(T, D)
    for l in range(L):
        qkv, hf, hqig, cab, gl = _inproj(x2d, mix_norm_g[l], w_in_bf, l)
        att = _attention(qkv, q_norm_g[l], k_norm_g[l], B, S)
        rec = _hgrn(hqig, hf, lb_logits, hgrn_norm_g[l], l, B, S)
        cv = _conv(cab, conv_w[l], conv_b[l], conv_ln_g[l], conv_ln_b[l], B, S)
        x_mid, route = _merge(att, rec, cv, gl, b_gate[l], x2d, wa_bf, wh_bf, wc_bf, wo_bf,
                              ffn_norm_g[l], w_route[l], b_route[l].reshape(1, LANES), l)
        te, n_used, src, pos_tiles = _dispatch_plan(route, n_experts, EXPERT_TILE, cap)
        y_sorted = _experts(x_mid, ffn_norm_g[l], wei_bf, weo_bf, te, n_used, src, l, cap)
        x2d = _combine(x_mid, route, y_sorted, pos_tiles)
    return x2d.reshape(B, S, D)
```

```python
import functools

import jax
import jax.numpy as jnp
from jax import lax
from jax.experimental import pallas as pl
from jax.experimental.pallas import tpu as pltpu

F32 = jnp.float32
BF16 = jnp.bfloat16

LANES = 128
SUBLANES = 8
RMS_EPS = 1e-6
LN_EPS = 1e-5
NEG_INF = -1e30
LOG2_E = 1.4426950408889634

ATT_HEADS = 8
ATT_HEAD_DIM = 64
MOBA_BLOCK = 256
MOBA_TOPK = 3
HGRN_HEADS = 4
HGRN_DIM = 128
HGRN_ROWS = 256
CONV_WIDTH = 31
CONV_ROWS = 256
CONV_HALO = 32
N_GROUPS = 4
EXPERTS_PER_GROUP = 8
EXPERT_TILE = 256
TOKEN_TILE = 256
DMA_UNROLL = 8
VMEM_LIMIT = 56 * 1024 * 1024


def _params(n_axes, vmem=None):
    return pltpu.CompilerParams(dimension_semantics=("arbitrary",) * n_axes,
                                vmem_limit_bytes=vmem)


def _dot(a, b):
    return jnp.dot(a, b, preferred_element_type=F32)


def _dot_nt(a, b):
    return lax.dot_general(a, b, (((1,), (1,)), ((), ())), preferred_element_type=F32)


def _dot_tn(a, b):
    return lax.dot_general(a, b, (((0,), (0,)), ((), ())), preferred_element_type=F32)


def _split3(x):
    x1 = x.astype(BF16)
    r1 = x - x1.astype(F32)
    x2 = r1.astype(BF16)
    x3 = (r1 - x2.astype(F32)).astype(BF16)
    return x1, x2, x3


def _sigmoid(x):
    return 1.0 / (1.0 + jnp.exp(-x))


def _silu(x):
    return x * _sigmoid(x)


def _inproj_body(x_ref, g_ref, w_ref, qkv_ref, hf_ref, hqig_ref, cv_ref, gl_ref, *, segs, chunk):
    x = x_ref[...]
    ms = jnp.mean(x * x, axis=-1, keepdims=True)
    h = (x * lax.rsqrt(ms + RMS_EPS) * g_ref[...]).astype(BF16)
    refs = (qkv_ref, hf_ref, hqig_ref, cv_ref, gl_ref)
    for ridx, dst0, src0, width in segs:
        ref = refs[ridx]
        for c in range(0, width, chunk):
            y = _dot(h, w_ref[:, src0 + c:src0 + c + chunk])
            ref[:, dst0 + c:dst0 + c + chunk] = y.astype(ref.dtype)


def _inproj(x2d, norm_g, w_in_bf, layer):
    T, D = x2d.shape
    A = ATT_HEADS * ATT_HEAD_DIM
    K = HGRN_HEADS * HGRN_DIM
    C = A
    segs = ((0, 0, 0, 3 * A),
            (2, 0, 3 * A, K),
            (1, 0, 3 * A + K, K),
            (2, K, 3 * A + 2 * K, K),
            (2, 2 * K, 3 * A + 3 * K, K),
            (3, 0, 3 * A + 4 * K, 2 * C),
            (4, 0, 3 * A + 4 * K + 2 * C, 3 * D))
    ncols = w_in_bf.shape[-1]
    tm = TOKEN_TILE
    row = lambda i: (i, 0)
    return pl.pallas_call(
        functools.partial(_inproj_body, segs=segs, chunk=512),
        grid=(T // tm,),
        in_specs=[pl.BlockSpec((tm, D), row),
                  pl.BlockSpec((1, D), lambda i: (0, 0)),
                  pl.BlockSpec((None, D, ncols), lambda i: (layer, 0, 0))],
        out_specs=[pl.BlockSpec((tm, 3 * A), row), pl.BlockSpec((tm, K), row),
                   pl.BlockSpec((tm, 3 * K), row), pl.BlockSpec((tm, 2 * C), row),
                   pl.BlockSpec((tm, 3 * D), row)],
        out_shape=[jax.ShapeDtypeStruct((T, 3 * A), BF16), jax.ShapeDtypeStruct((T, K), F32),
                   jax.ShapeDtypeStruct((T, 3 * K), BF16), jax.ShapeDtypeStruct((T, 2 * C), BF16),
                   jax.ShapeDtypeStruct((T, 3 * D), BF16)],
        compiler_params=_params(1, VMEM_LIMIT),
        name="inproj",
    )(x2d, norm_g.reshape(1, D), w_in_bf)


def _attn_body(q_ref, k_ref, v_ref, gq_ref, gk_ref, o_ref, kb_ref, vt_ref, s_ref, *, seq, blk, topk):
    nb = seq // blk
    dh = ATT_HEAD_DIM
    dh_sh = dh.bit_length() - 1
    scale = dh ** -0.5 * LOG2_E
    lane = lax.broadcasted_iota(jnp.int32, (1, LANES), 1)
    hr = lax.broadcasted_iota(jnp.int32, (LANES, LANES), 0) >> dh_sh
    hc = lax.broadcasted_iota(jnp.int32, (LANES, LANES), 1) >> dh_sh
    same_head = jnp.where(hr == hc, 1.0, 0.0).astype(BF16)

    def head_norm(x, g):
        x2 = x * x
        hi = x2.astype(BF16)
        lo = (x2 - hi.astype(F32)).astype(BF16)
        ssq = _dot(hi, same_head) + _dot(lo, same_head)
        return x * lax.rsqrt(ssq * (1.0 / dh) + RMS_EPS) * g

    kmeans = []
    for j in range(nb):
        kn = head_norm(k_ref[j * blk:(j + 1) * blk, :].astype(F32), gk_ref[...])
        kb_ref[j * blk:(j + 1) * blk, :] = kn.astype(BF16)
        kmeans.append(jnp.mean(kn, axis=0, keepdims=True))
        vt_ref[:, j * blk:(j + 1) * blk] = v_ref[j * blk:(j + 1) * blk, :].astype(F32).T.astype(BF16)
    kmean = jnp.concatenate(kmeans, axis=0)
    km_hi = kmean.astype(BF16)
    km_lo = (kmean - km_hi.astype(F32)).astype(BF16)
    first_head = lane < dh
    first_head_t = lax.broadcasted_iota(jnp.int32, (LANES, 1), 0) < dh
    blk_n = lax.broadcasted_iota(jnp.int32, (nb, 1), 0)

    key_l = lax.broadcasted_iota(jnp.int32, (blk, 2 * blk), 0)
    qry_l = lax.broadcasted_iota(jnp.int32, (blk, 2 * blk), 1) & (blk - 1)
    causal = key_l <= qry_l

    for i in range(nb):
        rows = head_norm(q_ref[i * blk:(i + 1) * blk, :].astype(F32), gq_ref[...])
        qi = jnp.concatenate([jnp.where(first_head, rows, 0.0), jnp.where(first_head, 0.0, rows)], axis=0)
        qs = (qi * scale).astype(BF16)

        sel = None
        if i > topk:
            q_hi = qi.astype(BF16)
            q_lo = (qi - q_hi.astype(F32)).astype(BF16)
            gate = _dot_nt(km_hi, q_hi) + _dot_nt(km_lo, q_hi) + _dot_nt(km_hi, q_lo)
            valid = blk_n < i
            gm = jnp.where(valid, gate, NEG_INF)
            rank = jnp.zeros(gm.shape, F32)
            for r in range(1, nb):
                gr = pltpu.roll(gm, r, 0)
                ahead = (gr > gm) | ((gr == gm) & (blk_n >= r))
                rank = rank + jnp.where(ahead, 1.0, 0.0)
            sel = jnp.where(valid & (rank < topk), 1.0, 0.0)

        m = None
        for j in range(i + 1):
            s = _dot_nt(kb_ref[j * blk:(j + 1) * blk, :], qs)
            if j == i:
                s = jnp.where(causal, s, NEG_INF)
            elif sel is not None:
                s = jnp.where(sel[j:j + 1, :] > 0.5, s, NEG_INF)
            s_ref[j * blk:(j + 1) * blk, :] = s
            mj = jnp.max(s, axis=0, keepdims=True)
            m = mj if m is None else jnp.maximum(m, mj)

        l = None
        acc = None
        for j in range(i + 1):
            p = jnp.exp2(s_ref[j * blk:(j + 1) * blk, :] - m)
            lj = jnp.sum(p, axis=0, keepdims=True)
            aj = _dot(vt_ref[:, j * blk:(j + 1) * blk], p.astype(BF16))
            l = lj if l is None else l + lj
            acc = aj if acc is None else acc + aj

        ot = acc / l
        merged = jnp.where(first_head_t, ot[:, 0:blk], ot[:, blk:2 * blk])
        o_ref[i * blk:(i + 1) * blk, :] = merged.T.astype(o_ref.dtype)


def _attention(qkv, q_norm_g, k_norm_g, batch, seq):
    T = qkv.shape[0]
    A = ATT_HEADS * ATT_HEAD_DIM
    npair = A // LANES
    reps = LANES // ATT_HEAD_DIM
    gq = jnp.tile(q_norm_g.astype(F32), reps).reshape(1, LANES)
    gk = jnp.tile(k_norm_g.astype(F32), reps).reshape(1, LANES)
    blk_spec = lambda off: pl.BlockSpec((seq, LANES), lambda b, p: (b, off + p))
    vec = pl.BlockSpec((1, LANES), lambda b, p: (0, 0))
    return pl.pallas_call(
        functools.partial(_attn_body, seq=seq, blk=MOBA_BLOCK, topk=MOBA_TOPK),
        grid=(batch, npair),
        in_specs=[blk_spec(0), blk_spec(npair), blk_spec(2 * npair), vec, vec],
        out_specs=pl.BlockSpec((seq, LANES), lambda b, p: (b, p)),
        out_shape=jax.ShapeDtypeStruct((T, A), BF16),
        scratch_shapes=[pltpu.VMEM((seq, LANES), BF16), pltpu.VMEM((LANES, seq), BF16),
                        pltpu.VMEM((seq, 2 * MOBA_BLOCK), F32)],
        compiler_params=_params(2, VMEM_LIMIT),
        name="moba_attn",
    )(qkv, qkv, qkv, gq, gk)


def _hgrn_body(q_ref, f_ref, i_ref, g_ref, lbl_ref, ng_ref, o_ref, st_ref, *, rows, layer):
    R = rows
    c = pl.program_id(2)

    @pl.when(c == 0)
    def _():
        st_ref[...] = jnp.zeros(st_ref.shape, F32)

    lbl = lbl_ref[...]
    e = jnp.exp(lbl - jnp.max(lbl, axis=0, keepdims=True))
    p = e / jnp.sum(e, axis=0, keepdims=True)
    lb = jnp.maximum(jnp.sum(p[0:layer + 1], axis=0, keepdims=True) - p[0:1], 0.0)

    sig = _sigmoid(f_ref[...])
    g = jnp.log(lb + (1.0 - lb) * sig)
    kin = (1.0 - lb) * (1.0 - sig)
    qa = _silu(q_ref[...].astype(F32))
    vb = i_ref[...]

    t_col = lax.broadcasted_iota(jnp.int32, (R, 1), 0)
    t_row = lax.broadcasted_iota(jnp.int32, (R, R), 0)
    s_col = lax.broadcasted_iota(jnp.int32, (R, R), 1)

    tri = jnp.where(s_col <= t_row, 1.0, 0.0).astype(BF16)
    g1, g2, g3 = _split3(g)
    b = _dot(tri, g1) + _dot(tri, g2) + _dot(tri, g3)

    scores = None
    m = R // 2
    while m >= 1:
        w = 2 * m
        if w >= 8:
            b3 = b.reshape(R // w, w, LANES)
            bref = jnp.broadcast_to(b3[:, m - 1:m, :], b3.shape).reshape(R, LANES)
        else:
            tl = t_col & (w - 1)
            bref = None
            for resid in range(w):
                shift = resid - (m - 1)
                cand = b if shift == 0 else pltpu.roll(b, shift % R, 0)
                bref = cand if bref is None else jnp.where(tl == resid, cand, bref)
        upper = (t_col & (w - 1)) >= m
        qt = jnp.where(upper, qa * jnp.exp(jnp.minimum(b - bref, 0.0)), 0.0).astype(BF16)
        kt = jnp.where(upper, 0.0, kin * jnp.exp(jnp.minimum(bref - b, 0.0))).astype(BF16)
        part = _dot_nt(qt, kt)
        if w < R:
            sh = w.bit_length() - 1
            part = jnp.where((t_row >> sh) == (s_col >> sh), part, 0.0)
        scores = part if scores is None else scores + part
        m //= 2

    st = st_ref[...]
    o = _dot_nt((qa * jnp.exp(b)).astype(BF16), st.astype(BF16))
    o = o + _dot(scores.astype(BF16), vb)
    o = o + jnp.sum(qa * kin, axis=-1, keepdims=True) * vb.astype(F32)

    b_end = b[R - 1:R, :]
    k_end = (kin * jnp.exp(b_end - b)).astype(BF16)
    st_ref[...] = st * jnp.exp(b_end) + _dot_tn(vb, k_end)

    on = o * lax.rsqrt(jnp.mean(o * o, axis=-1, keepdims=True) + RMS_EPS) * ng_ref[...]
    o_ref[...] = (on * _silu(g_ref[...].astype(F32))).astype(o_ref.dtype)


def _hgrn(hqig, hf, lb_logits, norm_g, layer, batch, seq):
    T = hf.shape[0]
    H = HGRN_HEADS
    R = HGRN_ROWS
    nc = seq // R
    L = lb_logits.shape[0]
    blk = lambda off: pl.BlockSpec((R, LANES), lambda b, h, c: (b * nc + c, off + h))
    return pl.pallas_call(
        functools.partial(_hgrn_body, rows=R, layer=layer),
        grid=(batch, H, nc),
        in_specs=[blk(0), blk(0), blk(H), blk(2 * H),
                  pl.BlockSpec((L, LANES), lambda b, h, c: (0, h)),
                  pl.BlockSpec((1, LANES), lambda b, h, c: (0, 0))],
        out_specs=blk(0),
        out_shape=jax.ShapeDtypeStruct((T, H * HGRN_DIM), BF16),
        scratch_shapes=[pltpu.VMEM((HGRN_DIM, HGRN_DIM), F32)],
        compiler_params=_params(3, VMEM_LIMIT),
        name="hgrn2",
    )(hqig, hf, hqig, hqig, lb_logits.astype(F32), norm_g.astype(F32).reshape(1, LANES))


def _conv_body(ab_ref, w_ref, bdw_ref, lg_ref, lb_ref, o_ref, ubuf, cbuf, ush, *, rows, width, chans):
    R, W, C, H = rows, width, chans, CONV_HALO
    c = pl.program_id(1)

    @pl.when(c == 0)
    def _():
        ubuf[0:H, :] = jnp.zeros((H, C), F32)

    a = ab_ref[:, 0:C].astype(F32)
    gate = ab_ref[:, C:2 * C].astype(F32)
    ubuf[H:H + R, :] = a * _sigmoid(gate)

    n_sh = ush.shape[1]
    for s in range(1, SUBLANES):
        ush[s - 1] = ubuf[s:s + n_sh, :]

    rsub = 64
    for lb in range(0, C, LANES):
        for rb in range(0, R, rsub):
            acc = jnp.zeros((rsub, LANES), F32)
            for j in range(W):
                start = H - (W - 1) + j + rb
                s = start % SUBLANES
                a0 = start - s
                if s == 0:
                    u = ubuf[a0:a0 + rsub, lb:lb + LANES]
                else:
                    u = ush[s - 1, a0:a0 + rsub, lb:lb + LANES]
                acc = acc + w_ref[j:j + 1, lb:lb + LANES] * u
            cbuf[rb:rb + rsub, lb:lb + LANES] = acc

    uf = cbuf[...] + bdw_ref[...]
    mu = jnp.mean(uf, axis=-1, keepdims=True)
    d = uf - mu
    var = jnp.mean(d * d, axis=-1, keepdims=True)
    y = d * lax.rsqrt(var + LN_EPS) * lg_ref[...] + lb_ref[...]
    o_ref[...] = _silu(y).astype(o_ref.dtype)
    ubuf[0:H, :] = ubuf[R:R + H, :]


def _conv(cab, conv_w, conv_b, ln_g, ln_b, batch, seq):
    T = cab.shape[0]
    C = cab.shape[1] // 2
    W = conv_w.shape[0]
    R = CONV_ROWS
    nc = seq // R
    vec = pl.BlockSpec((1, C), lambda b, c: (0, 0))
    return pl.pallas_call(
        functools.partial(_conv_body, rows=R, width=W, chans=C),
        grid=(batch, nc),
        in_specs=[pl.BlockSpec((R, 2 * C), lambda b, c: (b * nc + c, 0)),
                  pl.BlockSpec((W, C), lambda b, c: (0, 0)), vec, vec, vec],
        out_specs=pl.BlockSpec((R, C), lambda b, c: (b * nc + c, 0)),
        out_shape=jax.ShapeDtypeStruct((T, C), BF16),
        scratch_shapes=[pltpu.VMEM((R + CONV_HALO, C), F32), pltpu.VMEM((R, C), F32),
                        pltpu.VMEM((SUBLANES - 1, R + CONV_HALO - SUBLANES, C), F32)],
        compiler_params=_params(2, VMEM_LIMIT),
        name="conformer_conv",
    )(cab, conv_w.astype(F32), conv_b.astype(F32).reshape(1, C), ln_g.astype(F32).reshape(1, C),
      ln_b.astype(F32).reshape(1, C))


def _merge_body(att_ref, rec_ref, cv_ref, gl_ref, bg_ref, x_ref, wa_ref, wh_ref, wc_ref, wo_ref,
                fg_ref, wr_ref, br_ref, xo_ref, route_ref, cnt_out_ref, mg_ref, cnt_ref, *, n_groups, per_group):
    D = x_ref.shape[1]
    att, rec, cv = att_ref[...], rec_ref[...], cv_ref[...]
    cw = 256
    for c in range(0, D, cw):
        def gate(k):
            return _sigmoid(gl_ref[:, k * D + c:k * D + c + cw].astype(F32) + bg_ref[:, k * D + c:k * D + c + cw])
        merged = (gate(0) * _dot(att, wa_ref[:, c:c + cw]) + gate(1) * _dot(rec, wh_ref[:, c:c + cw])
                  + gate(2) * _dot(cv, wc_ref[:, c:c + cw]))
        mg_ref[:, c:c + cw] = merged.astype(BF16)
    xn = x_ref[...] + _dot(mg_ref[...], wo_ref[...])
    xo_ref[...] = xn

    h = xn * lax.rsqrt(jnp.mean(xn * xn, axis=-1, keepdims=True) + RMS_EPS) * fg_ref[...]
    h_hi = h.astype(BF16)
    h_lo = (h - h_hi.astype(F32)).astype(BF16)
    wr = wr_ref[...]
    w_hi = wr.astype(BF16)
    w_lo = (wr - w_hi.astype(F32)).astype(BF16)
    logits = _dot(h_hi, w_hi) + _dot(h_hi, w_lo) + _dot(h_lo, w_hi) + br_ref[...]

    G, E = n_groups, per_group
    lane = lax.broadcasted_iota(jnp.int32, logits.shape, 1)
    big = jnp.int32(LANES)
    is_c = lane < G
    cl = jnp.where(is_c, logits, NEG_INF)
    cm = jnp.max(cl, axis=-1, keepdims=True)
    grp = jnp.min(jnp.where(cl == cm, lane, big), axis=-1, keepdims=True)
    se = jnp.sum(jnp.exp(jnp.where(is_c, logits - cm, NEG_INF)), axis=-1, keepdims=True)
    p_top = 1.0 / se
    lo = G + grp * E
    fl = jnp.where((lane >= lo) & (lane < lo + E), logits, NEG_INF)
    m1 = jnp.max(fl, axis=-1, keepdims=True)
    i1 = jnp.min(jnp.where(fl == m1, lane, big), axis=-1, keepdims=True)
    fl2 = jnp.where(lane == i1, NEG_INF, fl)
    m2 = jnp.max(fl2, axis=-1, keepdims=True)
    i2 = jnp.min(jnp.where(fl2 == m2, lane, big), axis=-1, keepdims=True)
    t = jnp.exp(m2 - m1)
    w1 = p_top / (1.0 + t)
    w2 = p_top * t / (1.0 + t)
    e1 = (i1 - G).astype(F32)
    e2 = (i2 - G).astype(F32)

    @pl.when(pl.program_id(0) == 0)
    def _():
        cnt_ref[...] = jnp.zeros(cnt_ref.shape, F32)

    tm = logits.shape[0]
    hit1 = lane == (i1 - G)
    hit2 = lane == (i2 - G)
    hits = jnp.where(hit1 | hit2, 1.0, 0.0)
    rr = lax.broadcasted_iota(jnp.int32, (tm, tm), 0)
    cc = lax.broadcasted_iota(jnp.int32, (tm, tm), 1)
    before = jnp.where(cc < rr, 1.0, 0.0).astype(BF16)
    prefix = _dot(before, hits.astype(BF16)) + cnt_ref[...]
    rank1 = jnp.sum(jnp.where(hit1, prefix, 0.0), axis=-1, keepdims=True)
    rank2 = jnp.sum(jnp.where(hit2, prefix, 0.0), axis=-1, keepdims=True)
    cnt = cnt_ref[...] + jnp.sum(hits, axis=0, keepdims=True)
    cnt_ref[...] = cnt
    cnt_out_ref[...] = cnt

    route_ref[...] = jnp.where(lane == 0, e1, jnp.where(lane == 1, e2,
                               jnp.where(lane == 2, w1, jnp.where(lane == 3, w2,
                               jnp.where(lane == 4, rank1, jnp.where(lane == 5, rank2, 0.0))))))


def _merge(att, rec, cv, gl, b_gate, x2d, wa, wh, wc, wo, ffn_g, w_route, b_route, layer):
    T, D = x2d.shape
    tm = TOKEN_TILE
    row = lambda i: (i, 0)
    const2 = lambda i: (0, 0)
    wsel = lambda i: (layer, 0, 0)
    W = att.shape[1]
    return pl.pallas_call(
        functools.partial(_merge_body, n_groups=N_GROUPS, per_group=EXPERTS_PER_GROUP),
        grid=(T // tm,),
        in_specs=[pl.BlockSpec((tm, W), row), pl.BlockSpec((tm, W), row), pl.BlockSpec((tm, W), row),
                  pl.BlockSpec((tm, 3 * D), row), pl.BlockSpec((1, 3 * D), const2),
                  pl.BlockSpec((tm, D), row),
                  pl.BlockSpec((None, W, D), wsel), pl.BlockSpec((None, W, D), wsel),
                  pl.BlockSpec((None, W, D), wsel), pl.BlockSpec((None, D, D), wsel),
                  pl.BlockSpec((1, D), const2), pl.BlockSpec((D, LANES), const2),
                  pl.BlockSpec((1, LANES), const2)],
        out_specs=[pl.BlockSpec((tm, D), row), pl.BlockSpec((tm, LANES), row),
                   pl.BlockSpec((1, LANES), const2)],
        out_shape=[jax.ShapeDtypeStruct((T, D), F32), jax.ShapeDtypeStruct((T, LANES), F32),
                   jax.ShapeDtypeStruct((1, LANES), F32)],
        scratch_shapes=[pltpu.VMEM((tm, D), BF16), pltpu.VMEM((1, LANES), F32)],
        compiler_params=_params(1, VMEM_LIMIT),
        name="merge_route",
    )(att, rec, cv, gl, b_gate.astype(F32).reshape(1, 3 * D), x2d, wa, wh, wc, wo,
      ffn_g.astype(F32).reshape(1, D), w_route, b_route)


def _bulk_wait(view, sem):
    pltpu.make_async_copy(view, view, sem).wait()


def _dispatch_body(ps_ref, pn_ref, nu_ref, pos_ref, x_ref, fg_ref, xs_hbm, hbuf, zbuf, sem, zsem, *,
                   tile, n_tiles, n_experts, n_sorted_tiles):
    i = pl.program_id(0)
    slot = i % 2
    ztile = zbuf.shape[0]

    def pad_copy(e, r):
        return pltpu.make_async_copy(zbuf.at[pl.ds(0, 1), :], xs_hbm.at[pl.ds(ps_ref[e] + r, 1), :], zsem)

    def tail_copy(t):
        return pltpu.make_async_copy(zbuf, xs_hbm.at[pl.ds(t * ztile, ztile), :], zsem)

    def wait_slot(s):
        _bulk_wait(hbuf.at[s], sem.at[s])
        _bulk_wait(hbuf.at[s], sem.at[s])

    @pl.when(i == 0)
    def _():
        zbuf[...] = jnp.zeros(zbuf.shape, F32)
        for e in range(n_experts):
            def start(r, carry, e=e):
                pad_copy(e, r).start()
                return carry
            lax.fori_loop(0, pn_ref[e], start, 0)

        def start_tail(t, carry):
            tail_copy(t).start()
            return carry
        lax.fori_loop(nu_ref[0], n_sorted_tiles, start_tail, 0)

    @pl.when(i >= 2)
    def _():
        wait_slot(slot)

    x = x_ref[...]
    hbuf[slot] = x * lax.rsqrt(jnp.mean(x * x, axis=-1, keepdims=True) + RMS_EPS) * fg_ref[...]

    def issue(r, carry):
        src = hbuf.at[slot, pl.ds(r, 1), :]
        pltpu.make_async_copy(src, xs_hbm.at[pl.ds(pos_ref[0, 0, r], 1), :], sem.at[slot]).start()
        pltpu.make_async_copy(src, xs_hbm.at[pl.ds(pos_ref[0, 0, tile + r], 1), :], sem.at[slot]).start()
        return carry

    lax.fori_loop(0, tile, issue, 0, unroll=DMA_UNROLL)

    @pl.when(i == n_tiles - 1)
    def _():
        wait_slot(slot)
        if n_tiles >= 2:
            wait_slot(1 - slot)
        for e in range(n_experts):
            def done(r, carry, e=e):
                pad_copy(e, r).wait()
                return carry
            lax.fori_loop(0, pn_ref[e], done, 0)

        def done_tail(t, carry):
            tail_copy(t).wait()
            return carry
        lax.fori_loop(nu_ref[0], n_sorted_tiles, done_tail, 0)


def _dispatch(x2d, ffn_g, pos_tiles, pad_start, pad_len, n_used, cap):
    T, D = x2d.shape
    tile = TOKEN_TILE
    nt = T // tile
    n_experts = pad_start.shape[0]
    grid_spec = pltpu.PrefetchScalarGridSpec(
        num_scalar_prefetch=3,
        grid=(nt,),
        in_specs=[pl.BlockSpec((1, 1, 2 * tile), lambda i, ps, pn, nu: (i, 0, 0), memory_space=pltpu.SMEM),
                  pl.BlockSpec((tile, D), lambda i, ps, pn, nu: (i, 0)),
                  pl.BlockSpec((1, D), lambda i, ps, pn, nu: (0, 0))],
        out_specs=pl.BlockSpec(memory_space=pl.ANY),
        scratch_shapes=[pltpu.VMEM((2, tile, D), F32), pltpu.VMEM((EXPERT_TILE, D), F32),
                        pltpu.SemaphoreType.DMA((2,)), pltpu.SemaphoreType.DMA(())],
    )
    return pl.pallas_call(
        functools.partial(_dispatch_body, tile=tile, n_tiles=nt, n_experts=n_experts,
                          n_sorted_tiles=cap // EXPERT_TILE),
        grid_spec=grid_spec,
        out_shape=jax.ShapeDtypeStruct((cap, D), F32),
        compiler_params=_params(1, VMEM_LIMIT),
        name="moe_dispatch",
    )(pad_start, pad_len, n_used, pos_tiles, x2d, ffn_g.astype(F32).reshape(1, D))


def _expert_body(te_ref, nu_ref, x_ref, win_ref, wout_ref, y_ref):
    ff = wout_ref.shape[0]

    @pl.when(pl.program_id(0) < nu_ref[0])
    def _():
        gu = _dot(x_ref[...].astype(BF16), win_ref[...])
        act = (_silu(gu[:, 0:ff]) * gu[:, ff:2 * ff]).astype(BF16)
        y_ref[...] = _dot(act, wout_ref[...])

    @pl.when(pl.program_id(0) >= nu_ref[0])
    def _():
        y_ref[...] = jnp.zeros(y_ref.shape, F32)


def _experts(x_sorted, w_in_bf, w_out_bf, tile_expert, n_used, layer):
    cap, D = x_sorted.shape
    tile = EXPERT_TILE
    nt = cap // tile
    ff = w_out_bf.shape[2]
    rows = lambda i, te, nu: (i, 0)
    grid_spec = pltpu.PrefetchScalarGridSpec(
        num_scalar_prefetch=2,
        grid=(nt,),
        in_specs=[pl.BlockSpec((tile, D), rows),
                  pl.BlockSpec((None, None, D, 2 * ff), lambda i, te, nu: (layer, te[i], 0, 0)),
                  pl.BlockSpec((None, None, ff, D), lambda i, te, nu: (layer, te[i], 0, 0))],
        out_specs=pl.BlockSpec((tile, D), rows),
    )
    return pl.pallas_call(
        _expert_body,
        grid_spec=grid_spec,
        out_shape=jax.ShapeDtypeStruct((cap, D), F32),
        compiler_params=_params(1, VMEM_LIMIT),
        name="expert_ffn",
    )(tile_expert, n_used, x_sorted, w_in_bf, w_out_bf)


def _combine_body(pos0_ref, posn_ref, y_hbm, x_ref, route_ref, o_ref, ybuf, sem, *, tile, n_tiles):
    i = pl.program_id(0)
    slot = i % 2

    def gather(pos_ref, s):
        def issue(r, carry):
            pltpu.make_async_copy(y_hbm.at[pl.ds(pos_ref[0, 0, r], 1), :],
                                  ybuf.at[s, pl.ds(r, 1), :], sem.at[s]).start()
            return carry
        lax.fori_loop(0, 2 * tile, issue, 0, unroll=DMA_UNROLL)

    @pl.when(i == 0)
    def _():
        gather(pos0_ref, 0)

    @pl.when(i + 1 < n_tiles)
    def _():
        gather(posn_ref, 1 - slot)

    _bulk_wait(ybuf.at[slot], sem.at[slot])
    route = route_ref[...]
    o_ref[...] = (x_ref[...] + route[:, 2:3] * ybuf[slot, 0:tile, :]
                  + route[:, 3:4] * ybuf[slot, tile:2 * tile, :])


def _combine(x2d, route, y_sorted, pos_tiles):
    T, D = x2d.shape
    tile = TOKEN_TILE
    nt = T // tile
    return pl.pallas_call(
        functools.partial(_combine_body, tile=tile, n_tiles=nt),
        grid=(nt,),
        in_specs=[pl.BlockSpec((1, 1, 2 * tile), lambda i: (0, 0, 0), memory_space=pltpu.SMEM),
                  pl.BlockSpec((1, 1, 2 * tile), lambda i: (jnp.minimum(i + 1, nt - 1), 0, 0),
                               memory_space=pltpu.SMEM),
                  pl.BlockSpec(memory_space=pl.ANY),
                  pl.BlockSpec((tile, D), lambda i: (i, 0)),
                  pl.BlockSpec((tile, LANES), lambda i: (i, 0))],
        out_specs=pl.BlockSpec((tile, D), lambda i: (i, 0)),
        out_shape=jax.ShapeDtypeStruct((T, D), F32),
        scratch_shapes=[pltpu.VMEM((2, 2 * tile, D), F32), pltpu.SemaphoreType.DMA((2,))],
        compiler_params=_params(1, VMEM_LIMIT),
        name="moe_combine",
    )(pos_tiles, pos_tiles, y_sorted, x2d, route)


def _dispatch_plan(route, counts, n_experts, tile, cap):
    T = route.shape[0]
    cnt = counts[0, :n_experts].astype(jnp.int32)
    padded = ((cnt + tile - 1) // tile) * tile
    ends = jnp.cumsum(padded)
    offs = ends - padded
    pos1 = offs[route[:, 0].astype(jnp.int32)] + route[:, 4].astype(jnp.int32)
    pos2 = offs[route[:, 1].astype(jnp.int32)] + route[:, 5].astype(jnp.int32)
    nt = cap // tile
    n_used = (ends[-1] // tile).astype(jnp.int32)
    first_row = jnp.minimum(jnp.arange(nt, dtype=jnp.int32), n_used - 1) * tile
    te = jnp.sum((ends[None, :] <= first_row[:, None]).astype(jnp.int32), axis=1)
    te = jnp.minimum(te, n_experts - 1).astype(jnp.int32)
    ttile = TOKEN_TILE
    pos_tiles = jnp.concatenate([pos1.reshape(T // ttile, 1, ttile), pos2.reshape(T // ttile, 1, ttile)], axis=2)
    return te, n_used.reshape(1), pos_tiles, (offs + cnt).astype(jnp.int32), (padded - cnt).astype(jnp.int32)


def kernel(x, mix_norm_g, w_in, b_gate, q_norm_g, k_norm_g, lb_logits, hgrn_norm_g, conv_w, conv_b,
           conv_ln_g, conv_ln_b, w_att_o, w_hgrn_o, w_conv_o, w_out, ffn_norm_g, w_coarse, b_coarse,
           w_fine, b_fine, w_exp_in, w_exp_out):
    B, S, D = x.shape
    L = w_in.shape[0]
    T = B * S
    n_experts = w_exp_in.shape[1]
    cap = 2 * T + n_experts * EXPERT_TILE

    w_in_bf = w_in.astype(BF16)
    wa_bf, wh_bf, wc_bf, wo_bf = (w.astype(BF16) for w in (w_att_o, w_hgrn_o, w_conv_o, w_out))
    wei_bf = w_exp_in.astype(BF16)
    weo_bf = w_exp_out.astype(BF16)
    pad = LANES - N_GROUPS - n_experts
    w_route = jnp.concatenate([w_coarse, w_fine, jnp.zeros((L, D, pad), F32)], axis=-1).astype(F32)
    b_route = jnp.concatenate([b_coarse, b_fine, jnp.zeros((L, pad), F32)], axis=-1).astype(F32)

    x2d = x.reshape(T, D)
    for l in range(L):
        qkv, hf, hqig, cab, gl = _inproj(x2d, mix_norm_g[l], w_in_bf, l)
        att = _attention(qkv, q_norm_g[l], k_norm_g[l], B, S)
        rec = _hgrn(hqig, hf, lb_logits, hgrn_norm_g[l], l, B, S)
        cv = _conv(cab, conv_w[l], conv_b[l], conv_ln_g[l], conv_ln_b[l], B, S)
        x_mid, route, counts = _merge(att, rec, cv, gl, b_gate[l], x2d, wa_bf, wh_bf, wc_bf, wo_bf,
                                      ffn_norm_g[l], w_route[l], b_route[l].reshape(1, LANES), l)
        te, n_used, pos_tiles, pad_start, pad_len = _dispatch_plan(route, counts, n_experts, EXPERT_TILE, cap)
        x_sorted = _dispatch(x_mid, ffn_norm_g[l], pos_tiles, pad_start, pad_len, n_used, cap)
        y_sorted = _experts(x_sorted, wei_bf, weo_bf, te, n_used, l)
        x2d = _combine(x_mid, route, y_sorted, pos_tiles)
    return x2d.reshape(B, S, D)
```

```python
import functools

import jax
import jax.numpy as jnp
from jax import lax
from jax.experimental import pallas as pl
from jax.experimental.pallas import tpu as pltpu

F32 = jnp.float32
BF16 = jnp.bfloat16

LANES = 128
SUBLANES = 8
RMS_EPS = 1e-6
LN_EPS = 1e-5
NEG_INF = -1e30
LOG2_E = 1.4426950408889634

ATT_HEADS = 8
ATT_HEAD_DIM = 64
MOBA_BLOCK = 256
MOBA_TOPK = 3
HGRN_HEADS = 4
HGRN_DIM = 128
HGRN_ROWS = 256
CONV_WIDTH = 31
CONV_ROWS = 256
CONV_HALO = 32
N_GROUPS = 4
EXPERTS_PER_GROUP = 8
EXPERT_TILE = 256
TOKEN_TILE = 256
MERGE_TILE = 512
DMA_UNROLL = 8
VMEM_LIMIT = 56 * 1024 * 1024


def _params(n_axes, vmem=None):
    return pltpu.CompilerParams(dimension_semantics=("arbitrary",) * n_axes,
                                vmem_limit_bytes=vmem)


def _dot(a, b):
    return jnp.dot(a, b, preferred_element_type=F32)


def _dot_nt(a, b):
    return lax.dot_general(a, b, (((1,), (1,)), ((), ())), preferred_element_type=F32)


def _dot_tn(a, b):
    return lax.dot_general(a, b, (((0,), (0,)), ((), ())), preferred_element_type=F32)


def _split3(x):
    x1 = x.astype(BF16)
    r1 = x - x1.astype(F32)
    x2 = r1.astype(BF16)
    x3 = (r1 - x2.astype(F32)).astype(BF16)
    return x1, x2, x3


def _sigmoid(x):
    return 1.0 / (1.0 + jnp.exp(-x))


def _silu(x):
    return x * _sigmoid(x)


def _inproj_body(x_ref, g_ref, w_ref, qkv_ref, hf_ref, hqig_ref, cv_ref, gl_ref, *, segs, chunk):
    x = x_ref[...]
    ms = jnp.mean(x * x, axis=-1, keepdims=True)
    h = (x * lax.rsqrt(ms + RMS_EPS) * g_ref[...]).astype(BF16)
    refs = (qkv_ref, hf_ref, hqig_ref, cv_ref, gl_ref)
    for ridx, dst0, src0, width in segs:
        ref = refs[ridx]
        for c in range(0, width, chunk):
            y = _dot(h, w_ref[:, src0 + c:src0 + c + chunk])
            ref[:, dst0 + c:dst0 + c + chunk] = y.astype(ref.dtype)


def _inproj(x2d, norm_g, w_in_bf, layer):
    T, D = x2d.shape
    A = ATT_HEADS * ATT_HEAD_DIM
    K = HGRN_HEADS * HGRN_DIM
    C = A
    segs = ((0, 0, 0, 3 * A),
            (2, 0, 3 * A, K),
            (1, 0, 3 * A + K, K),
            (2, K, 3 * A + 2 * K, K),
            (2, 2 * K, 3 * A + 3 * K, K),
            (3, 0, 3 * A + 4 * K, 2 * C),
            (4, 0, 3 * A + 4 * K + 2 * C, 3 * D))
    ncols = w_in_bf.shape[-1]
    tm = TOKEN_TILE
    row = lambda i: (i, 0)
    return pl.pallas_call(
        functools.partial(_inproj_body, segs=segs, chunk=512),
        grid=(T // tm,),
        in_specs=[pl.BlockSpec((tm, D), row),
                  pl.BlockSpec((1, D), lambda i: (0, 0)),
                  pl.BlockSpec((None, D, ncols), lambda i: (layer, 0, 0))],
        out_specs=[pl.BlockSpec((tm, 3 * A), row), pl.BlockSpec((tm, K), row),
                   pl.BlockSpec((tm, 3 * K), row), pl.BlockSpec((tm, 2 * C), row),
                   pl.BlockSpec((tm, 3 * D), row)],
        out_shape=[jax.ShapeDtypeStruct((T, 3 * A), BF16), jax.ShapeDtypeStruct((T, K), F32),
                   jax.ShapeDtypeStruct((T, 3 * K), BF16), jax.ShapeDtypeStruct((T, 2 * C), BF16),
                   jax.ShapeDtypeStruct((T, 3 * D), BF16)],
        compiler_params=_params(1, VMEM_LIMIT),
        name="inproj",
    )(x2d, norm_g.reshape(1, D), w_in_bf)


def _attn_body(q_ref, k_ref, v_ref, gq_ref, gk_ref, o_ref, kb_ref, vt_ref, s_ref, *, seq, blk, topk):
    nb = seq // blk
    dh = ATT_HEAD_DIM
    dh_sh = dh.bit_length() - 1
    scale = dh ** -0.5 * LOG2_E
    lane = lax.broadcasted_iota(jnp.int32, (1, LANES), 1)
    hr = lax.broadcasted_iota(jnp.int32, (LANES, LANES), 0) >> dh_sh
    hc = lax.broadcasted_iota(jnp.int32, (LANES, LANES), 1) >> dh_sh
    same_head = jnp.where(hr == hc, 1.0, 0.0).astype(BF16)

    def head_norm(x, g):
        x2 = x * x
        hi = x2.astype(BF16)
        lo = (x2 - hi.astype(F32)).astype(BF16)
        ssq = _dot(hi, same_head) + _dot(lo, same_head)
        return x * lax.rsqrt(ssq * (1.0 / dh) + RMS_EPS) * g

    kmeans = []
    for j in range(nb):
        kn = head_norm(k_ref[j * blk:(j + 1) * blk, :].astype(F32), gk_ref[...])
        kb_ref[j * blk:(j + 1) * blk, :] = kn.astype(BF16)
        kmeans.append(jnp.mean(kn, axis=0, keepdims=True))
        vt_ref[:, j * blk:(j + 1) * blk] = v_ref[j * blk:(j + 1) * blk, :].astype(F32).T.astype(BF16)
    kmean = jnp.concatenate(kmeans, axis=0)
    km_hi = kmean.astype(BF16)
    km_lo = (kmean - km_hi.astype(F32)).astype(BF16)
    first_head = lane < dh
    first_head_t = lax.broadcasted_iota(jnp.int32, (LANES, 1), 0) < dh
    blk_n = lax.broadcasted_iota(jnp.int32, (nb, 1), 0)

    key_l = lax.broadcasted_iota(jnp.int32, (blk, 2 * blk), 0)
    qry_l = lax.broadcasted_iota(jnp.int32, (blk, 2 * blk), 1) & (blk - 1)
    causal = key_l <= qry_l

    for i in range(nb):
        rows = head_norm(q_ref[i * blk:(i + 1) * blk, :].astype(F32), gq_ref[...])
        qi = jnp.concatenate([jnp.where(first_head, rows, 0.0), jnp.where(first_head, 0.0, rows)], axis=0)
        qs = (qi * scale).astype(BF16)

        sel = None
        if i > topk:
            q_hi = qi.astype(BF16)
            q_lo = (qi - q_hi.astype(F32)).astype(BF16)
            gate = _dot_nt(km_hi, q_hi) + _dot_nt(km_lo, q_hi) + _dot_nt(km_hi, q_lo)
            valid = blk_n < i
            gm = jnp.where(valid, gate, NEG_INF)
            rank = jnp.zeros(gm.shape, F32)
            for r in range(1, nb):
                gr = pltpu.roll(gm, r, 0)
                ahead = (gr > gm) | ((gr == gm) & (blk_n >= r))
                rank = rank + jnp.where(ahead, 1.0, 0.0)
            sel = jnp.where(valid & (rank < topk), 1.0, 0.0)

        m = None
        for j in range(i + 1):
            s = _dot_nt(kb_ref[j * blk:(j + 1) * blk, :], qs)
            if j == i:
                s = jnp.where(causal, s, NEG_INF)
            elif sel is not None:
                s = jnp.where(sel[j:j + 1, :] > 0.5, s, NEG_INF)
            s_ref[j * blk:(j + 1) * blk, :] = s
            mj = jnp.max(s, axis=0, keepdims=True)
            m = mj if m is None else jnp.maximum(m, mj)

        l = None
        acc = None
        for j in range(i + 1):
            p = jnp.exp2(s_ref[j * blk:(j + 1) * blk, :] - m)
            lj = jnp.sum(p, axis=0, keepdims=True)
            aj = _dot(vt_ref[:, j * blk:(j + 1) * blk], p.astype(BF16))
            l = lj if l is None else l + lj
            acc = aj if acc is None else acc + aj

        ot = acc / l
        merged = jnp.where(first_head_t, ot[:, 0:blk], ot[:, blk:2 * blk])
        o_ref[i * blk:(i + 1) * blk, :] = merged.T.astype(o_ref.dtype)


def _attention(qkv, q_norm_g, k_norm_g, batch, seq):
    T = qkv.shape[0]
    A = ATT_HEADS * ATT_HEAD_DIM
    npair = A // LANES
    reps = LANES // ATT_HEAD_DIM
    gq = jnp.tile(q_norm_g.astype(F32), reps).reshape(1, LANES)
    gk = jnp.tile(k_norm_g.astype(F32), reps).reshape(1, LANES)
    blk_spec = lambda off: pl.BlockSpec((seq, LANES), lambda b, p: (b, off + p))
    vec = pl.BlockSpec((1, LANES), lambda b, p: (0, 0))
    return pl.pallas_call(
        functools.partial(_attn_body, seq=seq, blk=MOBA_BLOCK, topk=MOBA_TOPK),
        grid=(batch, npair),
        in_specs=[blk_spec(0), blk_spec(npair), blk_spec(2 * npair), vec, vec],
        out_specs=pl.BlockSpec((seq, LANES), lambda b, p: (b, p)),
        out_shape=jax.ShapeDtypeStruct((T, A), BF16),
        scratch_shapes=[pltpu.VMEM((seq, LANES), BF16), pltpu.VMEM((LANES, seq), BF16),
                        pltpu.VMEM((seq, 2 * MOBA_BLOCK), F32)],
        compiler_params=_params(2, VMEM_LIMIT),
        name="moba_attn",
    )(qkv, qkv, qkv, gq, gk)


def _hgrn_head(q_ref, f_ref, i_ref, g_ref, lbl_ref, ng_ref, tri_ref, lvl_ref, o_ref, st_ref, *,
               rows, layer, head):
    R = rows
    cols = slice(head * HGRN_DIM, (head + 1) * HGRN_DIM)

    lbl = lbl_ref[:, cols]
    e = jnp.exp(lbl - jnp.max(lbl, axis=0, keepdims=True))
    p = e / jnp.sum(e, axis=0, keepdims=True)
    lb = jnp.maximum(jnp.sum(p[0:layer + 1], axis=0, keepdims=True) - p[0:1], 0.0)

    sig = _sigmoid(f_ref[:, cols])
    g = jnp.log(lb + (1.0 - lb) * sig)
    kin = (1.0 - lb) * (1.0 - sig)
    qa = _silu(q_ref[:, cols].astype(F32))
    vb = i_ref[:, cols]

    t_col = lax.broadcasted_iota(jnp.int32, (R, 1), 0)

    tri = tri_ref[...]
    g1, g2, g3 = _split3(g)
    b = (_dot(tri, g1) + _dot(tri, g2) + _dot(tri, g3)) * LOG2_E

    half = R // 2
    scores = [None, None]
    cross = None
    m = half
    while m >= 1:
        w = 2 * m
        if w >= SUBLANES:
            b3 = b.reshape(R // w, w, LANES)
            bref = jnp.broadcast_to(b3[:, m - 1:m, :], b3.shape).reshape(R, LANES)
        else:
            tl = t_col & (w - 1)
            bref = None
            for resid in range(w):
                shift = resid - (m - 1)
                cand = b if shift == 0 else pltpu.roll(b, shift % R, 0)
                bref = cand if bref is None else jnp.where(tl == resid, cand, bref)
        e = jnp.exp2(-jnp.abs(b - bref))
        qt = (qa * e).astype(BF16)
        kt = (kin * e).astype(BF16)
        if w == R:
            cross = _dot_nt(qt[half:R], kt[0:half])
        else:
            keep = lvl_ref[...] == (m.bit_length() - 1)
            for hh in range(2):
                part = _dot_nt(qt[hh * half:(hh + 1) * half], kt[hh * half:(hh + 1) * half])
                scores[hh] = jnp.where(keep, part, 0.0 if scores[hh] is None else scores[hh])
        m //= 2

    st = st_ref[head]
    o = _dot_nt((qa * jnp.exp2(b)).astype(BF16), st.astype(BF16))
    o = o + jnp.sum(qa * kin, axis=-1, keepdims=True) * vb.astype(F32)
    o_lo = o[0:half] + _dot(scores[0].astype(BF16), vb[0:half])
    o_hi = o[half:R] + _dot(scores[1].astype(BF16), vb[half:R]) + _dot(cross.astype(BF16), vb[0:half])
    o = jnp.concatenate([o_lo, o_hi], axis=0)

    b_end = b[R - 1:R, :]
    k_end = (kin * jnp.exp2(b_end - b)).astype(BF16)
    st_ref[head] = st * jnp.exp2(b_end) + _dot_tn(vb, k_end)

    on = o * lax.rsqrt(jnp.mean(o * o, axis=-1, keepdims=True) + RMS_EPS) * ng_ref[...]
    o_ref[:, cols] = (on * _silu(g_ref[:, cols].astype(F32))).astype(o_ref.dtype)


def _hgrn_body(q_ref, f_ref, i_ref, g_ref, lbl_ref, ng_ref, tri_ref, lvl_ref, o_ref, st_ref, *, rows, layer):
    @pl.when(pl.program_id(1) == 0)
    def _():
        st_ref[...] = jnp.zeros(st_ref.shape, F32)

    for head in range(st_ref.shape[0]):
        _hgrn_head(q_ref, f_ref, i_ref, g_ref, lbl_ref, ng_ref, tri_ref, lvl_ref, o_ref, st_ref,
                   rows=rows, layer=layer, head=head)


def _level_table(n):
    t = jnp.arange(n, dtype=jnp.int32)[:, None]
    s = jnp.arange(n, dtype=jnp.int32)[None, :]
    x = t ^ s
    hb = jnp.zeros((n, n), jnp.int32)
    for k in range(1, n.bit_length()):
        hb = hb + (x >= (1 << k)).astype(jnp.int32)
    return jnp.where(t > s, hb, -1)


def _hgrn(hqig, hf, lb_logits, norm_g, layer, batch, seq):
    T = hf.shape[0]
    H = HGRN_HEADS
    R = HGRN_ROWS
    W = H * HGRN_DIM
    nc = seq // R
    L = lb_logits.shape[0]
    blk = lambda off: pl.BlockSpec((R, W), lambda b, c: (b * nc + c, off))
    const = lambda b, c: (0, 0)
    idx = jnp.arange(R, dtype=jnp.int32)
    tri = (idx[None, :] <= idx[:, None]).astype(BF16)
    return pl.pallas_call(
        functools.partial(_hgrn_body, rows=R, layer=layer),
        grid=(batch, nc),
        in_specs=[blk(0), blk(0), blk(1), blk(2),
                  pl.BlockSpec((L, W), const), pl.BlockSpec((1, LANES), const),
                  pl.BlockSpec((R, R), const), pl.BlockSpec((R // 2, R // 2), const)],
        out_specs=blk(0),
        out_shape=jax.ShapeDtypeStruct((T, W), BF16),
        scratch_shapes=[pltpu.VMEM((H, HGRN_DIM, HGRN_DIM), F32)],
        compiler_params=_params(2, VMEM_LIMIT),
        name="hgrn2",
    )(hqig, hf, hqig, hqig, lb_logits.astype(F32), norm_g.astype(F32).reshape(1, LANES), tri,
      _level_table(R // 2))


def _conv_body(ab_ref, w_ref, bdw_ref, lg_ref, lb_ref, o_ref, ubuf, cbuf, ush, *, rows, width, chans):
    R, W, C, H = rows, width, chans, CONV_HALO
    c = pl.program_id(1)

    @pl.when(c == 0)
    def _():
        ubuf[0:H, :] = jnp.zeros((H, C), F32)

    a = ab_ref[:, 0:C].astype(F32)
    gate = ab_ref[:, C:2 * C].astype(F32)
    ubuf[H:H + R, :] = a * _sigmoid(gate)

    n_sh = ush.shape[1]
    for s in range(1, SUBLANES):
        ush[s - 1] = ubuf[s:s + n_sh, :]

    rsub = 64
    for lb in range(0, C, LANES):
        for rb in range(0, R, rsub):
            acc = jnp.zeros((rsub, LANES), F32)
            for j in range(W):
                start = H - (W - 1) + j + rb
                s = start % SUBLANES
                a0 = start - s
                if s == 0:
                    u = ubuf[a0:a0 + rsub, lb:lb + LANES]
                else:
                    u = ush[s - 1, a0:a0 + rsub, lb:lb + LANES]
                acc = acc + w_ref[j:j + 1, lb:lb + LANES] * u
            cbuf[rb:rb + rsub, lb:lb + LANES] = acc

    uf = cbuf[...] + bdw_ref[...]
    mu = jnp.mean(uf, axis=-1, keepdims=True)
    d = uf - mu
    var = jnp.mean(d * d, axis=-1, keepdims=True)
    y = d * lax.rsqrt(var + LN_EPS) * lg_ref[...] + lb_ref[...]
    o_ref[...] = _silu(y).astype(o_ref.dtype)
    ubuf[0:H, :] = ubuf[R:R + H, :]


def _conv(cab, conv_w, conv_b, ln_g, ln_b, batch, seq):
    T = cab.shape[0]
    C = cab.shape[1] // 2
    W = conv_w.shape[0]
    R = CONV_ROWS
    nc = seq // R
    vec = pl.BlockSpec((1, C), lambda b, c: (0, 0))
    return pl.pallas_call(
        functools.partial(_conv_body, rows=R, width=W, chans=C),
        grid=(batch, nc),
        in_specs=[pl.BlockSpec((R, 2 * C), lambda b, c: (b * nc + c, 0)),
                  pl.BlockSpec((W, C), lambda b, c: (0, 0)), vec, vec, vec],
        out_specs=pl.BlockSpec((R, C), lambda b, c: (b * nc + c, 0)),
        out_shape=jax.ShapeDtypeStruct((T, C), BF16),
        scratch_shapes=[pltpu.VMEM((R + CONV_HALO, C), F32), pltpu.VMEM((R, C), F32),
                        pltpu.VMEM((SUBLANES - 1, R + CONV_HALO - SUBLANES, C), F32)],
        compiler_params=_params(2, VMEM_LIMIT),
        name="conformer_conv",
    )(cab, conv_w.astype(F32), conv_b.astype(F32).reshape(1, C), ln_g.astype(F32).reshape(1, C),
      ln_b.astype(F32).reshape(1, C))


def _merge_body(att_ref, rec_ref, cv_ref, gl_ref, bg_ref, x_ref, wa_ref, wh_ref, wc_ref, wo_ref,
                fg_ref, wr_ref, br_ref, before_ref, xo_ref, route_ref, cnt_out_ref, mg_ref, cnt_ref, *,
                n_groups, per_group):
    D = x_ref.shape[1]
    att, rec, cv = att_ref[...], rec_ref[...], cv_ref[...]
    cw = 256
    for c in range(0, D, cw):
        def gate(k):
            return _sigmoid(gl_ref[:, k * D + c:k * D + c + cw].astype(F32) + bg_ref[:, k * D + c:k * D + c + cw])
        merged = (gate(0) * _dot(att, wa_ref[:, c:c + cw]) + gate(1) * _dot(rec, wh_ref[:, c:c + cw])
                  + gate(2) * _dot(cv, wc_ref[:, c:c + cw]))
        mg_ref[:, c:c + cw] = merged.astype(BF16)
    xn = x_ref[...] + _dot(mg_ref[...], wo_ref[...])
    xo_ref[...] = xn

    h = xn * lax.rsqrt(jnp.mean(xn * xn, axis=-1, keepdims=True) + RMS_EPS) * fg_ref[...]
    h_hi = h.astype(BF16)
    h_lo = (h - h_hi.astype(F32)).astype(BF16)
    wr = wr_ref[...]
    w_hi = wr.astype(BF16)
    w_lo = (wr - w_hi.astype(F32)).astype(BF16)
    logits = _dot(h_hi, w_hi) + _dot(h_hi, w_lo) + _dot(h_lo, w_hi) + br_ref[...]

    G, E = n_groups, per_group
    lane = lax.broadcasted_iota(jnp.int32, logits.shape, 1)
    big = jnp.int32(LANES)
    is_c = lane < G
    cl = jnp.where(is_c, logits, NEG_INF)
    cm = jnp.max(cl, axis=-1, keepdims=True)
    grp = jnp.min(jnp.where(cl == cm, lane, big), axis=-1, keepdims=True)
    se = jnp.sum(jnp.exp(jnp.where(is_c, logits - cm, NEG_INF)), axis=-1, keepdims=True)
    p_top = 1.0 / se
    lo = G + grp * E
    fl = jnp.where((lane >= lo) & (lane < lo + E), logits, NEG_INF)
    m1 = jnp.max(fl, axis=-1, keepdims=True)
    i1 = jnp.min(jnp.where(fl == m1, lane, big), axis=-1, keepdims=True)
    fl2 = jnp.where(lane == i1, NEG_INF, fl)
    m2 = jnp.max(fl2, axis=-1, keepdims=True)
    i2 = jnp.min(jnp.where(fl2 == m2, lane, big), axis=-1, keepdims=True)
    t = jnp.exp(m2 - m1)
    w1 = p_top / (1.0 + t)
    w2 = p_top * t / (1.0 + t)
    e1 = (i1 - G).astype(F32)
    e2 = (i2 - G).astype(F32)

    @pl.when(pl.program_id(0) == 0)
    def _():
        cnt_ref[...] = jnp.zeros(cnt_ref.shape, F32)

    hit1 = lane == (i1 - G)
    hit2 = lane == (i2 - G)
    hits = jnp.where(hit1 | hit2, 1.0, 0.0)
    prefix = _dot(before_ref[...], hits.astype(BF16)) + cnt_ref[...]
    rank1 = jnp.sum(jnp.where(hit1, prefix, 0.0), axis=-1, keepdims=True)
    rank2 = jnp.sum(jnp.where(hit2, prefix, 0.0), axis=-1, keepdims=True)
    cnt = cnt_ref[...] + jnp.sum(hits, axis=0, keepdims=True)
    cnt_ref[...] = cnt
    cnt_out_ref[...] = cnt

    route_ref[...] = jnp.where(lane == 0, e1, jnp.where(lane == 1, e2,
                               jnp.where(lane == 2, w1, jnp.where(lane == 3, w2,
                               jnp.where(lane == 4, rank1, jnp.where(lane == 5, rank2, 0.0))))))


def _merge(att, rec, cv, gl, b_gate, x2d, wa, wh, wc, wo, ffn_g, w_route, b_route, layer):
    T, D = x2d.shape
    tm = MERGE_TILE
    row = lambda i: (i, 0)
    const2 = lambda i: (0, 0)
    wsel = lambda i: (layer, 0, 0)
    W = att.shape[1]
    idx = jnp.arange(tm, dtype=jnp.int32)
    before = (idx[None, :] < idx[:, None]).astype(BF16)
    return pl.pallas_call(
        functools.partial(_merge_body, n_groups=N_GROUPS, per_group=EXPERTS_PER_GROUP),
        grid=(T // tm,),
        in_specs=[pl.BlockSpec((tm, W), row), pl.BlockSpec((tm, W), row), pl.BlockSpec((tm, W), row),
                  pl.BlockSpec((tm, 3 * D), row), pl.BlockSpec((1, 3 * D), const2),
                  pl.BlockSpec((tm, D), row),
                  pl.BlockSpec((None, W, D), wsel), pl.BlockSpec((None, W, D), wsel),
                  pl.BlockSpec((None, W, D), wsel), pl.BlockSpec((None, D, D), wsel),
                  pl.BlockSpec((1, D), const2), pl.BlockSpec((D, LANES), const2),
                  pl.BlockSpec((1, LANES), const2), pl.BlockSpec((tm, tm), const2)],
        out_specs=[pl.BlockSpec((tm, D), row), pl.BlockSpec((tm, LANES), row),
                   pl.BlockSpec((1, LANES), const2)],
        out_shape=[jax.ShapeDtypeStruct((T, D), F32), jax.ShapeDtypeStruct((T, LANES), F32),
                   jax.ShapeDtypeStruct((1, LANES), F32)],
        scratch_shapes=[pltpu.VMEM((tm, D), BF16), pltpu.VMEM((1, LANES), F32)],
        compiler_params=_params(1, VMEM_LIMIT),
        name="merge_route",
    )(att, rec, cv, gl, b_gate.astype(F32).reshape(1, 3 * D), x2d, wa, wh, wc, wo,
      ffn_g.astype(F32).reshape(1, D), w_route, b_route, before)


def _bulk_wait(view, sem):
    pltpu.make_async_copy(view, view, sem).wait()


def _dispatch_body(ps_ref, pn_ref, nu_ref, pos_ref, x_ref, fg_ref, xs_hbm, hbuf, zbuf, sem, zsem, *,
                   tile, n_tiles, n_experts, n_sorted_tiles):
    i = pl.program_id(0)
    slot = i % 2
    ztile = zbuf.shape[0]

    def pad_copy(e, r):
        return pltpu.make_async_copy(zbuf.at[pl.ds(0, 1), :], xs_hbm.at[pl.ds(ps_ref[e] + r, 1), :], zsem)

    def tail_copy(t):
        return pltpu.make_async_copy(zbuf, xs_hbm.at[pl.ds(t * ztile, ztile), :], zsem)

    def wait_slot(s):
        _bulk_wait(hbuf.at[s], sem.at[s])
        _bulk_wait(hbuf.at[s], sem.at[s])

    @pl.when(i == 0)
    def _():
        zbuf[...] = jnp.zeros(zbuf.shape, F32)
        for e in range(n_experts):
            def start(r, carry, e=e):
                pad_copy(e, r).start()
                return carry
            lax.fori_loop(0, pn_ref[e], start, 0)

        def start_tail(t, carry):
            tail_copy(t).start()
            return carry
        lax.fori_loop(nu_ref[0], n_sorted_tiles, start_tail, 0)

    def scatter_tile(s):
        @pl.when(i >= 2)
        def _():
            wait_slot(s)

        x = x_ref[...]
        hbuf[s] = x * lax.rsqrt(jnp.mean(x * x, axis=-1, keepdims=True) + RMS_EPS) * fg_ref[...]

        def issue(r, carry):
            src = hbuf.at[s, pl.ds(r, 1), :]
            pltpu.make_async_copy(src, xs_hbm.at[pl.ds(pos_ref[0, 0, r], 1), :], sem.at[s]).start()
            pltpu.make_async_copy(src, xs_hbm.at[pl.ds(pos_ref[0, 0, tile + r], 1), :], sem.at[s]).start()
            return carry

        lax.fori_loop(0, tile, issue, 0, unroll=DMA_UNROLL)

    for s in range(2):
        pl.when(slot == s)(functools.partial(scatter_tile, s))

    @pl.when(i == n_tiles - 1)
    def _():
        wait_slot((n_tiles - 1) % 2)
        if n_tiles >= 2:
            wait_slot(n_tiles % 2)
        for e in range(n_experts):
            def done(r, carry, e=e):
                pad_copy(e, r).wait()
                return carry
            lax.fori_loop(0, pn_ref[e], done, 0)

        def done_tail(t, carry):
            tail_copy(t).wait()
            return carry
        lax.fori_loop(nu_ref[0], n_sorted_tiles, done_tail, 0)


def _dispatch(x2d, ffn_g, pos_tiles, pad_start, pad_len, n_used, cap):
    T, D = x2d.shape
    tile = TOKEN_TILE
    nt = T // tile
    n_experts = pad_start.shape[0]
    grid_spec = pltpu.PrefetchScalarGridSpec(
        num_scalar_prefetch=3,
        grid=(nt,),
        in_specs=[pl.BlockSpec((1, 1, 2 * tile), lambda i, ps, pn, nu: (i, 0, 0), memory_space=pltpu.SMEM),
                  pl.BlockSpec((tile, D), lambda i, ps, pn, nu: (i, 0)),
                  pl.BlockSpec((1, D), lambda i, ps, pn, nu: (0, 0))],
        out_specs=pl.BlockSpec(memory_space=pl.ANY),
        scratch_shapes=[pltpu.VMEM((2, tile, D), F32), pltpu.VMEM((EXPERT_TILE, D), F32),
                        pltpu.SemaphoreType.DMA((2,)), pltpu.SemaphoreType.DMA(())],
    )
    return pl.pallas_call(
        functools.partial(_dispatch_body, tile=tile, n_tiles=nt, n_experts=n_experts,
                          n_sorted_tiles=cap // EXPERT_TILE),
        grid_spec=grid_spec,
        out_shape=jax.ShapeDtypeStruct((cap, D), F32),
        compiler_params=_params(1, VMEM_LIMIT),
        name="moe_dispatch",
    )(pad_start, pad_len, n_used, pos_tiles, x2d, ffn_g.astype(F32).reshape(1, D))


def _expert_body(te_ref, nu_ref, x_ref, win_ref, wout_ref, y_ref, win_bf, wout_bf):
    i = pl.program_id(0)
    ff = wout_ref.shape[0]
    used = i < nu_ref[0]
    new_expert = jnp.logical_or(i == 0, te_ref[i] != te_ref[jnp.maximum(i - 1, 0)])

    @pl.when(jnp.logical_and(used, new_expert))
    def _():
        win_bf[...] = win_ref[...].astype(BF16)
        wout_bf[...] = wout_ref[...].astype(BF16)

    @pl.when(used)
    def _():
        gu = _dot(x_ref[...].astype(BF16), win_bf[...])
        act = (_silu(gu[:, 0:ff]) * gu[:, ff:2 * ff]).astype(BF16)
        y_ref[...] = _dot(act, wout_bf[...])

    @pl.when(jnp.logical_not(used))
    def _():
        y_ref[...] = jnp.zeros(y_ref.shape, F32)


def _experts(x_sorted, w_exp_in, w_exp_out, tile_expert, n_used, layer):
    cap, D = x_sorted.shape
    tile = EXPERT_TILE
    nt = cap // tile
    ff = w_exp_out.shape[2]
    rows = lambda i, te, nu: (i, 0)
    grid_spec = pltpu.PrefetchScalarGridSpec(
        num_scalar_prefetch=2,
        grid=(nt,),
        in_specs=[pl.BlockSpec((tile, D), rows),
                  pl.BlockSpec((None, None, D, 2 * ff), lambda i, te, nu: (layer, te[i], 0, 0)),
                  pl.BlockSpec((None, None, ff, D), lambda i, te, nu: (layer, te[i], 0, 0))],
        out_specs=pl.BlockSpec((tile, D), rows),
        scratch_shapes=[pltpu.VMEM((D, 2 * ff), BF16), pltpu.VMEM((ff, D), BF16)],
    )
    return pl.pallas_call(
        _expert_body,
        grid_spec=grid_spec,
        out_shape=jax.ShapeDtypeStruct((cap, D), F32),
        compiler_params=_params(1, VMEM_LIMIT),
        name="expert_ffn",
    )(tile_expert, n_used, x_sorted, w_exp_in, w_exp_out)


def _combine_body(pos0_ref, posn_ref, y_hbm, x_ref, route_ref, o_ref, ybuf, sem, *, tile, n_tiles):
    i = pl.program_id(0)
    slot = i % 2

    def gather(pos_ref, s):
        def issue(r, carry):
            pltpu.make_async_copy(y_hbm.at[pl.ds(pos_ref[0, 0, r], 1), :],
                                  ybuf.at[s, pl.ds(r, 1), :], sem.at[s]).start()
            return carry
        lax.fori_loop(0, 2 * tile, issue, 0, unroll=DMA_UNROLL)

    @pl.when(i == 0)
    def _():
        gather(pos0_ref, 0)

    def combine_tile(s):
        @pl.when(i + 1 < n_tiles)
        def _():
            gather(posn_ref, 1 - s)

        _bulk_wait(ybuf.at[s], sem.at[s])
        route = route_ref[...]
        o_ref[...] = (x_ref[...] + route[:, 2:3] * ybuf[s, 0:tile, :]
                      + route[:, 3:4] * ybuf[s, tile:2 * tile, :])

    for s in range(2):
        pl.when(slot == s)(functools.partial(combine_tile, s))


def _combine(x2d, route, y_sorted, pos_tiles):
    T, D = x2d.shape
    tile = TOKEN_TILE
    nt = T // tile
    return pl.pallas_call(
        functools.partial(_combine_body, tile=tile, n_tiles=nt),
        grid=(nt,),
        in_specs=[pl.BlockSpec((1, 1, 2 * tile), lambda i: (0, 0, 0), memory_space=pltpu.SMEM),
                  pl.BlockSpec((1, 1, 2 * tile), lambda i: (jnp.minimum(i + 1, nt - 1), 0, 0),
                               memory_space=pltpu.SMEM),
                  pl.BlockSpec(memory_space=pl.ANY),
                  pl.BlockSpec((tile, D), lambda i: (i, 0)),
                  pl.BlockSpec((tile, LANES), lambda i: (i, 0))],
        out_specs=pl.BlockSpec((tile, D), lambda i: (i, 0)),
        out_shape=jax.ShapeDtypeStruct((T, D), F32),
        scratch_shapes=[pltpu.VMEM((2, 2 * tile, D), F32), pltpu.SemaphoreType.DMA((2,))],
        compiler_params=_params(1, VMEM_LIMIT),
        name="moe_combine",
    )(pos_tiles, pos_tiles, y_sorted, x2d, route)


def _positions_body(route_ref, offs_ref, pos_ref, *, tile):
    r = route_ref[...]
    lane = lax.broadcasted_iota(jnp.int32, r.shape, 1)
    lane_f = lane.astype(F32)
    offs = offs_ref[...]
    pos1 = jnp.sum(jnp.where(lane_f == r[:, 0:1], offs, 0.0), axis=-1, keepdims=True) + r[:, 4:5]
    pos2 = jnp.sum(jnp.where(lane_f == r[:, 1:2], offs, 0.0), axis=-1, keepdims=True) + r[:, 5:6]
    both = jnp.where(lane == 0, pos1, jnp.where(lane == 1, pos2, 0.0))
    for k in range(r.shape[0] // tile):
        t = both[k * tile:(k + 1) * tile, :].T
        pos_ref[k, :, 0:tile] = t[0:1, :].astype(jnp.int32)
        pos_ref[k, :, tile:2 * tile] = t[1:2, :].astype(jnp.int32)


def _positions(route, offs_row):
    T = route.shape[0]
    tile = TOKEN_TILE
    step = 4 * tile
    return pl.pallas_call(
        functools.partial(_positions_body, tile=tile),
        grid=(T // step,),
        in_specs=[pl.BlockSpec((step, LANES), lambda i: (i, 0)),
                  pl.BlockSpec((1, LANES), lambda i: (0, 0))],
        out_specs=pl.BlockSpec((step // tile, 1, 2 * tile), lambda i: (i, 0, 0)),
        out_shape=jax.ShapeDtypeStruct((T // tile, 1, 2 * tile), jnp.int32),
        compiler_params=_params(1, VMEM_LIMIT),
        name="moe_positions",
    )(route, offs_row)


def _dispatch_plan(route, counts, n_experts, tile, cap):
    cnt = counts[0, :n_experts].astype(jnp.int32)
    padded = ((cnt + tile - 1) // tile) * tile
    ends = jnp.cumsum(padded)
    offs = ends - padded
    nt = cap // tile
    n_used = (ends[-1] // tile).astype(jnp.int32)
    first_row = jnp.minimum(jnp.arange(nt, dtype=jnp.int32), n_used - 1) * tile
    te = jnp.sum((ends[None, :] <= first_row[:, None]).astype(jnp.int32), axis=1)
    te = jnp.minimum(te, n_experts - 1).astype(jnp.int32)
    offs_row = jnp.pad(offs.astype(F32), (0, LANES - n_experts)).reshape(1, LANES)
    pos_tiles = _positions(route, offs_row)
    return te, n_used.reshape(1), pos_tiles, (offs + cnt).astype(jnp.int32), (padded - cnt).astype(jnp.int32)


def kernel(x, mix_norm_g, w_in, b_gate, q_norm_g, k_norm_g, lb_logits, hgrn_norm_g, conv_w, conv_b,
           conv_ln_g, conv_ln_b, w_att_o, w_hgrn_o, w_conv_o, w_out, ffn_norm_g, w_coarse, b_coarse,
           w_fine, b_fine, w_exp_in, w_exp_out):
    B, S, D = x.shape
    L = w_in.shape[0]
    T = B * S
    n_experts = w_exp_in.shape[1]
    cap = 2 * T + n_experts * EXPERT_TILE

    w_in_bf = w_in.astype(BF16)
    wa_bf, wh_bf, wc_bf, wo_bf = (w.astype(BF16) for w in (w_att_o, w_hgrn_o, w_conv_o, w_out))
    pad = LANES - N_GROUPS - n_experts
    w_route = jnp.concatenate([w_coarse, w_fine, jnp.zeros((L, D, pad), F32)], axis=-1).astype(F32)
    b_route = jnp.concatenate([b_coarse, b_fine, jnp.zeros((L, pad), F32)], axis=-1).astype(F32)

    x2d = x.reshape(T, D)
    for l in range(L):
        qkv, hf, hqig, cab, gl = _inproj(x2d, mix_norm_g[l], w_in_bf, l)
        att = _attention(qkv, q_norm_g[l], k_norm_g[l], B, S)
        rec = _hgrn(hqig, hf, lb_logits, hgrn_norm_g[l], l, B, S)
        cv = _conv(cab, conv_w[l], conv_b[l], conv_ln_g[l], conv_ln_b[l], B, S)
        x_mid, route, counts = _merge(att, rec, cv, gl, b_gate[l], x2d, wa_bf, wh_bf, wc_bf, wo_bf,
                                      ffn_norm_g[l], w_route[l], b_route[l].reshape(1, LANES), l)
        te, n_used, pos_tiles, pad_start, pad_len = _dispatch_plan(route, counts, n_experts, EXPERT_TILE, cap)
        x_sorted = _dispatch(x_mid, ffn_norm_g[l], pos_tiles, pad_start, pad_len, n_used, cap)
        y_sorted = _experts(x_sorted, w_exp_in, w_exp_out, te, n_used, l)
        x2d = _combine(x_mid, route, y_sorted, pos_tiles)
    return x2d.reshape(B, S, D)
```

```python
import functools

import jax
import jax.numpy as jnp
from jax import lax
from jax.experimental import pallas as pl
from jax.experimental.pallas import tpu as pltpu

F32 = jnp.float32
BF16 = jnp.bfloat16

LANES = 128
SUBLANES = 8
RMS_EPS = 1e-6
LN_EPS = 1e-5
NEG_INF = -1e30
LOG2_E = 1.4426950408889634

ATT_HEADS = 8
ATT_HEAD_DIM = 64
MOBA_BLOCK = 256
MOBA_TOPK = 3
HGRN_HEADS = 4
HGRN_DIM = 128
HGRN_ROWS = 256
CONV_WIDTH = 31
CONV_ROWS = 256
CONV_HALO = 32
N_GROUPS = 4
EXPERTS_PER_GROUP = 8
EXPERT_TILE = 256
TOKEN_TILE = 256
MERGE_TILE = 512
DMA_UNROLL = 8
VMEM_LIMIT = 56 * 1024 * 1024


def _params(n_axes, vmem=None):
    return pltpu.CompilerParams(dimension_semantics=("arbitrary",) * n_axes,
                                vmem_limit_bytes=vmem)


def _dot(a, b):
    return jnp.dot(a, b, preferred_element_type=F32)


def _dot_nt(a, b):
    return lax.dot_general(a, b, (((1,), (1,)), ((), ())), preferred_element_type=F32)


def _dot_tn(a, b):
    return lax.dot_general(a, b, (((0,), (0,)), ((), ())), preferred_element_type=F32)


def _split3(x):
    x1 = x.astype(BF16)
    r1 = x - x1.astype(F32)
    x2 = r1.astype(BF16)
    x3 = (r1 - x2.astype(F32)).astype(BF16)
    return x1, x2, x3


def _sigmoid(x):
    return 1.0 / (1.0 + jnp.exp(-x))


def _silu(x):
    return x * _sigmoid(x)


def _inproj_body(x_ref, g_ref, w_ref, qkv_ref, hf_ref, hqig_ref, cv_ref, gl_ref, *, segs, chunk):
    x = x_ref[...]
    ms = jnp.mean(x * x, axis=-1, keepdims=True)
    h = (x * lax.rsqrt(ms + RMS_EPS) * g_ref[...]).astype(BF16)
    refs = (qkv_ref, hf_ref, hqig_ref, cv_ref, gl_ref)
    for ridx, dst0, src0, width in segs:
        ref = refs[ridx]
        for c in range(0, width, chunk):
            y = _dot(h, w_ref[:, src0 + c:src0 + c + chunk])
            ref[:, dst0 + c:dst0 + c + chunk] = y.astype(ref.dtype)


def _inproj(x2d, norm_g, w_in_bf, layer):
    T, D = x2d.shape
    A = ATT_HEADS * ATT_HEAD_DIM
    K = HGRN_HEADS * HGRN_DIM
    C = A
    segs = ((0, 0, 0, 3 * A),
            (2, 0, 3 * A, K),
            (1, 0, 3 * A + K, K),
            (2, K, 3 * A + 2 * K, K),
            (2, 2 * K, 3 * A + 3 * K, K),
            (3, 0, 3 * A + 4 * K, 2 * C),
            (4, 0, 3 * A + 4 * K + 2 * C, 3 * D))
    ncols = w_in_bf.shape[-1]
    tm = TOKEN_TILE
    row = lambda i: (i, 0)
    return pl.pallas_call(
        functools.partial(_inproj_body, segs=segs, chunk=512),
        grid=(T // tm,),
        in_specs=[pl.BlockSpec((tm, D), row),
                  pl.BlockSpec((1, D), lambda i: (0, 0)),
                  pl.BlockSpec((None, D, ncols), lambda i: (layer, 0, 0))],
        out_specs=[pl.BlockSpec((tm, 3 * A), row), pl.BlockSpec((tm, K), row),
                   pl.BlockSpec((tm, 3 * K), row), pl.BlockSpec((tm, 2 * C), row),
                   pl.BlockSpec((tm, 3 * D), row)],
        out_shape=[jax.ShapeDtypeStruct((T, 3 * A), BF16), jax.ShapeDtypeStruct((T, K), F32),
                   jax.ShapeDtypeStruct((T, 3 * K), BF16), jax.ShapeDtypeStruct((T, 2 * C), BF16),
                   jax.ShapeDtypeStruct((T, 3 * D), BF16)],
        compiler_params=_params(1, VMEM_LIMIT),
        name="inproj",
    )(x2d, norm_g.reshape(1, D), w_in_bf)


def _attn_body(q_ref, k_ref, v_ref, gq_ref, gk_ref, o_ref, kb_ref, vt_ref, s_ref, *, seq, blk, topk):
    nb = seq // blk
    dh = ATT_HEAD_DIM
    dh_sh = dh.bit_length() - 1
    scale = dh ** -0.5 * LOG2_E
    lane = lax.broadcasted_iota(jnp.int32, (1, LANES), 1)
    hr = lax.broadcasted_iota(jnp.int32, (LANES, LANES), 0) >> dh_sh
    hc = lax.broadcasted_iota(jnp.int32, (LANES, LANES), 1) >> dh_sh
    same_head = jnp.where(hr == hc, 1.0, 0.0).astype(BF16)

    def head_norm(x, g):
        x2 = x * x
        hi = x2.astype(BF16)
        lo = (x2 - hi.astype(F32)).astype(BF16)
        ssq = _dot(hi, same_head) + _dot(lo, same_head)
        return x * lax.rsqrt(ssq * (1.0 / dh) + RMS_EPS) * g

    kmeans = []
    for j in range(nb):
        kn = head_norm(k_ref[j * blk:(j + 1) * blk, :].astype(F32), gk_ref[...])
        kb_ref[j * blk:(j + 1) * blk, :] = kn.astype(BF16)
        kmeans.append(jnp.mean(kn, axis=0, keepdims=True))
        vt_ref[:, j * blk:(j + 1) * blk] = v_ref[j * blk:(j + 1) * blk, :].astype(F32).T.astype(BF16)
    kmean = jnp.concatenate(kmeans, axis=0)
    km_hi = kmean.astype(BF16)
    km_lo = (kmean - km_hi.astype(F32)).astype(BF16)
    first_head = lane < dh
    first_head_t = lax.broadcasted_iota(jnp.int32, (LANES, 1), 0) < dh
    blk_n = lax.broadcasted_iota(jnp.int32, (nb, 1), 0)

    key_l = lax.broadcasted_iota(jnp.int32, (blk, 2 * blk), 0)
    qry_l = lax.broadcasted_iota(jnp.int32, (blk, 2 * blk), 1) & (blk - 1)
    causal = key_l <= qry_l

    for i in range(nb):
        rows = head_norm(q_ref[i * blk:(i + 1) * blk, :].astype(F32), gq_ref[...])
        qi = jnp.concatenate([jnp.where(first_head, rows, 0.0), jnp.where(first_head, 0.0, rows)], axis=0)
        qs = (qi * scale).astype(BF16)

        sel = None
        if i > topk:
            q_hi = qi.astype(BF16)
            q_lo = (qi - q_hi.astype(F32)).astype(BF16)
            gate = _dot_nt(km_hi, q_hi) + _dot_nt(km_lo, q_hi) + _dot_nt(km_hi, q_lo)
            valid = blk_n < i
            gm = jnp.where(valid, gate, NEG_INF)
            rank = jnp.zeros(gm.shape, F32)
            for r in range(1, nb):
                gr = pltpu.roll(gm, r, 0)
                ahead = (gr > gm) | ((gr == gm) & (blk_n >= r))
                rank = rank + jnp.where(ahead, 1.0, 0.0)
            sel = jnp.where(valid & (rank < topk), 1.0, 0.0)

        m = None
        for j in range(i + 1):
            s = _dot_nt(kb_ref[j * blk:(j + 1) * blk, :], qs)
            if j == i:
                s = jnp.where(causal, s, NEG_INF)
            elif sel is not None:
                s = jnp.where(sel[j:j + 1, :] > 0.5, s, NEG_INF)
            s_ref[j * blk:(j + 1) * blk, :] = s
            mj = jnp.max(s, axis=0, keepdims=True)
            m = mj if m is None else jnp.maximum(m, mj)

        l = None
        acc = None
        for j in range(i + 1):
            p = jnp.exp2(s_ref[j * blk:(j + 1) * blk, :] - m)
            lj = jnp.sum(p, axis=0, keepdims=True)
            aj = _dot(vt_ref[:, j * blk:(j + 1) * blk], p.astype(BF16))
            l = lj if l is None else l + lj
            acc = aj if acc is None else acc + aj

        ot = acc / l
        merged = jnp.where(first_head_t, ot[:, 0:blk], ot[:, blk:2 * blk])
        o_ref[i * blk:(i + 1) * blk, :] = merged.T.astype(o_ref.dtype)


def _attention(qkv, q_norm_g, k_norm_g, batch, seq):
    T = qkv.shape[0]
    A = ATT_HEADS * ATT_HEAD_DIM
    npair = A // LANES
    reps = LANES // ATT_HEAD_DIM
    gq = jnp.tile(q_norm_g.astype(F32), reps).reshape(1, LANES)
    gk = jnp.tile(k_norm_g.astype(F32), reps).reshape(1, LANES)
    blk_spec = lambda off: pl.BlockSpec((seq, LANES), lambda b, p: (b, off + p))
    vec = pl.BlockSpec((1, LANES), lambda b, p: (0, 0))
    return pl.pallas_call(
        functools.partial(_attn_body, seq=seq, blk=MOBA_BLOCK, topk=MOBA_TOPK),
        grid=(batch, npair),
        in_specs=[blk_spec(0), blk_spec(npair), blk_spec(2 * npair), vec, vec],
        out_specs=pl.BlockSpec((seq, LANES), lambda b, p: (b, p)),
        out_shape=jax.ShapeDtypeStruct((T, A), BF16),
        scratch_shapes=[pltpu.VMEM((seq, LANES), BF16), pltpu.VMEM((LANES, seq), BF16),
                        pltpu.VMEM((seq, 2 * MOBA_BLOCK), F32)],
        compiler_params=_params(2, VMEM_LIMIT),
        name="moba_attn",
    )(qkv, qkv, qkv, gq, gk)


def _hgrn_head(q_ref, f_ref, i_ref, g_ref, lbl_ref, ng_ref, tri_ref, lvl_ref, o_ref, st_ref, *,
               rows, layer, head):
    R = rows
    cols = slice(head * HGRN_DIM, (head + 1) * HGRN_DIM)

    lbl = lbl_ref[:, cols]
    e = jnp.exp(lbl - jnp.max(lbl, axis=0, keepdims=True))
    p = e / jnp.sum(e, axis=0, keepdims=True)
    lb = jnp.maximum(jnp.sum(p[0:layer + 1], axis=0, keepdims=True) - p[0:1], 0.0)

    sig = _sigmoid(f_ref[:, cols])
    g = jnp.log(lb + (1.0 - lb) * sig)
    kin = (1.0 - lb) * (1.0 - sig)
    qa = _silu(q_ref[:, cols].astype(F32))
    vb = i_ref[:, cols]

    t_col = lax.broadcasted_iota(jnp.int32, (R, 1), 0)

    tri = tri_ref[...]
    g1, g2, g3 = _split3(g)
    b = (_dot(tri, g1) + _dot(tri, g2) + _dot(tri, g3)) * LOG2_E

    half = R // 2
    scores = [None, None]
    cross = None
    m = half
    while m >= 1:
        w = 2 * m
        if w >= SUBLANES:
            b3 = b.reshape(R // w, w, LANES)
            bref = jnp.broadcast_to(b3[:, m - 1:m, :], b3.shape).reshape(R, LANES)
        else:
            tl = t_col & (w - 1)
            bref = None
            for resid in range(w):
                shift = resid - (m - 1)
                cand = b if shift == 0 else pltpu.roll(b, shift % R, 0)
                bref = cand if bref is None else jnp.where(tl == resid, cand, bref)
        e = jnp.exp2(-jnp.abs(b - bref))
        qt = (qa * e).astype(BF16)
        kt = (kin * e).astype(BF16)
        if w == R:
            cross = _dot_nt(qt[half:R], kt[0:half])
        else:
            keep = lvl_ref[...] == (m.bit_length() - 1)
            for hh in range(2):
                part = _dot_nt(qt[hh * half:(hh + 1) * half], kt[hh * half:(hh + 1) * half])
                scores[hh] = jnp.where(keep, part, 0.0 if scores[hh] is None else scores[hh])
        m //= 2

    st = st_ref[head]
    o = _dot_nt((qa * jnp.exp2(b)).astype(BF16), st.astype(BF16))
    o = o + jnp.sum(qa * kin, axis=-1, keepdims=True) * vb.astype(F32)
    o_lo = o[0:half] + _dot(scores[0].astype(BF16), vb[0:half])
    o_hi = o[half:R] + _dot(scores[1].astype(BF16), vb[half:R]) + _dot(cross.astype(BF16), vb[0:half])
    o = jnp.concatenate([o_lo, o_hi], axis=0)

    b_end = b[R - 1:R, :]
    k_end = (kin * jnp.exp2(b_end - b)).astype(BF16)
    st_ref[head] = st * jnp.exp2(b_end) + _dot_tn(vb, k_end)

    on = o * lax.rsqrt(jnp.mean(o * o, axis=-1, keepdims=True) + RMS_EPS) * ng_ref[...]
    o_ref[:, cols] = (on * _silu(g_ref[:, cols].astype(F32))).astype(o_ref.dtype)


def _hgrn_body(q_ref, f_ref, i_ref, g_ref, lbl_ref, ng_ref, tri_ref, lvl_ref, o_ref, st_ref, *, rows, layer):
    @pl.when(pl.program_id(1) == 0)
    def _():
        st_ref[...] = jnp.zeros(st_ref.shape, F32)

    for head in range(st_ref.shape[0]):
        _hgrn_head(q_ref, f_ref, i_ref, g_ref, lbl_ref, ng_ref, tri_ref, lvl_ref, o_ref, st_ref,
                   rows=rows, layer=layer, head=head)


def _level_table(n):
    t = jnp.arange(n, dtype=jnp.int32)[:, None]
    s = jnp.arange(n, dtype=jnp.int32)[None, :]
    x = t ^ s
    hb = jnp.zeros((n, n), jnp.int32)
    for k in range(1, n.bit_length()):
        hb = hb + (x >= (1 << k)).astype(jnp.int32)
    return jnp.where(t > s, hb, -1)


def _hgrn(hqig, hf, lb_logits, norm_g, layer, batch, seq):
    T = hf.shape[0]
    H = HGRN_HEADS
    R = HGRN_ROWS
    W = H * HGRN_DIM
    nc = seq // R
    L = lb_logits.shape[0]
    blk = lambda off: pl.BlockSpec((R, W), lambda b, c: (b * nc + c, off))
    const = lambda b, c: (0, 0)
    idx = jnp.arange(R, dtype=jnp.int32)
    tri = (idx[None, :] <= idx[:, None]).astype(BF16)
    return pl.pallas_call(
        functools.partial(_hgrn_body, rows=R, layer=layer),
        grid=(batch, nc),
        in_specs=[blk(0), blk(0), blk(1), blk(2),
                  pl.BlockSpec((L, W), const), pl.BlockSpec((1, LANES), const),
                  pl.BlockSpec((R, R), const), pl.BlockSpec((R // 2, R // 2), const)],
        out_specs=blk(0),
        out_shape=jax.ShapeDtypeStruct((T, W), BF16),
        scratch_shapes=[pltpu.VMEM((H, HGRN_DIM, HGRN_DIM), F32)],
        compiler_params=_params(2, VMEM_LIMIT),
        name="hgrn2",
    )(hqig, hf, hqig, hqig, lb_logits.astype(F32), norm_g.astype(F32).reshape(1, LANES), tri,
      _level_table(R // 2))


def _conv_body(ab_ref, w_ref, bdw_ref, lg_ref, lb_ref, o_ref, ubuf, cbuf, ush, *, rows, width, chans):
    R, W, C, H = rows, width, chans, CONV_HALO
    c = pl.program_id(1)

    @pl.when(c == 0)
    def _():
        ubuf[0:H, :] = jnp.zeros((H, C), F32)

    a = ab_ref[:, 0:C].astype(F32)
    gate = ab_ref[:, C:2 * C].astype(F32)
    ubuf[H:H + R, :] = a * _sigmoid(gate)

    n_sh = ush.shape[1]
    for s in range(1, SUBLANES):
        ush[s - 1] = ubuf[s:s + n_sh, :]

    rsub = 64
    for lb in range(0, C, LANES):
        for rb in range(0, R, rsub):
            acc = jnp.zeros((rsub, LANES), F32)
            for j in range(W):
                start = H - (W - 1) + j + rb
                s = start % SUBLANES
                a0 = start - s
                if s == 0:
                    u = ubuf[a0:a0 + rsub, lb:lb + LANES]
                else:
                    u = ush[s - 1, a0:a0 + rsub, lb:lb + LANES]
                acc = acc + w_ref[j:j + 1, lb:lb + LANES] * u
            cbuf[rb:rb + rsub, lb:lb + LANES] = acc

    uf = cbuf[...] + bdw_ref[...]
    mu = jnp.mean(uf, axis=-1, keepdims=True)
    d = uf - mu
    var = jnp.mean(d * d, axis=-1, keepdims=True)
    y = d * lax.rsqrt(var + LN_EPS) * lg_ref[...] + lb_ref[...]
    o_ref[...] = _silu(y).astype(o_ref.dtype)
    ubuf[0:H, :] = ubuf[R:R + H, :]


def _conv(cab, conv_w, conv_b, ln_g, ln_b, batch, seq):
    T = cab.shape[0]
    C = cab.shape[1] // 2
    W = conv_w.shape[0]
    R = CONV_ROWS
    nc = seq // R
    vec = pl.BlockSpec((1, C), lambda b, c: (0, 0))
    return pl.pallas_call(
        functools.partial(_conv_body, rows=R, width=W, chans=C),
        grid=(batch, nc),
        in_specs=[pl.BlockSpec((R, 2 * C), lambda b, c: (b * nc + c, 0)),
                  pl.BlockSpec((W, C), lambda b, c: (0, 0)), vec, vec, vec],
        out_specs=pl.BlockSpec((R, C), lambda b, c: (b * nc + c, 0)),
        out_shape=jax.ShapeDtypeStruct((T, C), BF16),
        scratch_shapes=[pltpu.VMEM((R + CONV_HALO, C), F32), pltpu.VMEM((R, C), F32),
                        pltpu.VMEM((SUBLANES - 1, R + CONV_HALO - SUBLANES, C), F32)],
        compiler_params=_params(2, VMEM_LIMIT),
        name="conformer_conv",
    )(cab, conv_w.astype(F32), conv_b.astype(F32).reshape(1, C), ln_g.astype(F32).reshape(1, C),
      ln_b.astype(F32).reshape(1, C))


def _merge_body(att_ref, rec_ref, cv_ref, gl_ref, bg_ref, x_ref, wa_ref, wh_ref, wc_ref, wo_ref,
                fg_ref, wr_ref, br_ref, before_ref, xo_ref, route_ref, cnt_out_ref, mg_ref, cnt_ref, *,
                n_groups, per_group):
    D = x_ref.shape[1]
    att, rec, cv = att_ref[...], rec_ref[...], cv_ref[...]
    cw = 256
    for c in range(0, D, cw):
        def gate(k):
            return _sigmoid(gl_ref[:, k * D + c:k * D + c + cw].astype(F32) + bg_ref[:, k * D + c:k * D + c + cw])
        merged = (gate(0) * _dot(att, wa_ref[:, c:c + cw]) + gate(1) * _dot(rec, wh_ref[:, c:c + cw])
                  + gate(2) * _dot(cv, wc_ref[:, c:c + cw]))
        mg_ref[:, c:c + cw] = merged.astype(BF16)
    xn = x_ref[...] + _dot(mg_ref[...], wo_ref[...])
    xo_ref[...] = xn

    h = xn * lax.rsqrt(jnp.mean(xn * xn, axis=-1, keepdims=True) + RMS_EPS) * fg_ref[...]
    h_hi = h.astype(BF16)
    h_lo = (h - h_hi.astype(F32)).astype(BF16)
    wr = wr_ref[...]
    w_hi = wr.astype(BF16)
    w_lo = (wr - w_hi.astype(F32)).astype(BF16)
    logits = _dot(h_hi, w_hi) + _dot(h_hi, w_lo) + _dot(h_lo, w_hi) + br_ref[...]

    G, E = n_groups, per_group
    lane = lax.broadcasted_iota(jnp.int32, logits.shape, 1)
    big = jnp.int32(LANES)
    is_c = lane < G
    cl = jnp.where(is_c, logits, NEG_INF)
    cm = jnp.max(cl, axis=-1, keepdims=True)
    grp = jnp.min(jnp.where(cl == cm, lane, big), axis=-1, keepdims=True)
    se = jnp.sum(jnp.exp(jnp.where(is_c, logits - cm, NEG_INF)), axis=-1, keepdims=True)
    p_top = 1.0 / se
    lo = G + grp * E
    fl = jnp.where((lane >= lo) & (lane < lo + E), logits, NEG_INF)
    m1 = jnp.max(fl, axis=-1, keepdims=True)
    i1 = jnp.min(jnp.where(fl == m1, lane, big), axis=-1, keepdims=True)
    fl2 = jnp.where(lane == i1, NEG_INF, fl)
    m2 = jnp.max(fl2, axis=-1, keepdims=True)
    i2 = jnp.min(jnp.where(fl2 == m2, lane, big), axis=-1, keepdims=True)
    t = jnp.exp(m2 - m1)
    w1 = p_top / (1.0 + t)
    w2 = p_top * t / (1.0 + t)
    e1 = (i1 - G).astype(F32)
    e2 = (i2 - G).astype(F32)

    @pl.when(pl.program_id(0) == 0)
    def _():
        cnt_ref[...] = jnp.zeros(cnt_ref.shape, F32)

    hit1 = lane == (i1 - G)
    hit2 = lane == (i2 - G)
    hits = jnp.where(hit1 | hit2, 1.0, 0.0)
    prefix = _dot(before_ref[...], hits.astype(BF16)) + cnt_ref[...]
    rank1 = jnp.sum(jnp.where(hit1, prefix, 0.0), axis=-1, keepdims=True)
    rank2 = jnp.sum(jnp.where(hit2, prefix, 0.0), axis=-1, keepdims=True)
    cnt = cnt_ref[...] + jnp.sum(hits, axis=0, keepdims=True)
    cnt_ref[...] = cnt
    cnt_out_ref[...] = cnt

    route_ref[...] = jnp.where(lane == 0, e1, jnp.where(lane == 1, e2,
                               jnp.where(lane == 2, w1, jnp.where(lane == 3, w2,
                               jnp.where(lane == 4, rank1, jnp.where(lane == 5, rank2, 0.0))))))


def _merge(att, rec, cv, gl, b_gate, x2d, wa, wh, wc, wo, ffn_g, w_route, b_route, layer):
    T, D = x2d.shape
    tm = MERGE_TILE
    row = lambda i: (i, 0)
    const2 = lambda i: (0, 0)
    wsel = lambda i: (layer, 0, 0)
    W = att.shape[1]
    idx = jnp.arange(tm, dtype=jnp.int32)
    before = (idx[None, :] < idx[:, None]).astype(BF16)
    return pl.pallas_call(
        functools.partial(_merge_body, n_groups=N_GROUPS, per_group=EXPERTS_PER_GROUP),
        grid=(T // tm,),
        in_specs=[pl.BlockSpec((tm, W), row), pl.BlockSpec((tm, W), row), pl.BlockSpec((tm, W), row),
                  pl.BlockSpec((tm, 3 * D), row), pl.BlockSpec((1, 3 * D), const2),
                  pl.BlockSpec((tm, D), row),
                  pl.BlockSpec((None, W, D), wsel), pl.BlockSpec((None, W, D), wsel),
                  pl.BlockSpec((None, W, D), wsel), pl.BlockSpec((None, D, D), wsel),
                  pl.BlockSpec((1, D), const2), pl.BlockSpec((D, LANES), const2),
                  pl.BlockSpec((1, LANES), const2), pl.BlockSpec((tm, tm), const2)],
        out_specs=[pl.BlockSpec((tm, D), row), pl.BlockSpec((tm, LANES), row),
                   pl.BlockSpec((1, LANES), const2)],
        out_shape=[jax.ShapeDtypeStruct((T, D), F32), jax.ShapeDtypeStruct((T, LANES), F32),
                   jax.ShapeDtypeStruct((1, LANES), F32)],
        scratch_shapes=[pltpu.VMEM((tm, D), BF16), pltpu.VMEM((1, LANES), F32)],
        compiler_params=_params(1, VMEM_LIMIT),
        name="merge_route",
    )(att, rec, cv, gl, b_gate.astype(F32).reshape(1, 3 * D), x2d, wa, wh, wc, wo,
      ffn_g.astype(F32).reshape(1, D), w_route, b_route, before)


def _bulk_wait(view, sem):
    pltpu.make_async_copy(view, view, sem).wait()


def _dispatch_body(ps_ref, pn_ref, nu_ref, pos_ref, x_ref, fg_ref, xs_hbm, hbuf, zbuf, sem, zsem, *,
                   tile, n_tiles, n_experts, n_sorted_tiles):
    i = pl.program_id(0)
    slot = i % 2
    ztile = zbuf.shape[0]

    def pad_copy(e, r):
        return pltpu.make_async_copy(zbuf.at[pl.ds(0, 1)], xs_hbm.at[pl.ds(ps_ref[e] + r, 1)], zsem)

    def tail_copy(t):
        return pltpu.make_async_copy(zbuf, xs_hbm.at[pl.ds(t * ztile, ztile)], zsem)

    def wait_slot(s):
        _bulk_wait(hbuf.at[s], sem.at[s])
        _bulk_wait(hbuf.at[s], sem.at[s])

    @pl.when(i == 0)
    def _():
        zbuf[...] = jnp.zeros(zbuf.shape, F32)
        for e in range(n_experts):
            def start(r, carry, e=e):
                pad_copy(e, r).start()
                return carry
            lax.fori_loop(0, pn_ref[e], start, 0)

        def start_tail(t, carry):
            tail_copy(t).start()
            return carry
        lax.fori_loop(nu_ref[0], n_sorted_tiles, start_tail, 0)

    def scatter_tile(s):
        @pl.when(i >= 2)
        def _():
            wait_slot(s)

        x = x_ref[...]
        h = x * lax.rsqrt(jnp.mean(x * x, axis=-1, keepdims=True) + RMS_EPS) * fg_ref[...]
        hbuf[s] = h.reshape(h.shape[0], 1, h.shape[1])

        def issue(r, carry):
            src = hbuf.at[s, pl.ds(r, 1)]
            pltpu.make_async_copy(src, xs_hbm.at[pl.ds(pos_ref[0, 0, r], 1)], sem.at[s]).start()
            pltpu.make_async_copy(src, xs_hbm.at[pl.ds(pos_ref[0, 0, tile + r], 1)], sem.at[s]).start()
            return carry

        lax.fori_loop(0, tile, issue, 0, unroll=DMA_UNROLL)

    for s in range(2):
        pl.when(slot == s)(functools.partial(scatter_tile, s))

    @pl.when(i == n_tiles - 1)
    def _():
        wait_slot((n_tiles - 1) % 2)
        if n_tiles >= 2:
            wait_slot(n_tiles % 2)
        for e in range(n_experts):
            def done(r, carry, e=e):
                pad_copy(e, r).wait()
                return carry
            lax.fori_loop(0, pn_ref[e], done, 0)

        def done_tail(t, carry):
            tail_copy(t).wait()
            return carry
        lax.fori_loop(nu_ref[0], n_sorted_tiles, done_tail, 0)


def _dispatch(x2d, ffn_g, pos_tiles, pad_start, pad_len, n_used, cap):
    T, D = x2d.shape
    tile = TOKEN_TILE
    nt = T // tile
    n_experts = pad_start.shape[0]
    grid_spec = pltpu.PrefetchScalarGridSpec(
        num_scalar_prefetch=3,
        grid=(nt,),
        in_specs=[pl.BlockSpec((1, 1, 2 * tile), lambda i, ps, pn, nu: (i, 0, 0), memory_space=pltpu.SMEM),
                  pl.BlockSpec((tile, D), lambda i, ps, pn, nu: (i, 0)),
                  pl.BlockSpec((1, D), lambda i, ps, pn, nu: (0, 0))],
        out_specs=pl.BlockSpec(memory_space=pl.ANY),
        scratch_shapes=[pltpu.VMEM((2, tile, 1, D), F32), pltpu.VMEM((EXPERT_TILE, 1, D), F32),
                        pltpu.SemaphoreType.DMA((2,)), pltpu.SemaphoreType.DMA(())],
    )
    return pl.pallas_call(
        functools.partial(_dispatch_body, tile=tile, n_tiles=nt, n_experts=n_experts,
                          n_sorted_tiles=cap // EXPERT_TILE),
        grid_spec=grid_spec,
        out_shape=jax.ShapeDtypeStruct((cap, 1, D), F32),
        compiler_params=_params(1, VMEM_LIMIT),
        name="moe_dispatch",
    )(pad_start, pad_len, n_used, pos_tiles, x2d, ffn_g.astype(F32).reshape(1, D))


def _expert_body(te_ref, nu_ref, x_ref, win_ref, wout_ref, y_ref, win_bf, wout_bf, x2_ref):
    i = pl.program_id(0)
    ff = wout_ref.shape[0]
    used = i < nu_ref[0]
    new_expert = jnp.logical_or(i == 0, te_ref[i] != te_ref[jnp.maximum(i - 1, 0)])

    @pl.when(jnp.logical_and(used, new_expert))
    def _():
        win_bf[...] = win_ref[...].astype(BF16)
        wout_bf[...] = wout_ref[...].astype(BF16)

    @pl.when(used)
    def _():
        x2_ref[...] = x_ref[...].reshape(x2_ref.shape)
        gu = _dot(x2_ref[...].astype(BF16), win_bf[...])
        act = (_silu(gu[:, 0:ff]) * gu[:, ff:2 * ff]).astype(BF16)
        y = _dot(act, wout_bf[...])
        y_ref[...] = y.reshape(y_ref.shape)

    @pl.when(jnp.logical_not(used))
    def _():
        y_ref[...] = jnp.zeros(y_ref.shape, F32)


def _experts(x_sorted, w_exp_in, w_exp_out, tile_expert, n_used, layer):
    cap, _, D = x_sorted.shape
    tile = EXPERT_TILE
    nt = cap // tile
    ff = w_exp_out.shape[2]
    rows = lambda i, te, nu: (i, 0, 0)
    grid_spec = pltpu.PrefetchScalarGridSpec(
        num_scalar_prefetch=2,
        grid=(nt,),
        in_specs=[pl.BlockSpec((tile, 1, D), rows),
                  pl.BlockSpec((None, None, D, 2 * ff), lambda i, te, nu: (layer, te[i], 0, 0)),
                  pl.BlockSpec((None, None, ff, D), lambda i, te, nu: (layer, te[i], 0, 0))],
        out_specs=pl.BlockSpec((tile, 1, D), rows),
        scratch_shapes=[pltpu.VMEM((D, 2 * ff), BF16), pltpu.VMEM((ff, D), BF16),
                        pltpu.VMEM((tile, D), F32)],
    )
    return pl.pallas_call(
        _expert_body,
        grid_spec=grid_spec,
        out_shape=jax.ShapeDtypeStruct((cap, 1, D), F32),
        compiler_params=_params(1, VMEM_LIMIT),
        name="expert_ffn",
    )(tile_expert, n_used, x_sorted, w_exp_in, w_exp_out)


def _combine_body(pos0_ref, posn_ref, y_hbm, x_ref, route_ref, o_ref, ybuf, y2_ref, sem, *, tile, n_tiles):
    i = pl.program_id(0)
    slot = i % 2

    def gather(pos_ref, s):
        def issue(r, carry):
            pltpu.make_async_copy(y_hbm.at[pl.ds(pos_ref[0, 0, r], 1)],
                                  ybuf.at[s, pl.ds(r, 1)], sem.at[s]).start()
            return carry
        lax.fori_loop(0, 2 * tile, issue, 0, unroll=DMA_UNROLL)

    @pl.when(i == 0)
    def _():
        gather(pos0_ref, 0)

    def combine_tile(s):
        @pl.when(i + 1 < n_tiles)
        def _():
            gather(posn_ref, 1 - s)

        _bulk_wait(ybuf.at[s], sem.at[s])
        y2_ref[...] = ybuf[s].reshape(y2_ref.shape)
        route = route_ref[...]
        o_ref[...] = (x_ref[...] + route[:, 2:3] * y2_ref[0:tile, :]
                      + route[:, 3:4] * y2_ref[tile:2 * tile, :])

    for s in range(2):
        pl.when(slot == s)(functools.partial(combine_tile, s))


def _combine(x2d, route, y_sorted, pos_tiles):
    T, D = x2d.shape
    tile = TOKEN_TILE
    nt = T // tile
    return pl.pallas_call(
        functools.partial(_combine_body, tile=tile, n_tiles=nt),
        grid=(nt,),
        in_specs=[pl.BlockSpec((1, 1, 2 * tile), lambda i: (0, 0, 0), memory_space=pltpu.SMEM),
                  pl.BlockSpec((1, 1, 2 * tile), lambda i: (jnp.minimum(i + 1, nt - 1), 0, 0),
                               memory_space=pltpu.SMEM),
                  pl.BlockSpec(memory_space=pl.ANY),
                  pl.BlockSpec((tile, D), lambda i: (i, 0)),
                  pl.BlockSpec((tile, LANES), lambda i: (i, 0))],
        out_specs=pl.BlockSpec((tile, D), lambda i: (i, 0)),
        out_shape=jax.ShapeDtypeStruct((T, D), F32),
        scratch_shapes=[pltpu.VMEM((2, 2 * tile, 1, D), F32), pltpu.VMEM((2 * tile, D), F32),
                        pltpu.SemaphoreType.DMA((2,))],
        compiler_params=_params(1, VMEM_LIMIT),
        name="moe_combine",
    )(pos_tiles, pos_tiles, y_sorted, x2d, route)


def _positions_body(route_ref, offs_ref, pos_ref, *, tile):
    r = route_ref[...]
    lane = lax.broadcasted_iota(jnp.int32, r.shape, 1)
    lane_f = lane.astype(F32)
    offs = offs_ref[...]
    pos1 = jnp.sum(jnp.where(lane_f == r[:, 0:1], offs, 0.0), axis=-1, keepdims=True) + r[:, 4:5]
    pos2 = jnp.sum(jnp.where(lane_f == r[:, 1:2], offs, 0.0), axis=-1, keepdims=True) + r[:, 5:6]
    both = jnp.where(lane == 0, pos1, jnp.where(lane == 1, pos2, 0.0))
    for k in range(r.shape[0] // tile):
        t = both[k * tile:(k + 1) * tile, :].T
        pos_ref[k, :, 0:tile] = t[0:1, :].astype(jnp.int32)
        pos_ref[k, :, tile:2 * tile] = t[1:2, :].astype(jnp.int32)


def _positions(route, offs_row):
    T = route.shape[0]
    tile = TOKEN_TILE
    step = 4 * tile
    return pl.pallas_call(
        functools.partial(_positions_body, tile=tile),
        grid=(T // step,),
        in_specs=[pl.BlockSpec((step, LANES), lambda i: (i, 0)),
                  pl.BlockSpec((1, LANES), lambda i: (0, 0))],
        out_specs=pl.BlockSpec((step // tile, 1, 2 * tile), lambda i: (i, 0, 0)),
        out_shape=jax.ShapeDtypeStruct((T // tile, 1, 2 * tile), jnp.int32),
        compiler_params=_params(1, VMEM_LIMIT),
        name="moe_positions",
    )(route, offs_row)


def _dispatch_plan(route, counts, n_experts, tile, cap):
    cnt = counts[0, :n_experts].astype(jnp.int32)
    padded = ((cnt + tile - 1) // tile) * tile
    ends = jnp.cumsum(padded)
    offs = ends - padded
    nt = cap // tile
    n_used = (ends[-1] // tile).astype(jnp.int32)
    first_row = jnp.minimum(jnp.arange(nt, dtype=jnp.int32), n_used - 1) * tile
    te = jnp.sum((ends[None, :] <= first_row[:, None]).astype(jnp.int32), axis=1)
    te = jnp.minimum(te, n_experts - 1).astype(jnp.int32)
    offs_row = jnp.pad(offs.astype(F32), (0, LANES - n_experts)).reshape(1, LANES)
    pos_tiles = _positions(route, offs_row)
    return te, n_used.reshape(1), pos_tiles, (offs + cnt).astype(jnp.int32), (padded - cnt).astype(jnp.int32)


def kernel(x, mix_norm_g, w_in, b_gate, q_norm_g, k_norm_g, lb_logits, hgrn_norm_g, conv_w, conv_b,
           conv_ln_g, conv_ln_b, w_att_o, w_hgrn_o, w_conv_o, w_out, ffn_norm_g, w_coarse, b_coarse,
           w_fine, b_fine, w_exp_in, w_exp_out):
    B, S, D = x.shape
    L = w_in.shape[0]
    T = B * S
    n_experts = w_exp_in.shape[1]
    cap = 2 * T + n_experts * EXPERT_TILE

    w_in_bf = w_in.astype(BF16)
    wa_bf, wh_bf, wc_bf, wo_bf = (w.astype(BF16) for w in (w_att_o, w_hgrn_o, w_conv_o, w_out))
    pad = LANES - N_GROUPS - n_experts
    w_route = jnp.concatenate([w_coarse, w_fine, jnp.zeros((L, D, pad), F32)], axis=-1).astype(F32)
    b_route = jnp.concatenate([b_coarse, b_fine, jnp.zeros((L, pad), F32)], axis=-1).astype(F32)

    x2d = x.reshape(T, D)
    for l in range(L):
        qkv, hf, hqig, cab, gl = _inproj(x2d, mix_norm_g[l], w_in_bf, l)
        att = _attention(qkv, q_norm_g[l], k_norm_g[l], B, S)
        rec = _hgrn(hqig, hf, lb_logits, hgrn_norm_g[l], l, B, S)
        cv = _conv(cab, conv_w[l], conv_b[l], conv_ln_g[l], conv_ln_b[l], B, S)
        x_mid, route, counts = _merge(att, rec, cv, gl, b_gate[l], x2d, wa_bf, wh_bf, wc_bf, wo_bf,
                                      ffn_norm_g[l], w_route[l], b_route[l].reshape(1, LANES), l)
        te, n_used, pos_tiles, pad_start, pad_len = _dispatch_plan(route, counts, n_experts, EXPERT_TILE, cap)
        x_sorted = _dispatch(x_mid, ffn_norm_g[l], pos_tiles, pad_start, pad_len, n_used, cap)
        y_sorted = _experts(x_sorted, w_exp_in, w_exp_out, te, n_used, l)
        x2d = _combine(x_mid, route, y_sorted, pos_tiles)
    return x2d.reshape(B, S, D)
```

```python
import functools

import jax
import jax.numpy as jnp
from jax import lax
from jax.experimental import pallas as pl
from jax.experimental.pallas import tpu as pltpu

F32 = jnp.float32
BF16 = jnp.bfloat16

LANES = 128
SUBLANES = 8
RMS_EPS = 1e-6
LN_EPS = 1e-5
NEG_INF = -1e30
LOG2_E = 1.4426950408889634

ATT_HEADS = 8
ATT_HEAD_DIM = 64
MOBA_BLOCK = 256
MOBA_TOPK = 3
HGRN_HEADS = 4
HGRN_DIM = 128
HGRN_ROWS = 256
CONV_WIDTH = 31
CONV_ROWS = 256
CONV_HALO = 32
N_GROUPS = 4
EXPERTS_PER_GROUP = 8
EXPERT_TILE = 256
TOKEN_TILE = 256
MERGE_TILE = 512
DMA_UNROLL = 8
VMEM_LIMIT = 56 * 1024 * 1024


def _params(n_axes, vmem=None):
    return pltpu.CompilerParams(dimension_semantics=("arbitrary",) * n_axes,
                                vmem_limit_bytes=vmem)


def _dot(a, b):
    return jnp.dot(a, b, preferred_element_type=F32)


def _dot_nt(a, b):
    return lax.dot_general(a, b, (((1,), (1,)), ((), ())), preferred_element_type=F32)


def _dot_tn(a, b):
    return lax.dot_general(a, b, (((0,), (0,)), ((), ())), preferred_element_type=F32)


def _split3(x):
    x1 = x.astype(BF16)
    r1 = x - x1.astype(F32)
    x2 = r1.astype(BF16)
    x3 = (r1 - x2.astype(F32)).astype(BF16)
    return x1, x2, x3


def _sigmoid(x):
    return 1.0 / (1.0 + jnp.exp(-x))


def _silu(x):
    return x * _sigmoid(x)


def _inproj_body(x_ref, g_ref, w_ref, qkv_ref, hf_ref, hqig_ref, cv_ref, gl_ref, *, segs, chunk):
    x = x_ref[...]
    ms = jnp.mean(x * x, axis=-1, keepdims=True)
    h = (x * lax.rsqrt(ms + RMS_EPS) * g_ref[...]).astype(BF16)
    refs = (qkv_ref, hf_ref, hqig_ref, cv_ref, gl_ref)
    for ridx, dst0, src0, width in segs:
        ref = refs[ridx]
        for c in range(0, width, chunk):
            y = _dot(h, w_ref[:, src0 + c:src0 + c + chunk])
            ref[:, dst0 + c:dst0 + c + chunk] = y.astype(ref.dtype)


def _inproj(x2d, norm_g, w_in_bf, layer):
    T, D = x2d.shape
    A = ATT_HEADS * ATT_HEAD_DIM
    K = HGRN_HEADS * HGRN_DIM
    C = A
    segs = ((0, 0, 0, 3 * A),
            (2, 0, 3 * A, K),
            (1, 0, 3 * A + K, K),
            (2, K, 3 * A + 2 * K, K),
            (2, 2 * K, 3 * A + 3 * K, K),
            (3, 0, 3 * A + 4 * K, 2 * C),
            (4, 0, 3 * A + 4 * K + 2 * C, 3 * D))
    ncols = w_in_bf.shape[-1]
    tm = TOKEN_TILE
    row = lambda i: (i, 0)
    return pl.pallas_call(
        functools.partial(_inproj_body, segs=segs, chunk=512),
        grid=(T // tm,),
        in_specs=[pl.BlockSpec((tm, D), row),
                  pl.BlockSpec((1, D), lambda i: (0, 0)),
                  pl.BlockSpec((None, D, ncols), lambda i: (layer, 0, 0))],
        out_specs=[pl.BlockSpec((tm, 3 * A), row), pl.BlockSpec((tm, K), row),
                   pl.BlockSpec((tm, 3 * K), row), pl.BlockSpec((tm, 2 * C), row),
                   pl.BlockSpec((tm, 3 * D), row)],
        out_shape=[jax.ShapeDtypeStruct((T, 3 * A), BF16), jax.ShapeDtypeStruct((T, K), F32),
                   jax.ShapeDtypeStruct((T, 3 * K), BF16), jax.ShapeDtypeStruct((T, 2 * C), BF16),
                   jax.ShapeDtypeStruct((T, 3 * D), BF16)],
        compiler_params=_params(1, VMEM_LIMIT),
        name="inproj",
    )(x2d, norm_g.reshape(1, D), w_in_bf)


def _attn_body(q_ref, k_ref, v_ref, gq_ref, gk_ref, o_ref, kb_ref, vt_ref, s_ref, *, seq, blk, topk):
    nb = seq // blk
    dh = ATT_HEAD_DIM
    dh_sh = dh.bit_length() - 1
    scale = dh ** -0.5 * LOG2_E
    lane = lax.broadcasted_iota(jnp.int32, (1, LANES), 1)
    hr = lax.broadcasted_iota(jnp.int32, (LANES, LANES), 0) >> dh_sh
    hc = lax.broadcasted_iota(jnp.int32, (LANES, LANES), 1) >> dh_sh
    same_head = jnp.where(hr == hc, 1.0, 0.0).astype(BF16)

    def head_norm(x, g):
        x2 = x * x
        hi = x2.astype(BF16)
        lo = (x2 - hi.astype(F32)).astype(BF16)
        ssq = _dot(hi, same_head) + _dot(lo, same_head)
        return x * lax.rsqrt(ssq * (1.0 / dh) + RMS_EPS) * g

    kmeans = []
    for j in range(nb):
        kn = head_norm(k_ref[j * blk:(j + 1) * blk, :].astype(F32), gk_ref[...])
        kb_ref[j * blk:(j + 1) * blk, :] = kn.astype(BF16)
        kmeans.append(jnp.mean(kn, axis=0, keepdims=True))
        vt_ref[:, j * blk:(j + 1) * blk] = v_ref[j * blk:(j + 1) * blk, :].astype(F32).T.astype(BF16)
    kmean = jnp.concatenate(kmeans, axis=0)
    km_hi = kmean.astype(BF16)
    km_lo = (kmean - km_hi.astype(F32)).astype(BF16)
    first_head = lane < dh
    first_head_t = lax.broadcasted_iota(jnp.int32, (LANES, 1), 0) < dh
    blk_n = lax.broadcasted_iota(jnp.int32, (nb, 1), 0)

    key_l = lax.broadcasted_iota(jnp.int32, (blk, 2 * blk), 0)
    qry_l = lax.broadcasted_iota(jnp.int32, (blk, 2 * blk), 1) & (blk - 1)
    causal = key_l <= qry_l

    for i in range(nb):
        rows = head_norm(q_ref[i * blk:(i + 1) * blk, :].astype(F32), gq_ref[...])
        qi = jnp.concatenate([jnp.where(first_head, rows, 0.0), jnp.where(first_head, 0.0, rows)], axis=0)
        qs = (qi * scale).astype(BF16)

        sel = None
        if i > topk:
            q_hi = qi.astype(BF16)
            q_lo = (qi - q_hi.astype(F32)).astype(BF16)
            gate = _dot_nt(km_hi, q_hi) + _dot_nt(km_lo, q_hi) + _dot_nt(km_hi, q_lo)
            valid = blk_n < i
            gm = jnp.where(valid, gate, NEG_INF)
            rank = jnp.zeros(gm.shape, F32)
            for r in range(1, nb):
                gr = pltpu.roll(gm, r, 0)
                ahead = (gr > gm) | ((gr == gm) & (blk_n >= r))
                rank = rank + jnp.where(ahead, 1.0, 0.0)
            sel = jnp.where(valid & (rank < topk), 1.0, 0.0)

        m = None
        for j in range(i + 1):
            s = _dot_nt(kb_ref[j * blk:(j + 1) * blk, :], qs)
            if j == i:
                s = jnp.where(causal, s, NEG_INF)
            elif sel is not None:
                s = jnp.where(sel[j:j + 1, :] > 0.5, s, NEG_INF)
            s_ref[j * blk:(j + 1) * blk, :] = s
            mj = jnp.max(s, axis=0, keepdims=True)
            m = mj if m is None else jnp.maximum(m, mj)

        l = None
        acc = None
        for j in range(i + 1):
            p = jnp.exp2(s_ref[j * blk:(j + 1) * blk, :] - m)
            lj = jnp.sum(p, axis=0, keepdims=True)
            aj = _dot(vt_ref[:, j * blk:(j + 1) * blk], p.astype(BF16))
            l = lj if l is None else l + lj
            acc = aj if acc is None else acc + aj

        ot = acc / l
        merged = jnp.where(first_head_t, ot[:, 0:blk], ot[:, blk:2 * blk])
        o_ref[i * blk:(i + 1) * blk, :] = merged.T.astype(o_ref.dtype)


def _attention(qkv, q_norm_g, k_norm_g, batch, seq):
    T = qkv.shape[0]
    A = ATT_HEADS * ATT_HEAD_DIM
    npair = A // LANES
    reps = LANES // ATT_HEAD_DIM
    gq = jnp.tile(q_norm_g.astype(F32), reps).reshape(1, LANES)
    gk = jnp.tile(k_norm_g.astype(F32), reps).reshape(1, LANES)
    blk_spec = lambda off: pl.BlockSpec((seq, LANES), lambda b, p: (b, off + p))
    vec = pl.BlockSpec((1, LANES), lambda b, p: (0, 0))
    return pl.pallas_call(
        functools.partial(_attn_body, seq=seq, blk=MOBA_BLOCK, topk=MOBA_TOPK),
        grid=(batch, npair),
        in_specs=[blk_spec(0), blk_spec(npair), blk_spec(2 * npair), vec, vec],
        out_specs=pl.BlockSpec((seq, LANES), lambda b, p: (b, p)),
        out_shape=jax.ShapeDtypeStruct((T, A), BF16),
        scratch_shapes=[pltpu.VMEM((seq, LANES), BF16), pltpu.VMEM((LANES, seq), BF16),
                        pltpu.VMEM((seq, 2 * MOBA_BLOCK), F32)],
        compiler_params=_params(2, VMEM_LIMIT),
        name="moba_attn",
    )(qkv, qkv, qkv, gq, gk)


def _hgrn_head(q_ref, f_ref, i_ref, g_ref, lbl_ref, ng_ref, tri_ref, lvl_ref, o_ref, st_ref, *,
               rows, layer, head):
    R = rows
    cols = slice(head * HGRN_DIM, (head + 1) * HGRN_DIM)

    lbl = lbl_ref[:, cols]
    e = jnp.exp(lbl - jnp.max(lbl, axis=0, keepdims=True))
    p = e / jnp.sum(e, axis=0, keepdims=True)
    lb = jnp.maximum(jnp.sum(p[0:layer + 1], axis=0, keepdims=True) - p[0:1], 0.0)

    sig = _sigmoid(f_ref[:, cols])
    g = jnp.log(lb + (1.0 - lb) * sig)
    kin = (1.0 - lb) * (1.0 - sig)
    qa = _silu(q_ref[:, cols].astype(F32))
    vb = i_ref[:, cols]

    t_col = lax.broadcasted_iota(jnp.int32, (R, 1), 0)

    tri = tri_ref[...]
    g1, g2, g3 = _split3(g)
    b = (_dot(tri, g1) + _dot(tri, g2) + _dot(tri, g3)) * LOG2_E

    half = R // 2
    scores = [None, None]
    cross = None
    m = half
    while m >= 1:
        w = 2 * m
        if w >= SUBLANES:
            b3 = b.reshape(R // w, w, LANES)
            bref = jnp.broadcast_to(b3[:, m - 1:m, :], b3.shape).reshape(R, LANES)
        else:
            tl = t_col & (w - 1)
            bref = None
            for resid in range(w):
                shift = resid - (m - 1)
                cand = b if shift == 0 else pltpu.roll(b, shift % R, 0)
                bref = cand if bref is None else jnp.where(tl == resid, cand, bref)
        e = jnp.exp2(-jnp.abs(b - bref))
        qt = (qa * e).astype(BF16)
        kt = (kin * e).astype(BF16)
        if w == R:
            cross = _dot_nt(qt[half:R], kt[0:half])
        else:
            keep = lvl_ref[...] == (m.bit_length() - 1)
            for hh in range(2):
                part = _dot_nt(qt[hh * half:(hh + 1) * half], kt[hh * half:(hh + 1) * half])
                scores[hh] = jnp.where(keep, part, 0.0 if scores[hh] is None else scores[hh])
        m //= 2

    st = st_ref[head]
    o = _dot_nt((qa * jnp.exp2(b)).astype(BF16), st.astype(BF16))
    o = o + jnp.sum(qa * kin, axis=-1, keepdims=True) * vb.astype(F32)
    o_lo = o[0:half] + _dot(scores[0].astype(BF16), vb[0:half])
    o_hi = o[half:R] + _dot(scores[1].astype(BF16), vb[half:R]) + _dot(cross.astype(BF16), vb[0:half])
    o = jnp.concatenate([o_lo, o_hi], axis=0)

    b_end = b[R - 1:R, :]
    k_end = (kin * jnp.exp2(b_end - b)).astype(BF16)
    st_ref[head] = st * jnp.exp2(b_end) + _dot_tn(vb, k_end)

    on = o * lax.rsqrt(jnp.mean(o * o, axis=-1, keepdims=True) + RMS_EPS) * ng_ref[...]
    o_ref[:, cols] = (on * _silu(g_ref[:, cols].astype(F32))).astype(o_ref.dtype)


def _hgrn_body(q_ref, f_ref, i_ref, g_ref, lbl_ref, ng_ref, tri_ref, lvl_ref, o_ref, st_ref, *, rows, layer):
    @pl.when(pl.program_id(1) == 0)
    def _():
        st_ref[...] = jnp.zeros(st_ref.shape, F32)

    for head in range(st_ref.shape[0]):
        _hgrn_head(q_ref, f_ref, i_ref, g_ref, lbl_ref, ng_ref, tri_ref, lvl_ref, o_ref, st_ref,
                   rows=rows, layer=layer, head=head)


def _level_table(n):
    t = jnp.arange(n, dtype=jnp.int32)[:, None]
    s = jnp.arange(n, dtype=jnp.int32)[None, :]
    x = t ^ s
    hb = jnp.zeros((n, n), jnp.int32)
    for k in range(1, n.bit_length()):
        hb = hb + (x >= (1 << k)).astype(jnp.int32)
    return jnp.where(t > s, hb, -1)


def _hgrn(hqig, hf, lb_logits, norm_g, layer, batch, seq):
    T = hf.shape[0]
    H = HGRN_HEADS
    R = HGRN_ROWS
    W = H * HGRN_DIM
    nc = seq // R
    L = lb_logits.shape[0]
    blk = lambda off: pl.BlockSpec((R, W), lambda b, c: (b * nc + c, off))
    const = lambda b, c: (0, 0)
    idx = jnp.arange(R, dtype=jnp.int32)
    tri = (idx[None, :] <= idx[:, None]).astype(BF16)
    return pl.pallas_call(
        functools.partial(_hgrn_body, rows=R, layer=layer),
        grid=(batch, nc),
        in_specs=[blk(0), blk(0), blk(1), blk(2),
                  pl.BlockSpec((L, W), const), pl.BlockSpec((1, LANES), const),
                  pl.BlockSpec((R, R), const), pl.BlockSpec((R // 2, R // 2), const)],
        out_specs=blk(0),
        out_shape=jax.ShapeDtypeStruct((T, W), BF16),
        scratch_shapes=[pltpu.VMEM((H, HGRN_DIM, HGRN_DIM), F32)],
        compiler_params=_params(2, VMEM_LIMIT),
        name="hgrn2",
    )(hqig, hf, hqig, hqig, lb_logits.astype(F32), norm_g.astype(F32).reshape(1, LANES), tri,
      _level_table(R // 2))


def _conv_body(ab_ref, w_ref, bdw_ref, lg_ref, lb_ref, o_ref, ubuf, cbuf, ush, *, rows, width, chans):
    R, W, C, H = rows, width, chans, CONV_HALO
    c = pl.program_id(1)

    @pl.when(c == 0)
    def _():
        ubuf[0:H, :] = jnp.zeros((H, C), F32)

    a = ab_ref[:, 0:C].astype(F32)
    gate = ab_ref[:, C:2 * C].astype(F32)
    ubuf[H:H + R, :] = a * _sigmoid(gate)

    n_sh = ush.shape[1]
    for s in range(1, SUBLANES):
        ush[s - 1] = ubuf[s:s + n_sh, :]

    rsub = 64
    for lb in range(0, C, LANES):
        for rb in range(0, R, rsub):
            acc = jnp.zeros((rsub, LANES), F32)
            for j in range(W):
                start = H - (W - 1) + j + rb
                s = start % SUBLANES
                a0 = start - s
                if s == 0:
                    u = ubuf[a0:a0 + rsub, lb:lb + LANES]
                else:
                    u = ush[s - 1, a0:a0 + rsub, lb:lb + LANES]
                acc = acc + w_ref[j:j + 1, lb:lb + LANES] * u
            cbuf[rb:rb + rsub, lb:lb + LANES] = acc

    uf = cbuf[...] + bdw_ref[...]
    mu = jnp.mean(uf, axis=-1, keepdims=True)
    d = uf - mu
    var = jnp.mean(d * d, axis=-1, keepdims=True)
    y = d * lax.rsqrt(var + LN_EPS) * lg_ref[...] + lb_ref[...]
    o_ref[...] = _silu(y).astype(o_ref.dtype)
    ubuf[0:H, :] = ubuf[R:R + H, :]


def _conv(cab, conv_w, conv_b, ln_g, ln_b, batch, seq):
    T = cab.shape[0]
    C = cab.shape[1] // 2
    W = conv_w.shape[0]
    R = CONV_ROWS
    nc = seq // R
    vec = pl.BlockSpec((1, C), lambda b, c: (0, 0))
    return pl.pallas_call(
        functools.partial(_conv_body, rows=R, width=W, chans=C),
        grid=(batch, nc),
        in_specs=[pl.BlockSpec((R, 2 * C), lambda b, c: (b * nc + c, 0)),
                  pl.BlockSpec((W, C), lambda b, c: (0, 0)), vec, vec, vec],
        out_specs=pl.BlockSpec((R, C), lambda b, c: (b * nc + c, 0)),
        out_shape=jax.ShapeDtypeStruct((T, C), BF16),
        scratch_shapes=[pltpu.VMEM((R + CONV_HALO, C), F32), pltpu.VMEM((R, C), F32),
                        pltpu.VMEM((SUBLANES - 1, R + CONV_HALO - SUBLANES, C), F32)],
        compiler_params=_params(2, VMEM_LIMIT),
        name="conformer_conv",
    )(cab, conv_w.astype(F32), conv_b.astype(F32).reshape(1, C), ln_g.astype(F32).reshape(1, C),
      ln_b.astype(F32).reshape(1, C))


def _merge_body(att_ref, rec_ref, cv_ref, gl_ref, bg_ref, x_ref, wa_ref, wh_ref, wc_ref, wo_ref,
                fg_ref, wr_ref, br_ref, before_ref, xo_ref, route_ref, cnt_out_ref, mg_ref, cnt_ref, *,
                n_groups, per_group):
    D = x_ref.shape[1]
    att, rec, cv = att_ref[...], rec_ref[...], cv_ref[...]
    cw = 256
    for c in range(0, D, cw):
        def gate(k):
            return _sigmoid(gl_ref[:, k * D + c:k * D + c + cw].astype(F32) + bg_ref[:, k * D + c:k * D + c + cw])
        merged = (gate(0) * _dot(att, wa_ref[:, c:c + cw]) + gate(1) * _dot(rec, wh_ref[:, c:c + cw])
                  + gate(2) * _dot(cv, wc_ref[:, c:c + cw]))
        mg_ref[:, c:c + cw] = merged.astype(BF16)
    xn = x_ref[...] + _dot(mg_ref[...], wo_ref[...])
    xo_ref[...] = xn

    h = xn * lax.rsqrt(jnp.mean(xn * xn, axis=-1, keepdims=True) + RMS_EPS) * fg_ref[...]
    h_hi = h.astype(BF16)
    h_lo = (h - h_hi.astype(F32)).astype(BF16)
    wr = wr_ref[...]
    w_hi = wr.astype(BF16)
    w_lo = (wr - w_hi.astype(F32)).astype(BF16)
    logits = _dot(h_hi, w_hi) + _dot(h_hi, w_lo) + _dot(h_lo, w_hi) + br_ref[...]

    G, E = n_groups, per_group
    lane = lax.broadcasted_iota(jnp.int32, logits.shape, 1)
    big = jnp.int32(LANES)
    is_c = lane < G
    cl = jnp.where(is_c, logits, NEG_INF)
    cm = jnp.max(cl, axis=-1, keepdims=True)
    grp = jnp.min(jnp.where(cl == cm, lane, big), axis=-1, keepdims=True)
    se = jnp.sum(jnp.exp(jnp.where(is_c, logits - cm, NEG_INF)), axis=-1, keepdims=True)
    p_top = 1.0 / se
    lo = G + grp * E
    fl = jnp.where((lane >= lo) & (lane < lo + E), logits, NEG_INF)
    m1 = jnp.max(fl, axis=-1, keepdims=True)
    i1 = jnp.min(jnp.where(fl == m1, lane, big), axis=-1, keepdims=True)
    fl2 = jnp.where(lane == i1, NEG_INF, fl)
    m2 = jnp.max(fl2, axis=-1, keepdims=True)
    i2 = jnp.min(jnp.where(fl2 == m2, lane, big), axis=-1, keepdims=True)
    t = jnp.exp(m2 - m1)
    w1 = p_top / (1.0 + t)
    w2 = p_top * t / (1.0 + t)
    e1 = (i1 - G).astype(F32)
    e2 = (i2 - G).astype(F32)

    @pl.when(pl.program_id(0) == 0)
    def _():
        cnt_ref[...] = jnp.zeros(cnt_ref.shape, F32)

    hit1 = lane == (i1 - G)
    hit2 = lane == (i2 - G)
    hits = jnp.where(hit1 | hit2, 1.0, 0.0)
    prefix = _dot(before_ref[...], hits.astype(BF16)) + cnt_ref[...]
    rank1 = jnp.sum(jnp.where(hit1, prefix, 0.0), axis=-1, keepdims=True)
    rank2 = jnp.sum(jnp.where(hit2, prefix, 0.0), axis=-1, keepdims=True)
    cnt = cnt_ref[...] + jnp.sum(hits, axis=0, keepdims=True)
    cnt_ref[...] = cnt
    cnt_out_ref[...] = cnt

    route_ref[...] = jnp.where(lane == 0, e1, jnp.where(lane == 1, e2,
                               jnp.where(lane == 2, w1, jnp.where(lane == 3, w2,
                               jnp.where(lane == 4, rank1, jnp.where(lane == 5, rank2, 0.0))))))


def _merge(att, rec, cv, gl, b_gate, x2d, wa, wh, wc, wo, ffn_g, w_route, b_route, layer):
    T, D = x2d.shape
    tm = MERGE_TILE
    row = lambda i: (i, 0)
    const2 = lambda i: (0, 0)
    wsel = lambda i: (layer, 0, 0)
    W = att.shape[1]
    idx = jnp.arange(tm, dtype=jnp.int32)
    before = (idx[None, :] < idx[:, None]).astype(BF16)
    return pl.pallas_call(
        functools.partial(_merge_body, n_groups=N_GROUPS, per_group=EXPERTS_PER_GROUP),
        grid=(T // tm,),
        in_specs=[pl.BlockSpec((tm, W), row), pl.BlockSpec((tm, W), row), pl.BlockSpec((tm, W), row),
                  pl.BlockSpec((tm, 3 * D), row), pl.BlockSpec((1, 3 * D), const2),
                  pl.BlockSpec((tm, D), row),
                  pl.BlockSpec((None, W, D), wsel), pl.BlockSpec((None, W, D), wsel),
                  pl.BlockSpec((None, W, D), wsel), pl.BlockSpec((None, D, D), wsel),
                  pl.BlockSpec((1, D), const2), pl.BlockSpec((D, LANES), const2),
                  pl.BlockSpec((1, LANES), const2), pl.BlockSpec((tm, tm), const2)],
        out_specs=[pl.BlockSpec((tm, D), row), pl.BlockSpec((tm, LANES), row),
                   pl.BlockSpec((1, LANES), const2)],
        out_shape=[jax.ShapeDtypeStruct((T, D), F32), jax.ShapeDtypeStruct((T, LANES), F32),
                   jax.ShapeDtypeStruct((1, LANES), F32)],
        scratch_shapes=[pltpu.VMEM((tm, D), BF16), pltpu.VMEM((1, LANES), F32)],
        compiler_params=_params(1, VMEM_LIMIT),
        name="merge_route",
    )(att, rec, cv, gl, b_gate.astype(F32).reshape(1, 3 * D), x2d, wa, wh, wc, wo,
      ffn_g.astype(F32).reshape(1, D), w_route, b_route, before)


def _bulk_wait(view, sem):
    pltpu.make_async_copy(view, view, sem).wait()


def _dispatch_body(ps_ref, pn_ref, nu_ref, pos_ref, x_ref, fg_ref, xs_hbm, hbuf, zbuf, sem, zsem, *,
                   tile, n_tiles, n_experts, n_sorted_tiles):
    i = pl.program_id(0)
    slot = i % 2
    ztile = zbuf.shape[0]

    def pad_copy(e, r):
        return pltpu.make_async_copy(zbuf.at[pl.ds(0, 1)], xs_hbm.at[pl.ds(ps_ref[e] + r, 1)], zsem)

    def tail_copy(t):
        return pltpu.make_async_copy(zbuf, xs_hbm.at[pl.ds(t * ztile, ztile)], zsem)

    def wait_slot(s):
        _bulk_wait(hbuf.at[s], sem.at[s])
        _bulk_wait(hbuf.at[s], sem.at[s])

    @pl.when(i == 0)
    def _():
        zbuf[...] = jnp.zeros(zbuf.shape, F32)
        for e in range(n_experts):
            def start(r, carry, e=e):
                pad_copy(e, r).start()
                return carry
            lax.fori_loop(0, pn_ref[e], start, 0)

        def start_tail(t, carry):
            tail_copy(t).start()
            return carry
        lax.fori_loop(nu_ref[0], n_sorted_tiles, start_tail, 0)

    def scatter_tile(s):
        @pl.when(i >= 2)
        def _():
            wait_slot(s)

        x = x_ref[...]
        h = x * lax.rsqrt(jnp.mean(x * x, axis=-1, keepdims=True) + RMS_EPS) * fg_ref[...]
        hbuf[s] = h.reshape(h.shape[0], 1, h.shape[1])

        def issue(g, carry):
            for k in range(DMA_UNROLL):
                r = g * DMA_UNROLL + k
                src = hbuf.at[s, pl.ds(r, 1)]
                pltpu.make_async_copy(src, xs_hbm.at[pl.ds(pos_ref[0, 0, r], 1)],
                                      sem.at[s]).start(priority=0)
                pltpu.make_async_copy(src, xs_hbm.at[pl.ds(pos_ref[0, 0, tile + r], 1)],
                                      sem.at[s]).start(priority=1)
            return carry

        lax.fori_loop(0, tile // DMA_UNROLL, issue, 0)

    for s in range(2):
        pl.when(slot == s)(functools.partial(scatter_tile, s))

    @pl.when(i == n_tiles - 1)
    def _():
        wait_slot((n_tiles - 1) % 2)
        if n_tiles >= 2:
            wait_slot(n_tiles % 2)
        for e in range(n_experts):
            def done(r, carry, e=e):
                pad_copy(e, r).wait()
                return carry
            lax.fori_loop(0, pn_ref[e], done, 0)

        def done_tail(t, carry):
            tail_copy(t).wait()
            return carry
        lax.fori_loop(nu_ref[0], n_sorted_tiles, done_tail, 0)


def _dispatch(x2d, ffn_g, pos_tiles, pad_start, pad_len, n_used, cap):
    T, D = x2d.shape
    tile = TOKEN_TILE
    nt = T // tile
    n_experts = pad_start.shape[0]
    grid_spec = pltpu.PrefetchScalarGridSpec(
        num_scalar_prefetch=3,
        grid=(nt,),
        in_specs=[pl.BlockSpec((1, 1, 2 * tile), lambda i, ps, pn, nu: (i, 0, 0), memory_space=pltpu.SMEM),
                  pl.BlockSpec((tile, D), lambda i, ps, pn, nu: (i, 0)),
                  pl.BlockSpec((1, D), lambda i, ps, pn, nu: (0, 0))],
        out_specs=pl.BlockSpec(memory_space=pl.ANY),
        scratch_shapes=[pltpu.VMEM((2, tile, 1, D), F32), pltpu.VMEM((EXPERT_TILE, 1, D), F32),
                        pltpu.SemaphoreType.DMA((2,)), pltpu.SemaphoreType.DMA(())],
    )
    return pl.pallas_call(
        functools.partial(_dispatch_body, tile=tile, n_tiles=nt, n_experts=n_experts,
                          n_sorted_tiles=cap // EXPERT_TILE),
        grid_spec=grid_spec,
        out_shape=jax.ShapeDtypeStruct((cap, 1, D), F32),
        compiler_params=_params(1, VMEM_LIMIT),
        name="moe_dispatch",
    )(pad_start, pad_len, n_used, pos_tiles, x2d, ffn_g.astype(F32).reshape(1, D))


def _expert_body(te_ref, nu_ref, nx_ref, sl_ref, x_ref, win_hbm, wout_hbm, y_ref, win_bf, wout_bf, x2_ref,
                 win_st, wout_st, wsem, *, layer):
    i = pl.program_id(0)
    ff = wout_bf.shape[0]
    used = i < nu_ref[0]
    new_expert = jnp.logical_or(i == 0, te_ref[i] != te_ref[jnp.maximum(i - 1, 0)])

    def weight_copies(e, slot):
        return (pltpu.make_async_copy(win_hbm.at[layer, e], win_st.at[slot], wsem.at[0, slot]),
                pltpu.make_async_copy(wout_hbm.at[layer, e], wout_st.at[slot], wsem.at[1, slot]))

    @pl.when(i == 0)
    def _():
        for c in weight_copies(te_ref[0], 0):
            c.start()

    def begin_run(s):
        for c in weight_copies(te_ref[i], s):
            c.wait()
        win_bf[...] = win_st[s].astype(BF16)
        wout_bf[...] = wout_st[s].astype(BF16)

        @pl.when(nx_ref[i] >= 0)
        def _():
            for c in weight_copies(nx_ref[i], 1 - s):
                c.start()

    for s in range(2):
        pl.when(jnp.logical_and(jnp.logical_and(used, new_expert), sl_ref[i] == s))(
            functools.partial(begin_run, s))

    @pl.when(used)
    def _():
        x2_ref[...] = x_ref[...].reshape(x2_ref.shape)
        gu = _dot(x2_ref[...].astype(BF16), win_bf[...])
        act = (_silu(gu[:, 0:ff]) * gu[:, ff:2 * ff]).astype(BF16)
        y = _dot(act, wout_bf[...])
        y_ref[...] = y.reshape(y_ref.shape)

    @pl.when(jnp.logical_not(used))
    def _():
        y_ref[...] = jnp.zeros(y_ref.shape, F32)


def _experts(x_sorted, w_exp_in, w_exp_out, tile_expert, n_used, next_expert, run_slot, layer):
    cap, _, D = x_sorted.shape
    tile = EXPERT_TILE
    nt = cap // tile
    ff = w_exp_out.shape[2]
    rows = lambda i, te, nu, nx, sl: (i, 0, 0)
    grid_spec = pltpu.PrefetchScalarGridSpec(
        num_scalar_prefetch=4,
        grid=(nt,),
        in_specs=[pl.BlockSpec((tile, 1, D), rows),
                  pl.BlockSpec(memory_space=pl.ANY), pl.BlockSpec(memory_space=pl.ANY)],
        out_specs=pl.BlockSpec((tile, 1, D), rows),
        scratch_shapes=[pltpu.VMEM((D, 2 * ff), BF16), pltpu.VMEM((ff, D), BF16),
                        pltpu.VMEM((tile, D), F32),
                        pltpu.VMEM((2, D, 2 * ff), F32), pltpu.VMEM((2, ff, D), F32),
                        pltpu.SemaphoreType.DMA((2, 2))],
    )
    return pl.pallas_call(
        functools.partial(_expert_body, layer=layer),
        grid_spec=grid_spec,
        out_shape=jax.ShapeDtypeStruct((cap, 1, D), F32),
        compiler_params=_params(1, VMEM_LIMIT),
        name="expert_ffn",
    )(tile_expert, n_used, next_expert, run_slot, x_sorted, w_exp_in, w_exp_out)


def _combine_body(pos0_ref, posn_ref, y_hbm, x_ref, route_ref, o_ref, ybuf, y2_ref, sem, *, tile, n_tiles):
    i = pl.program_id(0)
    slot = i % 2

    def gather(pos_ref, s):
        def issue(g, carry):
            for k in range(DMA_UNROLL):
                r = g * DMA_UNROLL + k
                pltpu.make_async_copy(y_hbm.at[pl.ds(pos_ref[0, 0, r], 1)],
                                      ybuf.at[s, pl.ds(r, 1)], sem.at[s]).start(priority=k % 2)
            return carry
        lax.fori_loop(0, 2 * tile // DMA_UNROLL, issue, 0)

    @pl.when(i == 0)
    def _():
        gather(pos0_ref, 0)

    def combine_tile(s):
        @pl.when(i + 1 < n_tiles)
        def _():
            gather(posn_ref, 1 - s)

        _bulk_wait(ybuf.at[s], sem.at[s])
        y2_ref[...] = ybuf[s].reshape(y2_ref.shape)
        route = route_ref[...]
        o_ref[...] = (x_ref[...] + route[:, 2:3] * y2_ref[0:tile, :]
                      + route[:, 3:4] * y2_ref[tile:2 * tile, :])

    for s in range(2):
        pl.when(slot == s)(functools.partial(combine_tile, s))


def _combine(x2d, route, y_sorted, pos_tiles):
    T, D = x2d.shape
    tile = TOKEN_TILE
    nt = T // tile
    return pl.pallas_call(
        functools.partial(_combine_body, tile=tile, n_tiles=nt),
        grid=(nt,),
        in_specs=[pl.BlockSpec((1, 1, 2 * tile), lambda i: (0, 0, 0), memory_space=pltpu.SMEM),
                  pl.BlockSpec((1, 1, 2 * tile), lambda i: (jnp.minimum(i + 1, nt - 1), 0, 0),
                               memory_space=pltpu.SMEM),
                  pl.BlockSpec(memory_space=pl.ANY),
                  pl.BlockSpec((tile, D), lambda i: (i, 0)),
                  pl.BlockSpec((tile, LANES), lambda i: (i, 0))],
        out_specs=pl.BlockSpec((tile, D), lambda i: (i, 0)),
        out_shape=jax.ShapeDtypeStruct((T, D), F32),
        scratch_shapes=[pltpu.VMEM((2, 2 * tile, 1, D), F32), pltpu.VMEM((2 * tile, D), F32),
                        pltpu.SemaphoreType.DMA((2,))],
        compiler_params=_params(1, VMEM_LIMIT),
        name="moe_combine",
    )(pos_tiles, pos_tiles, y_sorted, x2d, route)


def _positions_body(route_ref, offs_ref, pos_ref, *, tile):
    r = route_ref[...]
    lane = lax.broadcasted_iota(jnp.int32, r.shape, 1)
    lane_f = lane.astype(F32)
    offs = offs_ref[...]
    pos1 = jnp.sum(jnp.where(lane_f == r[:, 0:1], offs, 0.0), axis=-1, keepdims=True) + r[:, 4:5]
    pos2 = jnp.sum(jnp.where(lane_f == r[:, 1:2], offs, 0.0), axis=-1, keepdims=True) + r[:, 5:6]
    both = jnp.where(lane == 0, pos1, jnp.where(lane == 1, pos2, 0.0))
    for k in range(r.shape[0] // tile):
        t = both[k * tile:(k + 1) * tile, :].T
        pos_ref[k, :, 0:tile] = t[0:1, :].astype(jnp.int32)
        pos_ref[k, :, tile:2 * tile] = t[1:2, :].astype(jnp.int32)


def _positions(route, offs_row):
    T = route.shape[0]
    tile = TOKEN_TILE
    step = 4 * tile
    return pl.pallas_call(
        functools.partial(_positions_body, tile=tile),
        grid=(T // step,),
        in_specs=[pl.BlockSpec((step, LANES), lambda i: (i, 0)),
                  pl.BlockSpec((1, LANES), lambda i: (0, 0))],
        out_specs=pl.BlockSpec((step // tile, 1, 2 * tile), lambda i: (i, 0, 0)),
        out_shape=jax.ShapeDtypeStruct((T // tile, 1, 2 * tile), jnp.int32),
        compiler_params=_params(1, VMEM_LIMIT),
        name="moe_positions",
    )(route, offs_row)


def _dispatch_plan(route, counts, n_experts, tile, cap):
    cnt = counts[0, :n_experts].astype(jnp.int32)
    padded = ((cnt + tile - 1) // tile) * tile
    ends = jnp.cumsum(padded)
    offs = ends - padded
    nt = cap // tile
    n_used = (ends[-1] // tile).astype(jnp.int32)
    first_row = jnp.minimum(jnp.arange(nt, dtype=jnp.int32), n_used - 1) * tile
    te = jnp.sum((ends[None, :] <= first_row[:, None]).astype(jnp.int32), axis=1)
    te = jnp.minimum(te, n_experts - 1).astype(jnp.int32)
    run_start = jnp.concatenate([jnp.ones((1,), bool), te[1:] != te[:-1]])
    run_slot = ((jnp.cumsum(run_start.astype(jnp.int32)) - 1) % 2).astype(jnp.int32)
    ids = jnp.arange(n_experts, dtype=jnp.int32)
    later = (ids[None, :] > ids[:, None]) & (cnt > 0)[None, :]
    following = jnp.min(jnp.where(later, ids[None, :], n_experts), axis=1)
    following = jnp.where(following >= n_experts, -1, following).astype(jnp.int32)
    next_expert = following[te]
    offs_row = jnp.pad(offs.astype(F32), (0, LANES - n_experts)).reshape(1, LANES)
    pos_tiles = _positions(route, offs_row)
    return (te, n_used.reshape(1), pos_tiles, (offs + cnt).astype(jnp.int32), (padded - cnt).astype(jnp.int32),
            next_expert, run_slot)


def kernel(x, mix_norm_g, w_in, b_gate, q_norm_g, k_norm_g, lb_logits, hgrn_norm_g, conv_w, conv_b,
           conv_ln_g, conv_ln_b, w_att_o, w_hgrn_o, w_conv_o, w_out, ffn_norm_g, w_coarse, b_coarse,
           w_fine, b_fine, w_exp_in, w_exp_out):
    B, S, D = x.shape
    L = w_in.shape[0]
    T = B * S
    n_experts = w_exp_in.shape[1]
    cap = 2 * T + n_experts * EXPERT_TILE

    w_in_bf = w_in.astype(BF16)
    wa_bf, wh_bf, wc_bf, wo_bf = (w.astype(BF16) for w in (w_att_o, w_hgrn_o, w_conv_o, w_out))
    pad = LANES - N_GROUPS - n_experts
    w_route = jnp.concatenate([w_coarse, w_fine, jnp.zeros((L, D, pad), F32)], axis=-1).astype(F32)
    b_route = jnp.concatenate([b_coarse, b_fine, jnp.zeros((L, pad), F32)], axis=-1).astype(F32)

    x2d = x.reshape(T, D)
    for l in range(L):
        qkv, hf, hqig, cab, gl = _inproj(x2d, mix_norm_g[l], w_in_bf, l)
        att = _attention(qkv, q_norm_g[l], k_norm_g[l], B, S)
        rec = _hgrn(hqig, hf, lb_logits, hgrn_norm_g[l], l, B, S)
        cv = _conv(cab, conv_w[l], conv_b[l], conv_ln_g[l], conv_ln_b[l], B, S)
        x_mid, route, counts = _merge(att, rec, cv, gl, b_gate[l], x2d, wa_bf, wh_bf, wc_bf, wo_bf,
                                      ffn_norm_g[l], w_route[l], b_route[l].reshape(1, LANES), l)
        te, n_used, pos_tiles, pad_start, pad_len, next_expert, run_slot = _dispatch_plan(
            route, counts, n_experts, EXPERT_TILE, cap)
        x_sorted = _dispatch(x_mid, ffn_norm_g[l], pos_tiles, pad_start, pad_len, n_used, cap)
        y_sorted = _experts(x_sorted, w_exp_in, w_exp_out, te, n_used, next_expert, run_slot, l)
        x2d = _combine(x_mid, route, y_sorted, pos_tiles)
    return x2d.reshape(B, S, D)
```

```python
import functools

import jax
import jax.numpy as jnp
from jax import lax
from jax.experimental import pallas as pl
from jax.experimental.pallas import tpu as pltpu

F32 = jnp.float32
BF16 = jnp.bfloat16

LANES = 128
SUBLANES = 8
RMS_EPS = 1e-6
LN_EPS = 1e-5
NEG_INF = -1e30
LOG2_E = 1.4426950408889634

ATT_HEADS = 8
ATT_HEAD_DIM = 64
MOBA_BLOCK = 256
MOBA_TOPK = 3
HGRN_HEADS = 4
HGRN_DIM = 128
HGRN_ROWS = 256
CONV_WIDTH = 31
CONV_ROWS = 256
CONV_HALO = 32
N_GROUPS = 4
EXPERTS_PER_GROUP = 8
EXPERT_TILE = 256
TOKEN_TILE = 256
MERGE_TILE = 512
DMA_UNROLL = 8
CHUNK_ROWS = 4
VMEM_LIMIT = 56 * 1024 * 1024


def _params(n_axes, vmem=None):
    return pltpu.CompilerParams(dimension_semantics=("arbitrary",) * n_axes,
                                vmem_limit_bytes=vmem)


def _dot(a, b):
    return jnp.dot(a, b, preferred_element_type=F32)


def _dot_nt(a, b):
    return lax.dot_general(a, b, (((1,), (1,)), ((), ())), preferred_element_type=F32)


def _dot_tn(a, b):
    return lax.dot_general(a, b, (((0,), (0,)), ((), ())), preferred_element_type=F32)


def _split3(x):
    x1 = x.astype(BF16)
    r1 = x - x1.astype(F32)
    x2 = r1.astype(BF16)
    x3 = (r1 - x2.astype(F32)).astype(BF16)
    return x1, x2, x3


def _sigmoid(x):
    return 1.0 / (1.0 + jnp.exp(-x))


def _silu(x):
    return x * _sigmoid(x)


def _inproj_body(x_ref, g_ref, w_ref, qkv_ref, hf_ref, hqig_ref, cv_ref, gl_ref, *, segs, chunk):
    x = x_ref[...]
    ms = jnp.mean(x * x, axis=-1, keepdims=True)
    h = (x * lax.rsqrt(ms + RMS_EPS) * g_ref[...]).astype(BF16)
    refs = (qkv_ref, hf_ref, hqig_ref, cv_ref, gl_ref)
    for ridx, dst0, src0, width in segs:
        ref = refs[ridx]
        for c in range(0, width, chunk):
            y = _dot(h, w_ref[:, src0 + c:src0 + c + chunk])
            ref[:, dst0 + c:dst0 + c + chunk] = y.astype(ref.dtype)


def _inproj(x2d, norm_g, w_in_bf, layer):
    T, D = x2d.shape
    A = ATT_HEADS * ATT_HEAD_DIM
    K = HGRN_HEADS * HGRN_DIM
    C = A
    segs = ((0, 0, 0, 3 * A),
            (2, 0, 3 * A, K),
            (1, 0, 3 * A + K, K),
            (2, K, 3 * A + 2 * K, K),
            (2, 2 * K, 3 * A + 3 * K, K),
            (3, 0, 3 * A + 4 * K, 2 * C),
            (4, 0, 3 * A + 4 * K + 2 * C, 3 * D))
    ncols = w_in_bf.shape[-1]
    tm = TOKEN_TILE
    row = lambda i: (i, 0)
    return pl.pallas_call(
        functools.partial(_inproj_body, segs=segs, chunk=512),
        grid=(T // tm,),
        in_specs=[pl.BlockSpec((tm, D), row),
                  pl.BlockSpec((1, D), lambda i: (0, 0)),
                  pl.BlockSpec((None, D, ncols), lambda i: (layer, 0, 0))],
        out_specs=[pl.BlockSpec((tm, 3 * A), row), pl.BlockSpec((tm, K), row),
                   pl.BlockSpec((tm, 3 * K), row), pl.BlockSpec((tm, 2 * C), row),
                   pl.BlockSpec((tm, 3 * D), row)],
        out_shape=[jax.ShapeDtypeStruct((T, 3 * A), BF16), jax.ShapeDtypeStruct((T, K), F32),
                   jax.ShapeDtypeStruct((T, 3 * K), BF16), jax.ShapeDtypeStruct((T, 2 * C), BF16),
                   jax.ShapeDtypeStruct((T, 3 * D), BF16)],
        compiler_params=_params(1, VMEM_LIMIT),
        name="inproj",
    )(x2d, norm_g.reshape(1, D), w_in_bf)


def _attn_body(q_ref, k_ref, v_ref, gq_ref, gk_ref, o_ref, kb_ref, vt_ref, s_ref, *, seq, blk, topk):
    nb = seq // blk
    dh = ATT_HEAD_DIM
    dh_sh = dh.bit_length() - 1
    scale = dh ** -0.5 * LOG2_E
    lane = lax.broadcasted_iota(jnp.int32, (1, LANES), 1)
    hr = lax.broadcasted_iota(jnp.int32, (LANES, LANES), 0) >> dh_sh
    hc = lax.broadcasted_iota(jnp.int32, (LANES, LANES), 1) >> dh_sh
    same_head = jnp.where(hr == hc, 1.0, 0.0).astype(BF16)

    def head_norm(x, g):
        x2 = x * x
        hi = x2.astype(BF16)
        lo = (x2 - hi.astype(F32)).astype(BF16)
        ssq = _dot(hi, same_head) + _dot(lo, same_head)
        return x * lax.rsqrt(ssq * (1.0 / dh) + RMS_EPS) * g

    kmeans = []
    for j in range(nb):
        kn = head_norm(k_ref[j * blk:(j + 1) * blk, :].astype(F32), gk_ref[...])
        kb_ref[j * blk:(j + 1) * blk, :] = kn.astype(BF16)
        kmeans.append(jnp.mean(kn, axis=0, keepdims=True))
        vt_ref[:, j * blk:(j + 1) * blk] = v_ref[j * blk:(j + 1) * blk, :].astype(F32).T.astype(BF16)
    kmean = jnp.concatenate(kmeans, axis=0)
    km_hi = kmean.astype(BF16)
    km_lo = (kmean - km_hi.astype(F32)).astype(BF16)
    first_head = lane < dh
    first_head_t = lax.broadcasted_iota(jnp.int32, (LANES, 1), 0) < dh
    blk_n = lax.broadcasted_iota(jnp.int32, (nb, 1), 0)

    key_l = lax.broadcasted_iota(jnp.int32, (blk, 2 * blk), 0)
    qry_l = lax.broadcasted_iota(jnp.int32, (blk, 2 * blk), 1) & (blk - 1)
    causal = key_l <= qry_l

    for i in range(nb):
        rows = head_norm(q_ref[i * blk:(i + 1) * blk, :].astype(F32), gq_ref[...])
        qi = jnp.concatenate([jnp.where(first_head, rows, 0.0), jnp.where(first_head, 0.0, rows)], axis=0)
        qs = (qi * scale).astype(BF16)

        sel = None
        if i > topk:
            q_hi = qi.astype(BF16)
            q_lo = (qi - q_hi.astype(F32)).astype(BF16)
            gate = _dot_nt(km_hi, q_hi) + _dot_nt(km_lo, q_hi) + _dot_nt(km_hi, q_lo)
            valid = blk_n < i
            gm = jnp.where(valid, gate, NEG_INF)
            rank = jnp.zeros(gm.shape, F32)
            for r in range(1, nb):
                gr = pltpu.roll(gm, r, 0)
                ahead = (gr > gm) | ((gr == gm) & (blk_n >= r))
                rank = rank + jnp.where(ahead, 1.0, 0.0)
            sel = jnp.where(valid & (rank < topk), 1.0, 0.0)

        m = None
        for j in range(i + 1):
            s = _dot_nt(kb_ref[j * blk:(j + 1) * blk, :], qs)
            if j == i:
                s = jnp.where(causal, s, NEG_INF)
            elif sel is not None:
                s = jnp.where(sel[j:j + 1, :] > 0.5, s, NEG_INF)
            s_ref[j * blk:(j + 1) * blk, :] = s
            mj = jnp.max(s, axis=0, keepdims=True)
            m = mj if m is None else jnp.maximum(m, mj)

        l = None
        acc = None
        for j in range(i + 1):
            p = jnp.exp2(s_ref[j * blk:(j + 1) * blk, :] - m)
            lj = jnp.sum(p, axis=0, keepdims=True)
            aj = _dot(vt_ref[:, j * blk:(j + 1) * blk], p.astype(BF16))
            l = lj if l is None else l + lj
            acc = aj if acc is None else acc + aj

        ot = acc / l
        merged = jnp.where(first_head_t, ot[:, 0:blk], ot[:, blk:2 * blk])
        o_ref[i * blk:(i + 1) * blk, :] = merged.T.astype(o_ref.dtype)


def _attention(qkv, q_norm_g, k_norm_g, batch, seq):
    T = qkv.shape[0]
    A = ATT_HEADS * ATT_HEAD_DIM
    npair = A // LANES
    reps = LANES // ATT_HEAD_DIM
    gq = jnp.tile(q_norm_g.astype(F32), reps).reshape(1, LANES)
    gk = jnp.tile(k_norm_g.astype(F32), reps).reshape(1, LANES)
    blk_spec = lambda off: pl.BlockSpec((seq, LANES), lambda b, p: (b, off + p))
    vec = pl.BlockSpec((1, LANES), lambda b, p: (0, 0))
    return pl.pallas_call(
        functools.partial(_attn_body, seq=seq, blk=MOBA_BLOCK, topk=MOBA_TOPK),
        grid=(batch, npair),
        in_specs=[blk_spec(0), blk_spec(npair), blk_spec(2 * npair), vec, vec],
        out_specs=pl.BlockSpec((seq, LANES), lambda b, p: (b, p)),
        out_shape=jax.ShapeDtypeStruct((T, A), BF16),
        scratch_shapes=[pltpu.VMEM((seq, LANES), BF16), pltpu.VMEM((LANES, seq), BF16),
                        pltpu.VMEM((seq, 2 * MOBA_BLOCK), F32)],
        compiler_params=_params(2, VMEM_LIMIT),
        name="moba_attn",
    )(qkv, qkv, qkv, gq, gk)


def _hgrn_head(q_ref, f_ref, i_ref, g_ref, lbl_ref, ng_ref, tri_ref, lvl_ref, o_ref, st_ref, *,
               rows, layer, head):
    R = rows
    cols = slice(head * HGRN_DIM, (head + 1) * HGRN_DIM)

    lbl = lbl_ref[:, cols]
    e = jnp.exp(lbl - jnp.max(lbl, axis=0, keepdims=True))
    p = e / jnp.sum(e, axis=0, keepdims=True)
    lb = jnp.maximum(jnp.sum(p[0:layer + 1], axis=0, keepdims=True) - p[0:1], 0.0)

    sig = _sigmoid(f_ref[:, cols])
    g = jnp.log(lb + (1.0 - lb) * sig)
    kin = (1.0 - lb) * (1.0 - sig)
    qa = _silu(q_ref[:, cols].astype(F32))
    vb = i_ref[:, cols]

    t_col = lax.broadcasted_iota(jnp.int32, (R, 1), 0)

    tri = tri_ref[...]
    g1, g2, g3 = _split3(g)
    b = (_dot(tri, g1) + _dot(tri, g2) + _dot(tri, g3)) * LOG2_E

    half = R // 2
    scores = [None, None]
    cross = None
    m = half
    while m >= 1:
        w = 2 * m
        if w >= SUBLANES:
            b3 = b.reshape(R // w, w, LANES)
            bref = jnp.broadcast_to(b3[:, m - 1:m, :], b3.shape).reshape(R, LANES)
        else:
            tl = t_col & (w - 1)
            bref = None
            for resid in range(w):
                shift = resid - (m - 1)
                cand = b if shift == 0 else pltpu.roll(b, shift % R, 0)
                bref = cand if bref is None else jnp.where(tl == resid, cand, bref)
        e = jnp.exp2(-jnp.abs(b - bref))
        qt = (qa * e).astype(BF16)
        kt = (kin * e).astype(BF16)
        if w == R:
            cross = _dot_nt(qt[half:R], kt[0:half])
        else:
            keep = lvl_ref[...] == (m.bit_length() - 1)
            for hh in range(2):
                part = _dot_nt(qt[hh * half:(hh + 1) * half], kt[hh * half:(hh + 1) * half])
                scores[hh] = jnp.where(keep, part, 0.0 if scores[hh] is None else scores[hh])
        m //= 2

    st = st_ref[head]
    o = _dot_nt((qa * jnp.exp2(b)).astype(BF16), st.astype(BF16))
    o = o + jnp.sum(qa * kin, axis=-1, keepdims=True) * vb.astype(F32)
    o_lo = o[0:half] + _dot(scores[0].astype(BF16), vb[0:half])
    o_hi = o[half:R] + _dot(scores[1].astype(BF16), vb[half:R]) + _dot(cross.astype(BF16), vb[0:half])
    o = jnp.concatenate([o_lo, o_hi], axis=0)

    b_end = b[R - 1:R, :]
    k_end = (kin * jnp.exp2(b_end - b)).astype(BF16)
    st_ref[head] = st * jnp.exp2(b_end) + _dot_tn(vb, k_end)

    on = o * lax.rsqrt(jnp.mean(o * o, axis=-1, keepdims=True) + RMS_EPS) * ng_ref[...]
    o_ref[:, cols] = (on * _silu(g_ref[:, cols].astype(F32))).astype(o_ref.dtype)


def _hgrn_body(q_ref, f_ref, i_ref, g_ref, lbl_ref, ng_ref, tri_ref, lvl_ref, o_ref, st_ref, *, rows, layer):
    @pl.when(pl.program_id(1) == 0)
    def _():
        st_ref[...] = jnp.zeros(st_ref.shape, F32)

    for head in range(st_ref.shape[0]):
        _hgrn_head(q_ref, f_ref, i_ref, g_ref, lbl_ref, ng_ref, tri_ref, lvl_ref, o_ref, st_ref,
                   rows=rows, layer=layer, head=head)


def _level_table(n):
    t = jnp.arange(n, dtype=jnp.int32)[:, None]
    s = jnp.arange(n, dtype=jnp.int32)[None, :]
    x = t ^ s
    hb = jnp.zeros((n, n), jnp.int32)
    for k in range(1, n.bit_length()):
        hb = hb + (x >= (1 << k)).astype(jnp.int32)
    return jnp.where(t > s, hb, -1)


def _hgrn(hqig, hf, lb_logits, norm_g, layer, batch, seq):
    T = hf.shape[0]
    H = HGRN_HEADS
    R = HGRN_ROWS
    W = H * HGRN_DIM
    nc = seq // R
    L = lb_logits.shape[0]
    blk = lambda off: pl.BlockSpec((R, W), lambda b, c: (b * nc + c, off))
    const = lambda b, c: (0, 0)
    idx = jnp.arange(R, dtype=jnp.int32)
    tri = (idx[None, :] <= idx[:, None]).astype(BF16)
    return pl.pallas_call(
        functools.partial(_hgrn_body, rows=R, layer=layer),
        grid=(batch, nc),
        in_specs=[blk(0), blk(0), blk(1), blk(2),
                  pl.BlockSpec((L, W), const), pl.BlockSpec((1, LANES), const),
                  pl.BlockSpec((R, R), const), pl.BlockSpec((R // 2, R // 2), const)],
        out_specs=blk(0),
        out_shape=jax.ShapeDtypeStruct((T, W), BF16),
        scratch_shapes=[pltpu.VMEM((H, HGRN_DIM, HGRN_DIM), F32)],
        compiler_params=_params(2, VMEM_LIMIT),
        name="hgrn2",
    )(hqig, hf, hqig, hqig, lb_logits.astype(F32), norm_g.astype(F32).reshape(1, LANES), tri,
      _level_table(R // 2))


def _conv_body(ab_ref, w_ref, bdw_ref, lg_ref, lb_ref, o_ref, ubuf, cbuf, ush, *, rows, width, chans):
    R, W, C, H = rows, width, chans, CONV_HALO
    c = pl.program_id(1)

    @pl.when(c == 0)
    def _():
        ubuf[0:H, :] = jnp.zeros((H, C), F32)

    a = ab_ref[:, 0:C].astype(F32)
    gate = ab_ref[:, C:2 * C].astype(F32)
    ubuf[H:H + R, :] = a * _sigmoid(gate)

    n_sh = ush.shape[1]
    for s in range(1, SUBLANES):
        ush[s - 1] = ubuf[s:s + n_sh, :]

    rsub = 64
    for lb in range(0, C, LANES):
        for rb in range(0, R, rsub):
            acc = jnp.zeros((rsub, LANES), F32)
            for j in range(W):
                start = H - (W - 1) + j + rb
                s = start % SUBLANES
                a0 = start - s
                if s == 0:
                    u = ubuf[a0:a0 + rsub, lb:lb + LANES]
                else:
                    u = ush[s - 1, a0:a0 + rsub, lb:lb + LANES]
                acc = acc + w_ref[j:j + 1, lb:lb + LANES] * u
            cbuf[rb:rb + rsub, lb:lb + LANES] = acc

    uf = cbuf[...] + bdw_ref[...]
    mu = jnp.mean(uf, axis=-1, keepdims=True)
    d = uf - mu
    var = jnp.mean(d * d, axis=-1, keepdims=True)
    y = d * lax.rsqrt(var + LN_EPS) * lg_ref[...] + lb_ref[...]
    o_ref[...] = _silu(y).astype(o_ref.dtype)
    ubuf[0:H, :] = ubuf[R:R + H, :]


def _conv(cab, conv_w, conv_b, ln_g, ln_b, batch, seq):
    T = cab.shape[0]
    C = cab.shape[1] // 2
    W = conv_w.shape[0]
    R = CONV_ROWS
    nc = seq // R
    vec = pl.BlockSpec((1, C), lambda b, c: (0, 0))
    return pl.pallas_call(
        functools.partial(_conv_body, rows=R, width=W, chans=C),
        grid=(batch, nc),
        in_specs=[pl.BlockSpec((R, 2 * C), lambda b, c: (b * nc + c, 0)),
                  pl.BlockSpec((W, C), lambda b, c: (0, 0)), vec, vec, vec],
        out_specs=pl.BlockSpec((R, C), lambda b, c: (b * nc + c, 0)),
        out_shape=jax.ShapeDtypeStruct((T, C), BF16),
        scratch_shapes=[pltpu.VMEM((R + CONV_HALO, C), F32), pltpu.VMEM((R, C), F32),
                        pltpu.VMEM((SUBLANES - 1, R + CONV_HALO - SUBLANES, C), F32)],
        compiler_params=_params(2, VMEM_LIMIT),
        name="conformer_conv",
    )(cab, conv_w.astype(F32), conv_b.astype(F32).reshape(1, C), ln_g.astype(F32).reshape(1, C),
      ln_b.astype(F32).reshape(1, C))


def _merge_body(att_ref, rec_ref, cv_ref, gl_ref, bg_ref, x_ref, wa_ref, wh_ref, wc_ref, wo_ref,
                fg_ref, wr_ref, br_ref, before_ref, xo_ref, route_ref, cnt_out_ref, mg_ref, *,
                n_groups, per_group):
    D = x_ref.shape[1]
    att, rec, cv = att_ref[...], rec_ref[...], cv_ref[...]
    cw = 256
    for c in range(0, D, cw):
        def gate(k):
            return _sigmoid(gl_ref[:, k * D + c:k * D + c + cw].astype(F32) + bg_ref[:, k * D + c:k * D + c + cw])
        merged = (gate(0) * _dot(att, wa_ref[:, c:c + cw]) + gate(1) * _dot(rec, wh_ref[:, c:c + cw])
                  + gate(2) * _dot(cv, wc_ref[:, c:c + cw]))
        mg_ref[:, c:c + cw] = merged.astype(BF16)
    xn = x_ref[...] + _dot(mg_ref[...], wo_ref[...])
    xo_ref[...] = xn

    h = xn * lax.rsqrt(jnp.mean(xn * xn, axis=-1, keepdims=True) + RMS_EPS) * fg_ref[...]
    h_hi = h.astype(BF16)
    h_lo = (h - h_hi.astype(F32)).astype(BF16)
    wr = wr_ref[...]
    w_hi = wr.astype(BF16)
    w_lo = (wr - w_hi.astype(F32)).astype(BF16)
    logits = _dot(h_hi, w_hi) + _dot(h_hi, w_lo) + _dot(h_lo, w_hi) + br_ref[...]

    G, E = n_groups, per_group
    lane = lax.broadcasted_iota(jnp.int32, logits.shape, 1)
    big = jnp.int32(LANES)
    is_c = lane < G
    cl = jnp.where(is_c, logits, NEG_INF)
    cm = jnp.max(cl, axis=-1, keepdims=True)
    grp = jnp.min(jnp.where(cl == cm, lane, big), axis=-1, keepdims=True)
    se = jnp.sum(jnp.exp(jnp.where(is_c, logits - cm, NEG_INF)), axis=-1, keepdims=True)
    p_top = 1.0 / se
    lo = G + grp * E
    fl = jnp.where((lane >= lo) & (lane < lo + E), logits, NEG_INF)
    m1 = jnp.max(fl, axis=-1, keepdims=True)
    i1 = jnp.min(jnp.where(fl == m1, lane, big), axis=-1, keepdims=True)
    fl2 = jnp.where(lane == i1, NEG_INF, fl)
    m2 = jnp.max(fl2, axis=-1, keepdims=True)
    i2 = jnp.min(jnp.where(fl2 == m2, lane, big), axis=-1, keepdims=True)
    t = jnp.exp(m2 - m1)
    w1 = p_top / (1.0 + t)
    w2 = p_top * t / (1.0 + t)
    e1 = (i1 - G).astype(F32)
    e2 = (i2 - G).astype(F32)

    hit1 = lane == (i1 - G)
    hit2 = lane == (i2 - G)
    hits = jnp.where(hit1 | hit2, 1.0, 0.0)
    prefix = _dot(before_ref[...], hits.astype(BF16))
    rank1 = jnp.sum(jnp.where(hit1, prefix, 0.0), axis=-1, keepdims=True)
    rank2 = jnp.sum(jnp.where(hit2, prefix, 0.0), axis=-1, keepdims=True)
    for k in range(cnt_out_ref.shape[0]):
        cnt_out_ref[k] = jnp.sum(hits[k * TOKEN_TILE:(k + 1) * TOKEN_TILE], axis=0, keepdims=True)

    route_ref[...] = jnp.where(lane == 0, e1, jnp.where(lane == 1, e2,
                               jnp.where(lane == 2, w1, jnp.where(lane == 3, w2,
                               jnp.where(lane == 4, rank1, jnp.where(lane == 5, rank2, 0.0))))))


def _merge(att, rec, cv, gl, b_gate, x2d, wa, wh, wc, wo, ffn_g, w_route, b_route, layer):
    T, D = x2d.shape
    tm = MERGE_TILE
    row = lambda i: (i, 0)
    const2 = lambda i: (0, 0)
    wsel = lambda i: (layer, 0, 0)
    W = att.shape[1]
    idx = jnp.arange(tm, dtype=jnp.int32)
    sub = idx // TOKEN_TILE
    before = ((idx[None, :] < idx[:, None]) & (sub[None, :] == sub[:, None])).astype(BF16)
    nsub = tm // TOKEN_TILE
    return pl.pallas_call(
        functools.partial(_merge_body, n_groups=N_GROUPS, per_group=EXPERTS_PER_GROUP),
        grid=(T // tm,),
        in_specs=[pl.BlockSpec((tm, W), row), pl.BlockSpec((tm, W), row), pl.BlockSpec((tm, W), row),
                  pl.BlockSpec((tm, 3 * D), row), pl.BlockSpec((1, 3 * D), const2),
                  pl.BlockSpec((tm, D), row),
                  pl.BlockSpec((None, W, D), wsel), pl.BlockSpec((None, W, D), wsel),
                  pl.BlockSpec((None, W, D), wsel), pl.BlockSpec((None, D, D), wsel),
                  pl.BlockSpec((1, D), const2), pl.BlockSpec((D, LANES), const2),
                  pl.BlockSpec((1, LANES), const2), pl.BlockSpec((tm, tm), const2)],
        out_specs=[pl.BlockSpec((tm, D), row), pl.BlockSpec((tm, LANES), row),
                   pl.BlockSpec((nsub, 1, LANES), lambda i: (i, 0, 0))],
        out_shape=[jax.ShapeDtypeStruct((T, D), F32), jax.ShapeDtypeStruct((T, LANES), F32),
                   jax.ShapeDtypeStruct((T // TOKEN_TILE, 1, LANES), F32)],
        scratch_shapes=[pltpu.VMEM((tm, D), BF16)],
        compiler_params=_params(1, VMEM_LIMIT),
        name="merge_route",
    )(att, rec, cv, gl, b_gate.astype(F32).reshape(1, 3 * D), x2d, wa, wh, wc, wo,
      ffn_g.astype(F32).reshape(1, D), w_route, b_route, before)


def _bulk_wait(view, sem):
    pltpu.make_async_copy(view, view, sem).wait()


def _dispatch_body(ps_ref, pn_ref, nu_ref, tr_ref, dst_ref, x_ref, lrow_ref, fg_ref, xs_hbm,
                   hloc, zbuf, sem, zsem, *, tile, n_tiles, n_experts, n_sorted_tiles):
    i = pl.program_id(0)
    slot = i % 2
    ztile = zbuf.shape[0]
    loc_rows = hloc.shape[1]
    ch = CHUNK_ROWS

    def pad_copy(e, r):
        return pltpu.make_async_copy(zbuf.at[pl.ds(0, 1)], xs_hbm.at[pl.ds(ps_ref[e] + r, 1)], zsem)

    def tail_copy(t):
        return pltpu.make_async_copy(zbuf, xs_hbm.at[pl.ds(t * ztile, ztile)], zsem)

    big = 16 * ch

    def wait_chunks(s, step):
        def wait_rows(n):
            def one(k, carry):
                pltpu.make_async_copy(hloc.at[s, pl.ds(0, n)], hloc.at[s, pl.ds(0, n)], sem.at[s]).wait()
                return carry
            return one
        rows = tr_ref[step]
        lax.fori_loop(0, rows // big, wait_rows(big), 0)
        lax.fori_loop(0, (rows % big) // ch, wait_rows(ch), 0)

    @pl.when(i == 0)
    def _():
        zbuf[...] = jnp.zeros(zbuf.shape, F32)
        for e in range(n_experts):
            def start(r, carry, e=e):
                pad_copy(e, r).start()
                return carry
            lax.fori_loop(0, pn_ref[e], start, 0)

        def start_tail(t, carry):
            tail_copy(t).start()
            return carry
        lax.fori_loop(nu_ref[0], n_sorted_tiles, start_tail, 0)

    def scatter_tile(s):
        @pl.when(i >= 2)
        def _():
            wait_chunks(s, i - 2)

        x = x_ref[...]
        h = (x * lax.rsqrt(jnp.mean(x * x, axis=-1, keepdims=True) + RMS_EPS) * fg_ref[...]).astype(BF16)
        lrow = lrow_ref[0]
        r_id = lax.broadcasted_iota(jnp.int32, (loc_rows, tile), 0).astype(F32)
        pick = jnp.where((r_id == lrow[:, 0:tile]) | (r_id == lrow[:, tile:2 * tile]), 1.0, 0.0)
        hloc[s] = _dot(pick.astype(BF16), h).reshape(loc_rows, 1, h.shape[1])

        def issue(c, carry):
            for k in range(2):
                src = hloc.at[s, pl.ds((2 * c + k) * ch, ch)]
                dst = xs_hbm.at[pl.ds(dst_ref[0, 0, 2 * c + k], ch)]
                pltpu.make_async_copy(src, dst, sem.at[s]).start(priority=k)
            return carry
        lax.fori_loop(0, tr_ref[i] // (2 * ch), issue, 0)

        @pl.when((tr_ref[i] // ch) % 2 == 1)
        def _():
            c = tr_ref[i] // ch - 1
            pltpu.make_async_copy(hloc.at[s, pl.ds(c * ch, ch)], xs_hbm.at[pl.ds(dst_ref[0, 0, c], ch)],
                                  sem.at[s]).start()

    for s in range(2):
        pl.when(slot == s)(functools.partial(scatter_tile, s))

    @pl.when(i == n_tiles - 1)
    def _():
        wait_chunks((n_tiles - 1) % 2, n_tiles - 1)
        if n_tiles >= 2:
            wait_chunks(n_tiles % 2, n_tiles - 2)
        for e in range(n_experts):
            def done(r, carry, e=e):
                pad_copy(e, r).wait()
                return carry
            lax.fori_loop(0, pn_ref[e], done, 0)

        def done_tail(t, carry):
            tail_copy(t).wait()
            return carry
        lax.fori_loop(nu_ref[0], n_sorted_tiles, done_tail, 0)


def _dispatch(x2d, ffn_g, plan, cap):
    T, D = x2d.shape
    tile = TOKEN_TILE
    nt = T // tile
    n_experts = plan["pad_start"].shape[0]
    loc_rows = -(-(2 * tile + n_experts * (CHUNK_ROWS - 1)) // LANES) * LANES
    width = plan["chunk_dst"].shape[2]
    im2 = lambda i, ps, pn, nu, tr: (i, 0)
    im3 = lambda i, ps, pn, nu, tr: (i, 0, 0)
    grid_spec = pltpu.PrefetchScalarGridSpec(
        num_scalar_prefetch=4,
        grid=(nt,),
        in_specs=[pl.BlockSpec((1, 1, width), im3, memory_space=pltpu.SMEM),
                  pl.BlockSpec((tile, D), im2),
                  pl.BlockSpec((1, 1, 2 * tile), im3),
                  pl.BlockSpec((1, D), lambda i, ps, pn, nu, tr: (0, 0))],
        out_specs=pl.BlockSpec(memory_space=pl.ANY),
        scratch_shapes=[pltpu.VMEM((2, loc_rows, 1, D), F32), pltpu.VMEM((EXPERT_TILE, 1, D), F32),
                        pltpu.SemaphoreType.DMA((2,)), pltpu.SemaphoreType.DMA(())],
    )
    return pl.pallas_call(
        functools.partial(_dispatch_body, tile=tile, n_tiles=nt, n_experts=n_experts,
                          n_sorted_tiles=cap // EXPERT_TILE),
        grid_spec=grid_spec,
        out_shape=jax.ShapeDtypeStruct((cap, 1, D), F32),
        compiler_params=_params(1, VMEM_LIMIT),
        name="moe_dispatch",
    )(plan["pad_start"], plan["pad_len"], plan["n_used"], plan["tile_rows"], plan["chunk_dst"], x2d,
      plan["local_tiles"], ffn_g.astype(F32).reshape(1, D))


def _expert_body(te_ref, nu_ref, nx_ref, sl_ref, x_ref, win_hbm, wout_hbm, y_ref, win_bf, wout_bf, x2_ref,
                 win_st, wout_st, wsem, *, layer):
    i = pl.program_id(0)
    ff = wout_bf.shape[0]
    used = i < nu_ref[0]
    new_expert = jnp.logical_or(i == 0, te_ref[i] != te_ref[jnp.maximum(i - 1, 0)])

    def weight_copies(e, slot):
        return (pltpu.make_async_copy(win_hbm.at[layer, e], win_st.at[slot], wsem.at[0, slot]),
                pltpu.make_async_copy(wout_hbm.at[layer, e], wout_st.at[slot], wsem.at[1, slot]))

    @pl.when(i == 0)
    def _():
        for c in weight_copies(te_ref[0], 0):
            c.start()

    def begin_run(s):
        for c in weight_copies(te_ref[i], s):
            c.wait()
        win_bf[...] = win_st[s].astype(BF16)
        wout_bf[...] = wout_st[s].astype(BF16)

        @pl.when(nx_ref[i] >= 0)
        def _():
            for c in weight_copies(nx_ref[i], 1 - s):
                c.start()

    for s in range(2):
        pl.when(jnp.logical_and(jnp.logical_and(used, new_expert), sl_ref[i] == s))(
            functools.partial(begin_run, s))

    @pl.when(used)
    def _():
        x2_ref[...] = x_ref[...].reshape(x2_ref.shape)
        gu = _dot(x2_ref[...].astype(BF16), win_bf[...])
        act = (_silu(gu[:, 0:ff]) * gu[:, ff:2 * ff]).astype(BF16)
        y = _dot(act, wout_bf[...])
        y_ref[...] = y.reshape(y_ref.shape)

    @pl.when(jnp.logical_not(used))
    def _():
        y_ref[...] = jnp.zeros(y_ref.shape, F32)


def _experts(x_sorted, w_exp_in, w_exp_out, tile_expert, n_used, next_expert, run_slot, layer):
    cap, _, D = x_sorted.shape
    tile = EXPERT_TILE
    nt = cap // tile
    ff = w_exp_out.shape[2]
    rows = lambda i, te, nu, nx, sl: (i, 0, 0)
    grid_spec = pltpu.PrefetchScalarGridSpec(
        num_scalar_prefetch=4,
        grid=(nt,),
        in_specs=[pl.BlockSpec((tile, 1, D), rows),
                  pl.BlockSpec(memory_space=pl.ANY), pl.BlockSpec(memory_space=pl.ANY)],
        out_specs=pl.BlockSpec((tile, 1, D), rows),
        scratch_shapes=[pltpu.VMEM((D, 2 * ff), BF16), pltpu.VMEM((ff, D), BF16),
                        pltpu.VMEM((tile, D), F32),
                        pltpu.VMEM((2, D, 2 * ff), F32), pltpu.VMEM((2, ff, D), F32),
                        pltpu.SemaphoreType.DMA((2, 2))],
    )
    return pl.pallas_call(
        functools.partial(_expert_body, layer=layer),
        grid_spec=grid_spec,
        out_shape=jax.ShapeDtypeStruct((cap, 1, D), F32),
        compiler_params=_params(1, VMEM_LIMIT),
        name="expert_ffn",
    )(tile_expert, n_used, next_expert, run_slot, x_sorted, w_exp_in, w_exp_out)


def _combine_body(pos0_ref, posn_ref, y_hbm, x_ref, route_ref, o_ref, ybuf, y2_ref, sem, *, tile, n_tiles):
    i = pl.program_id(0)
    slot = i % 2

    def gather(pos_ref, s):
        def issue(g, carry):
            for k in range(DMA_UNROLL):
                r = g * DMA_UNROLL + k
                pltpu.make_async_copy(y_hbm.at[pl.ds(pos_ref[0, 0, r], 1)],
                                      ybuf.at[s, pl.ds(r, 1)], sem.at[s]).start(priority=k % 2)
            return carry
        lax.fori_loop(0, 2 * tile // DMA_UNROLL, issue, 0)

    @pl.when(i == 0)
    def _():
        gather(pos0_ref, 0)

    def combine_tile(s):
        @pl.when(i + 1 < n_tiles)
        def _():
            gather(posn_ref, 1 - s)

        _bulk_wait(ybuf.at[s], sem.at[s])
        y2_ref[...] = ybuf[s].reshape(y2_ref.shape)
        route = route_ref[...]
        o_ref[...] = (x_ref[...] + route[:, 2:3] * y2_ref[0:tile, :]
                      + route[:, 3:4] * y2_ref[tile:2 * tile, :])

    for s in range(2):
        pl.when(slot == s)(functools.partial(combine_tile, s))


def _combine(x2d, route, y_sorted, pos_tiles):
    T, D = x2d.shape
    tile = TOKEN_TILE
    nt = T // tile
    return pl.pallas_call(
        functools.partial(_combine_body, tile=tile, n_tiles=nt),
        grid=(nt,),
        in_specs=[pl.BlockSpec((1, 1, 2 * tile), lambda i: (0, 0, 0), memory_space=pltpu.SMEM),
                  pl.BlockSpec((1, 1, 2 * tile), lambda i: (jnp.minimum(i + 1, nt - 1), 0, 0),
                               memory_space=pltpu.SMEM),
                  pl.BlockSpec(memory_space=pl.ANY),
                  pl.BlockSpec((tile, D), lambda i: (i, 0)),
                  pl.BlockSpec((tile, LANES), lambda i: (i, 0))],
        out_specs=pl.BlockSpec((tile, D), lambda i: (i, 0)),
        out_shape=jax.ShapeDtypeStruct((T, D), F32),
        scratch_shapes=[pltpu.VMEM((2, 2 * tile, 1, D), F32), pltpu.VMEM((2 * tile, D), F32),
                        pltpu.SemaphoreType.DMA((2,))],
        compiler_params=_params(1, VMEM_LIMIT),
        name="moe_combine",
    )(pos_tiles, pos_tiles, y_sorted, x2d, route)


def _positions_body(route_ref, base_ref, local_ref, pos_ref, lrow_ref, *, tile):
    for k in range(route_ref.shape[0] // tile):
        r = route_ref[k * tile:(k + 1) * tile, :]
        lane = lax.broadcasted_iota(jnp.int32, r.shape, 1)
        lane_f = lane.astype(F32)
        hit1 = lane_f == r[:, 0:1]
        hit2 = lane_f == r[:, 1:2]

        def rows_of(table):
            a = jnp.sum(jnp.where(hit1, table, 0.0), axis=-1, keepdims=True) + r[:, 4:5]
            b = jnp.sum(jnp.where(hit2, table, 0.0), axis=-1, keepdims=True) + r[:, 5:6]
            return jnp.where(lane == 0, a, jnp.where(lane == 1, b, 0.0)).T

        t = rows_of(base_ref[k])
        pos_ref[k, :, 0:tile] = t[0:1, :].astype(jnp.int32)
        pos_ref[k, :, tile:2 * tile] = t[1:2, :].astype(jnp.int32)
        u = rows_of(local_ref[k])
        lrow_ref[k, :, 0:tile] = u[0:1, :]
        lrow_ref[k, :, tile:2 * tile] = u[1:2, :]


def _positions(route, base_rows, local_rows):
    T = route.shape[0]
    tile = TOKEN_TILE
    step = 4 * tile
    tab = pl.BlockSpec((step // tile, 1, LANES), lambda i: (i, 0, 0))
    out = pl.BlockSpec((step // tile, 1, 2 * tile), lambda i: (i, 0, 0))
    return pl.pallas_call(
        functools.partial(_positions_body, tile=tile),
        grid=(T // step,),
        in_specs=[pl.BlockSpec((step, LANES), lambda i: (i, 0)), tab, tab],
        out_specs=[out, out],
        out_shape=[jax.ShapeDtypeStruct((T // tile, 1, 2 * tile), jnp.int32),
                   jax.ShapeDtypeStruct((T // tile, 1, 2 * tile), F32)],
        compiler_params=_params(1, VMEM_LIMIT),
        name="moe_positions",
    )(route, base_rows, local_rows)


def _dispatch_plan(route, tile_counts, n_experts, tile, cap):
    ch = CHUNK_ROWS
    n = tile_counts[:, 0, :n_experts].astype(jnp.int32)
    run = ((n + ch - 1) // ch) * ch
    tot = jnp.sum(run, axis=0)
    padded = ((tot + tile - 1) // tile) * tile
    ends = jnp.cumsum(padded)
    offs = ends - padded
    base = offs[None, :] + jnp.cumsum(run, axis=0) - run
    local = jnp.cumsum(run, axis=1) - run
    nt = cap // tile
    n_used = (ends[-1] // tile).astype(jnp.int32)
    first_row = jnp.minimum(jnp.arange(nt, dtype=jnp.int32), n_used - 1) * tile
    te = jnp.sum((ends[None, :] <= first_row[:, None]).astype(jnp.int32), axis=1)
    te = jnp.minimum(te, n_experts - 1).astype(jnp.int32)
    run_start = jnp.concatenate([jnp.ones((1,), bool), te[1:] != te[:-1]])
    run_slot = ((jnp.cumsum(run_start.astype(jnp.int32)) - 1) % 2).astype(jnp.int32)
    ids = jnp.arange(n_experts, dtype=jnp.int32)
    later = (ids[None, :] > ids[:, None]) & (tot > 0)[None, :]
    following = jnp.min(jnp.where(later, ids[None, :], n_experts), axis=1)
    following = jnp.where(following >= n_experts, -1, following).astype(jnp.int32)
    next_expert = following[te]
    lanes = lambda a: jnp.pad(a, ((0, 0), (0, LANES - n_experts))).astype(F32)[:, None, :]
    pos_tiles, local_tiles = _positions(route, lanes(base), lanes(local))
    max_chunks = (2 * TOKEN_TILE + n_experts * (ch - 1)) // ch
    chunk_row = jnp.arange(max_chunks, dtype=jnp.int32)[None, :, None] * ch
    run_end = (local + run)[:, None, :]
    owner = jnp.sum((run_end <= chunk_row).astype(jnp.int32), axis=2)
    owner = jnp.minimum(owner, n_experts - 1)
    onehot = owner[:, :, None] == ids[None, None, :]
    dst = jnp.sum(jnp.where(onehot, (base - local)[:, None, :], 0), axis=2) + chunk_row[:, :, 0]
    width = -(-max_chunks // LANES) * LANES
    chunk_dst = jnp.pad(dst, ((0, 0), (0, width - max_chunks))).astype(jnp.int32)[:, None, :]
    return dict(te=te, n_used=n_used.reshape(1), pos_tiles=pos_tiles, pad_start=(offs + tot).astype(jnp.int32),
                pad_len=(padded - tot).astype(jnp.int32), next_expert=next_expert, run_slot=run_slot,
                chunk_dst=chunk_dst, local_tiles=local_tiles,
                tile_rows=jnp.sum(run, axis=1).astype(jnp.int32))


def kernel(x, mix_norm_g, w_in, b_gate, q_norm_g, k_norm_g, lb_logits, hgrn_norm_g, conv_w, conv_b,
           conv_ln_g, conv_ln_b, w_att_o, w_hgrn_o, w_conv_o, w_out, ffn_norm_g, w_coarse, b_coarse,
           w_fine, b_fine, w_exp_in, w_exp_out):
    B, S, D = x.shape
    L = w_in.shape[0]
    T = B * S
    n_experts = w_exp_in.shape[1]
    worst = 2 * T + (T // TOKEN_TILE) * n_experts * (CHUNK_ROWS - 1) + n_experts * (EXPERT_TILE - 1)
    cap = -(-worst // EXPERT_TILE) * EXPERT_TILE

    w_in_bf = w_in.astype(BF16)
    wa_bf, wh_bf, wc_bf, wo_bf = (w.astype(BF16) for w in (w_att_o, w_hgrn_o, w_conv_o, w_out))
    pad = LANES - N_GROUPS - n_experts
    w_route = jnp.concatenate([w_coarse, w_fine, jnp.zeros((L, D, pad), F32)], axis=-1).astype(F32)
    b_route = jnp.concatenate([b_coarse, b_fine, jnp.zeros((L, pad), F32)], axis=-1).astype(F32)

    x2d = x.reshape(T, D)
    for l in range(L):
        qkv, hf, hqig, cab, gl = _inproj(x2d, mix_norm_g[l], w_in_bf, l)
        att = _attention(qkv, q_norm_g[l], k_norm_g[l], B, S)
        rec = _hgrn(hqig, hf, lb_logits, hgrn_norm_g[l], l, B, S)
        cv = _conv(cab, conv_w[l], conv_b[l], conv_ln_g[l], conv_ln_b[l], B, S)
        x_mid, route, counts = _merge(att, rec, cv, gl, b_gate[l], x2d, wa_bf, wh_bf, wc_bf, wo_bf,
                                      ffn_norm_g[l], w_route[l], b_route[l].reshape(1, LANES), l)
        plan = _dispatch_plan(route, counts, n_experts, EXPERT_TILE, cap)
        x_sorted = _dispatch(x_mid, ffn_norm_g[l], plan, cap)
        y_sorted = _experts(x_sorted, w_exp_in, w_exp_out, plan["te"], plan["n_used"], plan["next_expert"],
                            plan["run_slot"], l)
        x2d = _combine(x_mid, route, y_sorted, plan["pos_tiles"])
    return x2d.reshape(B, S, D)
```

```python
import functools

import jax
import jax.numpy as jnp
import numpy as np
from jax import lax
from jax.experimental import pallas as pl
from jax.experimental.pallas import tpu as pltpu

F32 = jnp.float32
BF16 = jnp.bfloat16

LANES = 128
SUBLANES = 8
RMS_EPS = 1e-6
LN_EPS = 1e-5
NEG_INF = -1e30
LOG2_E = 1.4426950408889634

ATT_HEADS = 8
ATT_HEAD_DIM = 64
MOBA_BLOCK = 256
MOBA_TOPK = 3
HGRN_HEADS = 4
HGRN_DIM = 128
HGRN_ROWS = 256
CONV_WIDTH = 31
CONV_ROWS = 512
CONV_HALO = 32
N_GROUPS = 4
EXPERTS_PER_GROUP = 8
EXPERT_TILE = 256
TOKEN_TILE = 256
INPROJ_TILE = 512
MERGE_TILE = 512
DMA_UNROLL = 8
VMEM_LIMIT = 56 * 1024 * 1024


def _params(n_axes, vmem=None):
    return pltpu.CompilerParams(dimension_semantics=("arbitrary",) * n_axes,
                                vmem_limit_bytes=vmem)


def _dot(a, b):
    return jnp.dot(a, b, preferred_element_type=F32)


def _dot_nt(a, b):
    return lax.dot_general(a, b, (((1,), (1,)), ((), ())), preferred_element_type=F32)


def _dot_tn(a, b):
    return lax.dot_general(a, b, (((0,), (0,)), ((), ())), preferred_element_type=F32)


def _split3(x):
    x1 = x.astype(BF16)
    r1 = x - x1.astype(F32)
    x2 = r1.astype(BF16)
    x3 = (r1 - x2.astype(F32)).astype(BF16)
    return x1, x2, x3


def _sigmoid(x):
    return 1.0 / (1.0 + jnp.exp(-x))


def _silu(x):
    return x * _sigmoid(x)


def _inproj_body(x_ref, g_ref, w_ref, qkv_ref, hf_ref, hqig_ref, cv_ref, gl_ref, *, segs, chunk):
    x = x_ref[...]
    ms = jnp.mean(x * x, axis=-1, keepdims=True)
    h = (x * lax.rsqrt(ms + RMS_EPS) * g_ref[...]).astype(BF16)
    refs = (qkv_ref, hf_ref, hqig_ref, cv_ref, gl_ref)
    for ridx, dst0, src0, width in segs:
        ref = refs[ridx]
        for c in range(0, width, chunk):
            y = _dot(h, w_ref[:, src0 + c:src0 + c + chunk])
            ref[:, dst0 + c:dst0 + c + chunk] = y.astype(ref.dtype)


def _inproj(x2d, norm_g, w_in_bf, layer):
    T, D = x2d.shape
    A = ATT_HEADS * ATT_HEAD_DIM
    K = HGRN_HEADS * HGRN_DIM
    C = A
    segs = ((0, 0, 0, 3 * A),
            (2, 0, 3 * A, K),
            (1, 0, 3 * A + K, K),
            (2, K, 3 * A + 2 * K, K),
            (2, 2 * K, 3 * A + 3 * K, K),
            (3, 0, 3 * A + 4 * K, 2 * C),
            (4, 0, 3 * A + 4 * K + 2 * C, 3 * D))
    ncols = w_in_bf.shape[-1]
    tm = INPROJ_TILE
    row = lambda i: (i, 0)
    return pl.pallas_call(
        functools.partial(_inproj_body, segs=segs, chunk=512),
        grid=(T // tm,),
        in_specs=[pl.BlockSpec((tm, D), row),
                  pl.BlockSpec((1, D), lambda i: (0, 0)),
                  pl.BlockSpec((None, D, ncols), lambda i: (layer, 0, 0))],
        out_specs=[pl.BlockSpec((tm, 3 * A), row), pl.BlockSpec((tm, K), row),
                   pl.BlockSpec((tm, 3 * K), row), pl.BlockSpec((tm, 2 * C), row),
                   pl.BlockSpec((tm, 3 * D), row)],
        out_shape=[jax.ShapeDtypeStruct((T, 3 * A), BF16), jax.ShapeDtypeStruct((T, K), F32),
                   jax.ShapeDtypeStruct((T, 3 * K), BF16), jax.ShapeDtypeStruct((T, 2 * C), BF16),
                   jax.ShapeDtypeStruct((T, 3 * D), BF16)],
        compiler_params=_params(1, VMEM_LIMIT),
        name="inproj",
    )(x2d, norm_g.reshape(1, D), w_in_bf)


def _attn_body(q_ref, k_ref, v_ref, gq_ref, gk_ref, o_ref, kb_ref, vt_ref, s_ref, *, seq, blk, topk):
    nb = seq // blk
    dh = ATT_HEAD_DIM
    dh_sh = dh.bit_length() - 1
    scale = dh ** -0.5 * LOG2_E
    lane = lax.broadcasted_iota(jnp.int32, (1, LANES), 1)
    hr = lax.broadcasted_iota(jnp.int32, (LANES, LANES), 0) >> dh_sh
    hc = lax.broadcasted_iota(jnp.int32, (LANES, LANES), 1) >> dh_sh
    same_head = jnp.where(hr == hc, 1.0, 0.0).astype(BF16)

    def head_norm(x, g):
        x2 = x * x
        hi = x2.astype(BF16)
        lo = (x2 - hi.astype(F32)).astype(BF16)
        ssq = _dot(hi, same_head) + _dot(lo, same_head)
        return x * lax.rsqrt(ssq * (1.0 / dh) + RMS_EPS) * g

    kmeans = []
    for j in range(nb):
        kn = head_norm(k_ref[j * blk:(j + 1) * blk, :].astype(F32), gk_ref[...])
        kb_ref[j * blk:(j + 1) * blk, :] = kn.astype(BF16)
        kmeans.append(jnp.mean(kn, axis=0, keepdims=True))
        vt_ref[:, j * blk:(j + 1) * blk] = v_ref[j * blk:(j + 1) * blk, :].astype(F32).T.astype(BF16)
    kmean = jnp.concatenate(kmeans, axis=0)
    km_hi = kmean.astype(BF16)
    km_lo = (kmean - km_hi.astype(F32)).astype(BF16)
    first_head = lane < dh
    first_head_t = lax.broadcasted_iota(jnp.int32, (LANES, 1), 0) < dh
    blk_n = lax.broadcasted_iota(jnp.int32, (nb, 1), 0)

    key_l = lax.broadcasted_iota(jnp.int32, (blk, 2 * blk), 0)
    qry_l = lax.broadcasted_iota(jnp.int32, (blk, 2 * blk), 1) & (blk - 1)
    causal = key_l <= qry_l

    for i in range(nb):
        rows = head_norm(q_ref[i * blk:(i + 1) * blk, :].astype(F32), gq_ref[...])
        qi = jnp.concatenate([jnp.where(first_head, rows, 0.0), jnp.where(first_head, 0.0, rows)], axis=0)
        qs = (qi * scale).astype(BF16)

        sel = None
        if i > topk:
            q_hi = qi.astype(BF16)
            q_lo = (qi - q_hi.astype(F32)).astype(BF16)
            gate = _dot_nt(km_hi, q_hi) + _dot_nt(km_lo, q_hi) + _dot_nt(km_hi, q_lo)
            valid = blk_n < i
            gm = jnp.where(valid, gate, NEG_INF)
            rank = jnp.zeros(gm.shape, F32)
            for r in range(1, nb):
                gr = pltpu.roll(gm, r, 0)
                ahead = (gr > gm) | ((gr == gm) & (blk_n >= r))
                rank = rank + jnp.where(ahead, 1.0, 0.0)
            sel = jnp.where(valid & (rank < topk), 1.0, 0.0)

        m = None
        for j in range(i + 1):
            s = _dot_nt(kb_ref[j * blk:(j + 1) * blk, :], qs)
            if j == i:
                s = jnp.where(causal, s, NEG_INF)
            elif sel is not None:
                s = jnp.where(sel[j:j + 1, :] > 0.5, s, NEG_INF)
            s_ref[j * blk:(j + 1) * blk, :] = s
            mj = jnp.max(s, axis=0, keepdims=True)
            m = mj if m is None else jnp.maximum(m, mj)

        l = None
        acc = None
        for j in range(i + 1):
            p = jnp.exp2(s_ref[j * blk:(j + 1) * blk, :] - m)
            lj = jnp.sum(p, axis=0, keepdims=True)
            aj = _dot(vt_ref[:, j * blk:(j + 1) * blk], p.astype(BF16))
            l = lj if l is None else l + lj
            acc = aj if acc is None else acc + aj

        ot = acc / l
        merged = jnp.where(first_head_t, ot[:, 0:blk], ot[:, blk:2 * blk])
        o_ref[i * blk:(i + 1) * blk, :] = merged.T.astype(o_ref.dtype)


def _attention(qkv, q_norm_g, k_norm_g, batch, seq):
    T = qkv.shape[0]
    A = ATT_HEADS * ATT_HEAD_DIM
    npair = A // LANES
    reps = LANES // ATT_HEAD_DIM
    gq = jnp.tile(q_norm_g.astype(F32), reps).reshape(1, LANES)
    gk = jnp.tile(k_norm_g.astype(F32), reps).reshape(1, LANES)
    blk_spec = lambda off: pl.BlockSpec((seq, LANES), lambda b, p: (b, off + p))
    vec = pl.BlockSpec((1, LANES), lambda b, p: (0, 0))
    return pl.pallas_call(
        functools.partial(_attn_body, seq=seq, blk=MOBA_BLOCK, topk=MOBA_TOPK),
        grid=(batch, npair),
        in_specs=[blk_spec(0), blk_spec(npair), blk_spec(2 * npair), vec, vec],
        out_specs=pl.BlockSpec((seq, LANES), lambda b, p: (b, p)),
        out_shape=jax.ShapeDtypeStruct((T, A), BF16),
        scratch_shapes=[pltpu.VMEM((seq, LANES), BF16), pltpu.VMEM((LANES, seq), BF16),
                        pltpu.VMEM((seq, 2 * MOBA_BLOCK), F32)],
        compiler_params=_params(2, VMEM_LIMIT),
        name="moba_attn",
    )(qkv, qkv, qkv, gq, gk)


def _hgrn_head(q_ref, f_ref, i_ref, g_ref, lbl_ref, ng_ref, tri_ref, lvl_ref, o_ref, st_ref, *,
               rows, layer, head):
    R = rows
    cols = slice(head * HGRN_DIM, (head + 1) * HGRN_DIM)

    lbl = lbl_ref[:, cols]
    e = jnp.exp(lbl - jnp.max(lbl, axis=0, keepdims=True))
    p = e / jnp.sum(e, axis=0, keepdims=True)
    lb = jnp.maximum(jnp.sum(p[0:layer + 1], axis=0, keepdims=True) - p[0:1], 0.0)

    sig = _sigmoid(f_ref[:, cols])
    g = jnp.log(lb + (1.0 - lb) * sig)
    kin = (1.0 - lb) * (1.0 - sig)
    qa = _silu(q_ref[:, cols].astype(F32))
    vb = i_ref[:, cols]

    t_col = lax.broadcasted_iota(jnp.int32, (R, 1), 0)

    tri = tri_ref[...]
    g1, g2, g3 = _split3(g)
    b = (_dot(tri, g1) + _dot(tri, g2) + _dot(tri, g3)) * LOG2_E

    half = R // 2
    scores = [None, None]
    cross = None
    m = half
    while m >= 1:
        w = 2 * m
        if w >= SUBLANES:
            b3 = b.reshape(R // w, w, LANES)
            bref = jnp.broadcast_to(b3[:, m - 1:m, :], b3.shape).reshape(R, LANES)
        else:
            tl = t_col & (w - 1)
            bref = None
            for resid in range(w):
                shift = resid - (m - 1)
                cand = b if shift == 0 else pltpu.roll(b, shift % R, 0)
                bref = cand if bref is None else jnp.where(tl == resid, cand, bref)
        e = jnp.exp2(-jnp.abs(b - bref))
        qt = (qa * e).astype(BF16)
        kt = (kin * e).astype(BF16)
        if w == R:
            cross = _dot_nt(qt[half:R], kt[0:half])
        else:
            keep = lvl_ref[...] == (m.bit_length() - 1)
            for hh in range(2):
                part = _dot_nt(qt[hh * half:(hh + 1) * half], kt[hh * half:(hh + 1) * half])
                scores[hh] = jnp.where(keep, part, 0.0 if scores[hh] is None else scores[hh])
        m //= 2

    st = st_ref[head]
    o = _dot_nt((qa * jnp.exp2(b)).astype(BF16), st.astype(BF16))
    o = o + jnp.sum(qa * kin, axis=-1, keepdims=True) * vb.astype(F32)
    o_lo = o[0:half] + _dot(scores[0].astype(BF16), vb[0:half])
    o_hi = o[half:R] + _dot(scores[1].astype(BF16), vb[half:R]) + _dot(cross.astype(BF16), vb[0:half])
    o = jnp.concatenate([o_lo, o_hi], axis=0)

    b_end = b[R - 1:R, :]
    k_end = (kin * jnp.exp2(b_end - b)).astype(BF16)
    st_ref[head] = st * jnp.exp2(b_end) + _dot_tn(vb, k_end)

    on = o * lax.rsqrt(jnp.mean(o * o, axis=-1, keepdims=True) + RMS_EPS) * ng_ref[...]
    o_ref[:, cols] = (on * _silu(g_ref[:, cols].astype(F32))).astype(o_ref.dtype)


def _hgrn_body(q_ref, f_ref, i_ref, g_ref, lbl_ref, ng_ref, tri_ref, lvl_ref, o_ref, st_ref, *, rows, layer):
    @pl.when(pl.program_id(1) == 0)
    def _():
        st_ref[...] = jnp.zeros(st_ref.shape, F32)

    for head in range(st_ref.shape[0]):
        _hgrn_head(q_ref, f_ref, i_ref, g_ref, lbl_ref, ng_ref, tri_ref, lvl_ref, o_ref, st_ref,
                   rows=rows, layer=layer, head=head)


def _level_table(n):
    t = np.arange(n, dtype=np.int32)[:, None]
    s = np.arange(n, dtype=np.int32)[None, :]
    x = t ^ s
    hb = np.zeros((n, n), np.int32)
    for k in range(1, n.bit_length()):
        hb = hb + (x >= (1 << k)).astype(np.int32)
    return jnp.asarray(np.where(t > s, hb, -1).astype(np.int32))


def _hgrn(hqig, hf, lb_logits, norm_g, layer, batch, seq):
    T = hf.shape[0]
    H = HGRN_HEADS
    R = HGRN_ROWS
    W = H * HGRN_DIM
    nc = seq // R
    L = lb_logits.shape[0]
    blk = lambda off: pl.BlockSpec((R, W), lambda b, c: (b * nc + c, off))
    const = lambda b, c: (0, 0)
    tri = jnp.asarray(np.tril(np.ones((R, R), np.float32)), dtype=BF16)
    return pl.pallas_call(
        functools.partial(_hgrn_body, rows=R, layer=layer),
        grid=(batch, nc),
        in_specs=[blk(0), blk(0), blk(1), blk(2),
                  pl.BlockSpec((L, W), const), pl.BlockSpec((1, LANES), const),
                  pl.BlockSpec((R, R), const), pl.BlockSpec((R // 2, R // 2), const)],
        out_specs=blk(0),
        out_shape=jax.ShapeDtypeStruct((T, W), BF16),
        scratch_shapes=[pltpu.VMEM((H, HGRN_DIM, HGRN_DIM), F32)],
        compiler_params=_params(2, VMEM_LIMIT),
        name="hgrn2",
    )(hqig, hf, hqig, hqig, lb_logits.astype(F32), norm_g.astype(F32).reshape(1, LANES), tri,
      _level_table(R // 2))


def _conv_body(ab_ref, w_ref, bdw_ref, lg_ref, lb_ref, o_ref, ubuf, cbuf, ush, *, rows, width, chans):
    R, W, C, H = rows, width, chans, CONV_HALO
    c = pl.program_id(1)

    @pl.when(c == 0)
    def _():
        ubuf[0:H, :] = jnp.zeros((H, C), F32)

    a = ab_ref[:, 0:C].astype(F32)
    gate = ab_ref[:, C:2 * C].astype(F32)
    ubuf[H:H + R, :] = a * _sigmoid(gate)

    n_sh = ush.shape[1]
    for s in range(1, SUBLANES):
        ush[s - 1] = ubuf[s:s + n_sh, :]

    rsub = 64
    for lb in range(0, C, LANES):
        for rb in range(0, R, rsub):
            acc = jnp.zeros((rsub, LANES), F32)
            for j in range(W):
                start = H - (W - 1) + j + rb
                s = start % SUBLANES
                a0 = start - s
                if s == 0:
                    u = ubuf[a0:a0 + rsub, lb:lb + LANES]
                else:
                    u = ush[s - 1, a0:a0 + rsub, lb:lb + LANES]
                acc = acc + w_ref[j:j + 1, lb:lb + LANES] * u
            cbuf[rb:rb + rsub, lb:lb + LANES] = acc

    uf = cbuf[...] + bdw_ref[...]
    mu = jnp.mean(uf, axis=-1, keepdims=True)
    d = uf - mu
    var = jnp.mean(d * d, axis=-1, keepdims=True)
    y = d * lax.rsqrt(var + LN_EPS) * lg_ref[...] + lb_ref[...]
    o_ref[...] = _silu(y).astype(o_ref.dtype)
    ubuf[0:H, :] = ubuf[R:R + H, :]


def _conv(cab, conv_w, conv_b, ln_g, ln_b, batch, seq):
    T = cab.shape[0]
    C = cab.shape[1] // 2
    W = conv_w.shape[0]
    R = CONV_ROWS
    nc = seq // R
    vec = pl.BlockSpec((1, C), lambda b, c: (0, 0))
    return pl.pallas_call(
        functools.partial(_conv_body, rows=R, width=W, chans=C),
        grid=(batch, nc),
        in_specs=[pl.BlockSpec((R, 2 * C), lambda b, c: (b * nc + c, 0)),
                  pl.BlockSpec((W, C), lambda b, c: (0, 0)), vec, vec, vec],
        out_specs=pl.BlockSpec((R, C), lambda b, c: (b * nc + c, 0)),
        out_shape=jax.ShapeDtypeStruct((T, C), BF16),
        scratch_shapes=[pltpu.VMEM((R + CONV_HALO, C), F32), pltpu.VMEM((R, C), F32),
                        pltpu.VMEM((SUBLANES - 1, R + CONV_HALO - SUBLANES, C), F32)],
        compiler_params=_params(2, VMEM_LIMIT),
        name="conformer_conv",
    )(cab, conv_w.astype(F32), conv_b.astype(F32).reshape(1, C), ln_g.astype(F32).reshape(1, C),
      ln_b.astype(F32).reshape(1, C))


def _merge_body(att_ref, rec_ref, cv_ref, gl_ref, bg_ref, x_ref, wa_ref, wh_ref, wc_ref, wo_ref,
                fg_ref, wr_ref, br_ref, before_ref, xo_ref, route_ref, cnt_out_ref, mg_ref, cnt_ref, *,
                n_groups, per_group):
    D = x_ref.shape[1]
    att, rec, cv = att_ref[...], rec_ref[...], cv_ref[...]
    cw = 256
    for c in range(0, D, cw):
        def gate(k):
            return _sigmoid(gl_ref[:, k * D + c:k * D + c + cw].astype(F32) + bg_ref[:, k * D + c:k * D + c + cw])
        merged = (gate(0) * _dot(att, wa_ref[:, c:c + cw]) + gate(1) * _dot(rec, wh_ref[:, c:c + cw])
                  + gate(2) * _dot(cv, wc_ref[:, c:c + cw]))
        mg_ref[:, c:c + cw] = merged.astype(BF16)
    xn = x_ref[...] + _dot(mg_ref[...], wo_ref[...])
    xo_ref[...] = xn

    h = xn * lax.rsqrt(jnp.mean(xn * xn, axis=-1, keepdims=True) + RMS_EPS) * fg_ref[...]
    h_hi = h.astype(BF16)
    h_lo = (h - h_hi.astype(F32)).astype(BF16)
    wr = wr_ref[...]
    w_hi = wr.astype(BF16)
    w_lo = (wr - w_hi.astype(F32)).astype(BF16)
    logits = _dot(h_hi, w_hi) + _dot(h_hi, w_lo) + _dot(h_lo, w_hi) + br_ref[...]

    G, E = n_groups, per_group
    lane = lax.broadcasted_iota(jnp.int32, logits.shape, 1)
    big = jnp.int32(LANES)
    is_c = lane < G
    cl = jnp.where(is_c, logits, NEG_INF)
    cm = jnp.max(cl, axis=-1, keepdims=True)
    grp = jnp.min(jnp.where(cl == cm, lane, big), axis=-1, keepdims=True)
    se = jnp.sum(jnp.exp(jnp.where(is_c, logits - cm, NEG_INF)), axis=-1, keepdims=True)
    p_top = 1.0 / se
    lo = G + grp * E
    fl = jnp.where((lane >= lo) & (lane < lo + E), logits, NEG_INF)
    m1 = jnp.max(fl, axis=-1, keepdims=True)
    i1 = jnp.min(jnp.where(fl == m1, lane, big), axis=-1, keepdims=True)
    fl2 = jnp.where(lane == i1, NEG_INF, fl)
    m2 = jnp.max(fl2, axis=-1, keepdims=True)
    i2 = jnp.min(jnp.where(fl2 == m2, lane, big), axis=-1, keepdims=True)
    t = jnp.exp(m2 - m1)
    w1 = p_top / (1.0 + t)
    w2 = p_top * t / (1.0 + t)
    e1 = (i1 - G).astype(F32)
    e2 = (i2 - G).astype(F32)

    @pl.when(pl.program_id(0) == 0)
    def _():
        cnt_ref[...] = jnp.zeros(cnt_ref.shape, F32)

    hit1 = lane == (i1 - G)
    hit2 = lane == (i2 - G)
    hits = jnp.where(hit1 | hit2, 1.0, 0.0)
    prefix = _dot(before_ref[...], hits.astype(BF16)) + cnt_ref[...]
    rank1 = jnp.sum(jnp.where(hit1, prefix, 0.0), axis=-1, keepdims=True)
    rank2 = jnp.sum(jnp.where(hit2, prefix, 0.0), axis=-1, keepdims=True)
    cnt = cnt_ref[...] + jnp.sum(hits, axis=0, keepdims=True)
    cnt_ref[...] = cnt
    cnt_out_ref[...] = cnt

    route_ref[...] = jnp.where(lane == 0, e1, jnp.where(lane == 1, e2,
                               jnp.where(lane == 2, w1, jnp.where(lane == 3, w2,
                               jnp.where(lane == 4, rank1, jnp.where(lane == 5, rank2, 0.0))))))


def _merge(att, rec, cv, gl, b_gate, x2d, wa, wh, wc, wo, ffn_g, w_route, b_route, layer):
    T, D = x2d.shape
    tm = MERGE_TILE
    row = lambda i: (i, 0)
    const2 = lambda i: (0, 0)
    wsel = lambda i: (layer, 0, 0)
    W = att.shape[1]
    before = jnp.asarray(np.tril(np.ones((tm, tm), np.float32), -1), dtype=BF16)
    return pl.pallas_call(
        functools.partial(_merge_body, n_groups=N_GROUPS, per_group=EXPERTS_PER_GROUP),
        grid=(T // tm,),
        in_specs=[pl.BlockSpec((tm, W), row), pl.BlockSpec((tm, W), row), pl.BlockSpec((tm, W), row),
                  pl.BlockSpec((tm, 3 * D), row), pl.BlockSpec((1, 3 * D), const2),
                  pl.BlockSpec((tm, D), row),
                  pl.BlockSpec((None, W, D), wsel), pl.BlockSpec((None, W, D), wsel),
                  pl.BlockSpec((None, W, D), wsel), pl.BlockSpec((None, D, D), wsel),
                  pl.BlockSpec((1, D), const2), pl.BlockSpec((D, LANES), const2),
                  pl.BlockSpec((1, LANES), const2), pl.BlockSpec((tm, tm), const2)],
        out_specs=[pl.BlockSpec((tm, D), row), pl.BlockSpec((tm, LANES), row),
                   pl.BlockSpec((1, LANES), const2)],
        out_shape=[jax.ShapeDtypeStruct((T, D), F32), jax.ShapeDtypeStruct((T, LANES), F32),
                   jax.ShapeDtypeStruct((1, LANES), F32)],
        scratch_shapes=[pltpu.VMEM((tm, D), BF16), pltpu.VMEM((1, LANES), F32)],
        compiler_params=_params(1, VMEM_LIMIT),
        name="merge_route",
    )(att, rec, cv, gl, b_gate.astype(F32).reshape(1, 3 * D), x2d, wa, wh, wc, wo,
      ffn_g.astype(F32).reshape(1, D), w_route, b_route, before)


def _bulk_wait(view, sem):
    pltpu.make_async_copy(view, view, sem).wait()


def _dispatch_body(ps_ref, pn_ref, nu_ref, pos_ref, x_ref, fg_ref, xs_hbm, hbuf, zbuf, sem, zsem, *,
                   tile, n_tiles, n_experts, n_sorted_tiles):
    i = pl.program_id(0)
    slot = i % 2
    ztile = zbuf.shape[0]

    def pad_copy(e, r):
        return pltpu.make_async_copy(zbuf.at[pl.ds(0, 1)], xs_hbm.at[pl.ds(ps_ref[e] + r, 1)], zsem)

    def tail_copy(t):
        return pltpu.make_async_copy(zbuf, xs_hbm.at[pl.ds(t * ztile, ztile)], zsem)

    def wait_slot(s):
        _bulk_wait(hbuf.at[s], sem.at[s])
        _bulk_wait(hbuf.at[s], sem.at[s])

    @pl.when(i == 0)
    def _():
        zbuf[...] = jnp.zeros(zbuf.shape, F32)
        for e in range(n_experts):
            def start(r, carry, e=e):
                pad_copy(e, r).start()
                return carry
            lax.fori_loop(0, pn_ref[e], start, 0)

        def start_tail(t, carry):
            tail_copy(t).start()
            return carry
        lax.fori_loop(nu_ref[0], n_sorted_tiles, start_tail, 0)

    def scatter_tile(s):
        @pl.when(i >= 2)
        def _():
            wait_slot(s)

        x = x_ref[...]
        h = x * lax.rsqrt(jnp.mean(x * x, axis=-1, keepdims=True) + RMS_EPS) * fg_ref[...]
        hbuf[s] = h.reshape(h.shape[0], 1, h.shape[1])

        def issue(g, carry):
            for k in range(DMA_UNROLL):
                r = g * DMA_UNROLL + k
                src = hbuf.at[s, pl.ds(r, 1)]
                pltpu.make_async_copy(src, xs_hbm.at[pl.ds(pos_ref[0, 0, r], 1)],
                                      sem.at[s]).start(priority=0)
                pltpu.make_async_copy(src, xs_hbm.at[pl.ds(pos_ref[0, 0, tile + r], 1)],
                                      sem.at[s]).start(priority=1)
            return carry

        lax.fori_loop(0, tile // DMA_UNROLL, issue, 0)

    for s in range(2):
        pl.when(slot == s)(functools.partial(scatter_tile, s))

    @pl.when(i == n_tiles - 1)
    def _():
        wait_slot((n_tiles - 1) % 2)
        if n_tiles >= 2:
            wait_slot(n_tiles % 2)
        for e in range(n_experts):
            def done(r, carry, e=e):
                pad_copy(e, r).wait()
                return carry
            lax.fori_loop(0, pn_ref[e], done, 0)

        def done_tail(t, carry):
            tail_copy(t).wait()
            return carry
        lax.fori_loop(nu_ref[0], n_sorted_tiles, done_tail, 0)


def _dispatch(x2d, ffn_g, pos_tiles, pad_start, pad_len, n_used, cap):
    T, D = x2d.shape
    tile = TOKEN_TILE
    nt = T // tile
    n_experts = pad_start.shape[0]
    grid_spec = pltpu.PrefetchScalarGridSpec(
        num_scalar_prefetch=3,
        grid=(nt,),
        in_specs=[pl.BlockSpec((1, 1, 2 * tile), lambda i, ps, pn, nu: (i, 0, 0), memory_space=pltpu.SMEM),
                  pl.BlockSpec((tile, D), lambda i, ps, pn, nu: (i, 0)),
                  pl.BlockSpec((1, D), lambda i, ps, pn, nu: (0, 0))],
        out_specs=pl.BlockSpec(memory_space=pl.ANY),
        scratch_shapes=[pltpu.VMEM((2, tile, 1, D), F32), pltpu.VMEM((EXPERT_TILE, 1, D), F32),
                        pltpu.SemaphoreType.DMA((2,)), pltpu.SemaphoreType.DMA(())],
    )
    return pl.pallas_call(
        functools.partial(_dispatch_body, tile=tile, n_tiles=nt, n_experts=n_experts,
                          n_sorted_tiles=cap // EXPERT_TILE),
        grid_spec=grid_spec,
        out_shape=jax.ShapeDtypeStruct((cap, 1, D), F32),
        compiler_params=_params(1, VMEM_LIMIT),
        name="moe_dispatch",
    )(pad_start, pad_len, n_used, pos_tiles, x2d, ffn_g.astype(F32).reshape(1, D))


def _expert_body(te_ref, nu_ref, nx_ref, sl_ref, x_ref, win_hbm, wout_hbm, y_ref, win_bf, wout_bf, x2_ref,
                 win_st, wout_st, wsem, *, layer):
    i = pl.program_id(0)
    ff = wout_bf.shape[0]
    used = i < nu_ref[0]
    new_expert = jnp.logical_or(i == 0, te_ref[i] != te_ref[jnp.maximum(i - 1, 0)])

    def weight_copies(e, slot):
        return (pltpu.make_async_copy(win_hbm.at[layer, e], win_st.at[slot], wsem.at[0, slot]),
                pltpu.make_async_copy(wout_hbm.at[layer, e], wout_st.at[slot], wsem.at[1, slot]))

    @pl.when(i == 0)
    def _():
        for c in weight_copies(te_ref[0], 0):
            c.start()

    def begin_run(s):
        for c in weight_copies(te_ref[i], s):
            c.wait()
        win_bf[...] = win_st[s].astype(BF16)
        wout_bf[...] = wout_st[s].astype(BF16)

        @pl.when(nx_ref[i] >= 0)
        def _():
            for c in weight_copies(nx_ref[i], 1 - s):
                c.start()

    for s in range(2):
        pl.when(jnp.logical_and(jnp.logical_and(used, new_expert), sl_ref[i] == s))(
            functools.partial(begin_run, s))

    @pl.when(used)
    def _():
        x2_ref[...] = x_ref[...].reshape(x2_ref.shape)
        gu = _dot(x2_ref[...].astype(BF16), win_bf[...])
        act = (_silu(gu[:, 0:ff]) * gu[:, ff:2 * ff]).astype(BF16)
        y = _dot(act, wout_bf[...])
        y_ref[...] = y.reshape(y_ref.shape)

    @pl.when(jnp.logical_not(used))
    def _():
        y_ref[...] = jnp.zeros(y_ref.shape, F32)


def _experts(x_sorted, w_exp_in, w_exp_out, tile_expert, n_used, next_expert, run_slot, layer):
    cap, _, D = x_sorted.shape
    tile = EXPERT_TILE
    nt = cap // tile
    ff = w_exp_out.shape[2]
    rows = lambda i, te, nu, nx, sl: (i, 0, 0)
    grid_spec = pltpu.PrefetchScalarGridSpec(
        num_scalar_prefetch=4,
        grid=(nt,),
        in_specs=[pl.BlockSpec((tile, 1, D), rows),
                  pl.BlockSpec(memory_space=pl.ANY), pl.BlockSpec(memory_space=pl.ANY)],
        out_specs=pl.BlockSpec((tile, 1, D), rows),
        scratch_shapes=[pltpu.VMEM((D, 2 * ff), BF16), pltpu.VMEM((ff, D), BF16),
                        pltpu.VMEM((tile, D), F32),
                        pltpu.VMEM((2, D, 2 * ff), F32), pltpu.VMEM((2, ff, D), F32),
                        pltpu.SemaphoreType.DMA((2, 2))],
    )
    return pl.pallas_call(
        functools.partial(_expert_body, layer=layer),
        grid_spec=grid_spec,
        out_shape=jax.ShapeDtypeStruct((cap, 1, D), F32),
        compiler_params=_params(1, VMEM_LIMIT),
        name="expert_ffn",
    )(tile_expert, n_used, next_expert, run_slot, x_sorted, w_exp_in, w_exp_out)


def _combine_body(pos0_ref, posn_ref, y_hbm, x_ref, route_ref, o_ref, ybuf, y2_ref, sem, *, tile, n_tiles):
    i = pl.program_id(0)
    slot = i % 2

    def gather(pos_ref, s):
        def issue(g, carry):
            for k in range(DMA_UNROLL):
                r = g * DMA_UNROLL + k
                pltpu.make_async_copy(y_hbm.at[pl.ds(pos_ref[0, 0, r], 1)],
                                      ybuf.at[s, pl.ds(r, 1)], sem.at[s]).start(priority=k % 2)
            return carry
        lax.fori_loop(0, 2 * tile // DMA_UNROLL, issue, 0)

    @pl.when(i == 0)
    def _():
        gather(pos0_ref, 0)

    def combine_tile(s):
        @pl.when(i + 1 < n_tiles)
        def _():
            gather(posn_ref, 1 - s)

        _bulk_wait(ybuf.at[s], sem.at[s])
        y2_ref[...] = ybuf[s].reshape(y2_ref.shape)
        route = route_ref[...]
        o_ref[...] = (x_ref[...] + route[:, 2:3] * y2_ref[0:tile, :]
                      + route[:, 3:4] * y2_ref[tile:2 * tile, :])

    for s in range(2):
        pl.when(slot == s)(functools.partial(combine_tile, s))


def _combine(x2d, route, y_sorted, pos_tiles):
    T, D = x2d.shape
    tile = TOKEN_TILE
    nt = T // tile
    return pl.pallas_call(
        functools.partial(_combine_body, tile=tile, n_tiles=nt),
        grid=(nt,),
        in_specs=[pl.BlockSpec((1, 1, 2 * tile), lambda i: (0, 0, 0), memory_space=pltpu.SMEM),
                  pl.BlockSpec((1, 1, 2 * tile), lambda i: (jnp.minimum(i + 1, nt - 1), 0, 0),
                               memory_space=pltpu.SMEM),
                  pl.BlockSpec(memory_space=pl.ANY),
                  pl.BlockSpec((tile, D), lambda i: (i, 0)),
                  pl.BlockSpec((tile, LANES), lambda i: (i, 0))],
        out_specs=pl.BlockSpec((tile, D), lambda i: (i, 0)),
        out_shape=jax.ShapeDtypeStruct((T, D), F32),
        scratch_shapes=[pltpu.VMEM((2, 2 * tile, 1, D), F32), pltpu.VMEM((2 * tile, D), F32),
                        pltpu.SemaphoreType.DMA((2,))],
        compiler_params=_params(1, VMEM_LIMIT),
        name="moe_combine",
    )(pos_tiles, pos_tiles, y_sorted, x2d, route)


def _positions_body(route_ref, offs_ref, pos_ref, *, tile):
    r = route_ref[...]
    lane = lax.broadcasted_iota(jnp.int32, r.shape, 1)
    lane_f = lane.astype(F32)
    offs = offs_ref[...]
    pos1 = jnp.sum(jnp.where(lane_f == r[:, 0:1], offs, 0.0), axis=-1, keepdims=True) + r[:, 4:5]
    pos2 = jnp.sum(jnp.where(lane_f == r[:, 1:2], offs, 0.0), axis=-1, keepdims=True) + r[:, 5:6]
    both = jnp.where(lane == 0, pos1, jnp.where(lane == 1, pos2, 0.0))
    for k in range(r.shape[0] // tile):
        t = both[k * tile:(k + 1) * tile, :].T
        pos_ref[k, :, 0:tile] = t[0:1, :].astype(jnp.int32)
        pos_ref[k, :, tile:2 * tile] = t[1:2, :].astype(jnp.int32)


def _positions(route, offs_row):
    T = route.shape[0]
    tile = TOKEN_TILE
    step = 4 * tile
    return pl.pallas_call(
        functools.partial(_positions_body, tile=tile),
        grid=(T // step,),
        in_specs=[pl.BlockSpec((step, LANES), lambda i: (i, 0)),
                  pl.BlockSpec((1, LANES), lambda i: (0, 0))],
        out_specs=pl.BlockSpec((step // tile, 1, 2 * tile), lambda i: (i, 0, 0)),
        out_shape=jax.ShapeDtypeStruct((T // tile, 1, 2 * tile), jnp.int32),
        compiler_params=_params(1, VMEM_LIMIT),
        name="moe_positions",
    )(route, offs_row)


def _dispatch_plan(route, counts, n_experts, tile, cap):
    cnt = counts[0, :n_experts].astype(jnp.int32)
    padded = ((cnt + tile - 1) // tile) * tile
    ends = jnp.cumsum(padded)
    offs = ends - padded
    nt = cap // tile
    n_used = (ends[-1] // tile).astype(jnp.int32)
    first_row = jnp.minimum(jnp.arange(nt, dtype=jnp.int32), n_used - 1) * tile
    te = jnp.sum((ends[None, :] <= first_row[:, None]).astype(jnp.int32), axis=1)
    te = jnp.minimum(te, n_experts - 1).astype(jnp.int32)
    run_start = jnp.concatenate([jnp.ones((1,), bool), te[1:] != te[:-1]])
    run_slot = ((jnp.cumsum(run_start.astype(jnp.int32)) - 1) % 2).astype(jnp.int32)
    ids = jnp.arange(n_experts, dtype=jnp.int32)
    later = (ids[None, :] > ids[:, None]) & (cnt > 0)[None, :]
    following = jnp.min(jnp.where(later, ids[None, :], n_experts), axis=1)
    following = jnp.where(following >= n_experts, -1, following).astype(jnp.int32)
    next_expert = following[te]
    offs_row = jnp.pad(offs.astype(F32), (0, LANES - n_experts)).reshape(1, LANES)
    pos_tiles = _positions(route, offs_row)
    return (te, n_used.reshape(1), pos_tiles, (offs + cnt).astype(jnp.int32), (padded - cnt).astype(jnp.int32),
            next_expert, run_slot)


def kernel(x, mix_norm_g, w_in, b_gate, q_norm_g, k_norm_g, lb_logits, hgrn_norm_g, conv_w, conv_b,
           conv_ln_g, conv_ln_b, w_att_o, w_hgrn_o, w_conv_o, w_out, ffn_norm_g, w_coarse, b_coarse,
           w_fine, b_fine, w_exp_in, w_exp_out):
    B, S, D = x.shape
    L = w_in.shape[0]
    T = B * S
    n_experts = w_exp_in.shape[1]
    cap = 2 * T + n_experts * EXPERT_TILE

    w_in_bf = w_in.astype(BF16)
    wa_bf, wh_bf, wc_bf, wo_bf = (w.astype(BF16) for w in (w_att_o, w_hgrn_o, w_conv_o, w_out))
    pad = LANES - N_GROUPS - n_experts
    w_route = jnp.concatenate([w_coarse, w_fine, jnp.zeros((L, D, pad), F32)], axis=-1).astype(F32)
    b_route = jnp.concatenate([b_coarse, b_fine, jnp.zeros((L, pad), F32)], axis=-1).astype(F32)

    x2d = x.reshape(T, D)
    for l in range(L):
        qkv, hf, hqig, cab, gl = _inproj(x2d, mix_norm_g[l], w_in_bf, l)
        att = _attention(qkv, q_norm_g[l], k_norm_g[l], B, S)
        rec = _hgrn(hqig, hf, lb_logits, hgrn_norm_g[l], l, B, S)
        cv = _conv(cab, conv_w[l], conv_b[l], conv_ln_g[l], conv_ln_b[l], B, S)
        x_mid, route, counts = _merge(att, rec, cv, gl, b_gate[l], x2d, wa_bf, wh_bf, wc_bf, wo_bf,
                                      ffn_norm_g[l], w_route[l], b_route[l].reshape(1, LANES), l)
        te, n_used, pos_tiles, pad_start, pad_len, next_expert, run_slot = _dispatch_plan(
            route, counts, n_experts, EXPERT_TILE, cap)
        x_sorted = _dispatch(x_mid, ffn_norm_g[l], pos_tiles, pad_start, pad_len, n_used, cap)
        y_sorted = _experts(x_sorted, w_exp_in, w_exp_out, te, n_used, next_expert, run_slot, l)
        x2d = _combine(x_mid, route, y_sorted, pos_tiles)
    return x2d.reshape(B, S, D)
```

```python
import functools

import jax
import jax.numpy as jnp
import numpy as np
from jax import lax
from jax.experimental import pallas as pl
from jax.experimental.pallas import tpu as pltpu

F32 = jnp.float32
BF16 = jnp.bfloat16

LANES = 128
SUBLANES = 8
RMS_EPS = 1e-6
LN_EPS = 1e-5
NEG_INF = -1e30
LOG2_E = 1.4426950408889634

ATT_HEADS = 8
ATT_HEAD_DIM = 64
MOBA_BLOCK = 256
MOBA_TOPK = 3
HGRN_HEADS = 4
HGRN_DIM = 128
HGRN_ROWS = 256
CONV_WIDTH = 31
CONV_ROWS = 512
CONV_HALO = 32
N_GROUPS = 4
EXPERTS_PER_GROUP = 8
EXPERT_TILE = 256
TOKEN_TILE = 256
INPROJ_TILE = 512
MERGE_TILE = 512
DMA_UNROLL = 8
VMEM_LIMIT = 56 * 1024 * 1024


def _params(n_axes, vmem=None):
    return pltpu.CompilerParams(dimension_semantics=("arbitrary",) * n_axes,
                                vmem_limit_bytes=vmem)


def _dot(a, b):
    return jnp.dot(a, b, preferred_element_type=F32)


def _dot_nt(a, b):
    return lax.dot_general(a, b, (((1,), (1,)), ((), ())), preferred_element_type=F32)


def _dot_tn(a, b):
    return lax.dot_general(a, b, (((0,), (0,)), ((), ())), preferred_element_type=F32)


def _split3(x):
    x1 = x.astype(BF16)
    r1 = x - x1.astype(F32)
    x2 = r1.astype(BF16)
    x3 = (r1 - x2.astype(F32)).astype(BF16)
    return x1, x2, x3


def _sigmoid(x):
    return 1.0 / (1.0 + jnp.exp(-x))


def _silu(x):
    return x * _sigmoid(x)


def _inproj_body(x_ref, g_ref, w_ref, qkv_ref, hf_ref, hqig_ref, cv_ref, gl_ref, *, segs, chunk):
    x = x_ref[...]
    ms = jnp.mean(x * x, axis=-1, keepdims=True)
    h = (x * lax.rsqrt(ms + RMS_EPS) * g_ref[...]).astype(BF16)
    refs = (qkv_ref, hf_ref, hqig_ref, cv_ref, gl_ref)
    for ridx, dst0, src0, width in segs:
        ref = refs[ridx]
        for c in range(0, width, chunk):
            y = _dot(h, w_ref[:, src0 + c:src0 + c + chunk])
            ref[:, dst0 + c:dst0 + c + chunk] = y.astype(ref.dtype)


def _inproj(x2d, norm_g, w_in_bf, layer):
    T, D = x2d.shape
    A = ATT_HEADS * ATT_HEAD_DIM
    K = HGRN_HEADS * HGRN_DIM
    C = A
    segs = ((0, 0, 0, 3 * A),
            (2, 0, 3 * A, K),
            (1, 0, 3 * A + K, K),
            (2, K, 3 * A + 2 * K, K),
            (2, 2 * K, 3 * A + 3 * K, K),
            (3, 0, 3 * A + 4 * K, 2 * C),
            (4, 0, 3 * A + 4 * K + 2 * C, 3 * D))
    ncols = w_in_bf.shape[-1]
    tm = INPROJ_TILE
    row = lambda i: (i, 0)
    return pl.pallas_call(
        functools.partial(_inproj_body, segs=segs, chunk=512),
        grid=(T // tm,),
        in_specs=[pl.BlockSpec((tm, D), row),
                  pl.BlockSpec((1, D), lambda i: (0, 0)),
                  pl.BlockSpec((None, D, ncols), lambda i: (layer, 0, 0))],
        out_specs=[pl.BlockSpec((tm, 3 * A), row), pl.BlockSpec((tm, K), row),
                   pl.BlockSpec((tm, 3 * K), row), pl.BlockSpec((tm, 2 * C), row),
                   pl.BlockSpec((tm, 3 * D), row)],
        out_shape=[jax.ShapeDtypeStruct((T, 3 * A), BF16), jax.ShapeDtypeStruct((T, K), F32),
                   jax.ShapeDtypeStruct((T, 3 * K), BF16), jax.ShapeDtypeStruct((T, 2 * C), BF16),
                   jax.ShapeDtypeStruct((T, 3 * D), BF16)],
        compiler_params=_params(1, VMEM_LIMIT),
        name="inproj",
    )(x2d, norm_g.reshape(1, D), w_in_bf)


def _attn_body(q_ref, k_ref, v_ref, gq_ref, gk_ref, o_ref, kb_ref, vt_ref, s_ref, *, seq, blk, topk):
    nb = seq // blk
    dh = ATT_HEAD_DIM
    dh_sh = dh.bit_length() - 1
    scale = dh ** -0.5 * LOG2_E
    lane = lax.broadcasted_iota(jnp.int32, (1, LANES), 1)
    hr = lax.broadcasted_iota(jnp.int32, (LANES, LANES), 0) >> dh_sh
    hc = lax.broadcasted_iota(jnp.int32, (LANES, LANES), 1) >> dh_sh
    same_head = jnp.where(hr == hc, 1.0, 0.0).astype(BF16)

    def head_norm(x, g):
        x2 = x * x
        hi = x2.astype(BF16)
        lo = (x2 - hi.astype(F32)).astype(BF16)
        ssq = _dot(hi, same_head) + _dot(lo, same_head)
        return x * lax.rsqrt(ssq * (1.0 / dh) + RMS_EPS) * g

    kmeans = []
    for j in range(nb):
        kn = head_norm(k_ref[j * blk:(j + 1) * blk, :].astype(F32), gk_ref[...])
        kb_ref[j * blk:(j + 1) * blk, :] = kn.astype(BF16)
        kmeans.append(jnp.mean(kn, axis=0, keepdims=True))
        vt_ref[:, j * blk:(j + 1) * blk] = v_ref[j * blk:(j + 1) * blk, :].astype(F32).T.astype(BF16)
    kmean = jnp.concatenate(kmeans, axis=0)
    km_hi = kmean.astype(BF16)
    km_lo = (kmean - km_hi.astype(F32)).astype(BF16)
    first_head = lane < dh
    first_head_t = lax.broadcasted_iota(jnp.int32, (LANES, 1), 0) < dh
    blk_n = lax.broadcasted_iota(jnp.int32, (nb, 1), 0)

    key_l = lax.broadcasted_iota(jnp.int32, (blk, 2 * blk), 0)
    qry_l = lax.broadcasted_iota(jnp.int32, (blk, 2 * blk), 1) & (blk - 1)
    causal = key_l <= qry_l

    for i in range(nb):
        rows = head_norm(q_ref[i * blk:(i + 1) * blk, :].astype(F32), gq_ref[...])
        qi = jnp.concatenate([jnp.where(first_head, rows, 0.0), jnp.where(first_head, 0.0, rows)], axis=0)
        qs = (qi * scale).astype(BF16)

        sel = None
        if i > topk:
            q_hi = qi.astype(BF16)
            q_lo = (qi - q_hi.astype(F32)).astype(BF16)
            gate = _dot_nt(km_hi, q_hi) + _dot_nt(km_lo, q_hi) + _dot_nt(km_hi, q_lo)
            valid = blk_n < i
            gm = jnp.where(valid, gate, NEG_INF)
            rank = jnp.zeros(gm.shape, F32)
            for r in range(1, nb):
                gr = pltpu.roll(gm, r, 0)
                ahead = (gr > gm) | ((gr == gm) & (blk_n >= r))
                rank = rank + jnp.where(ahead, 1.0, 0.0)
            sel = jnp.where(valid & (rank < topk), 1.0, 0.0)

        m = None
        for j in range(i + 1):
            s = _dot_nt(kb_ref[j * blk:(j + 1) * blk, :], qs)
            if j == i:
                s = jnp.where(causal, s, NEG_INF)
            elif sel is not None:
                s = jnp.where(sel[j:j + 1, :] > 0.5, s, NEG_INF)
            s_ref[j * blk:(j + 1) * blk, :] = s
            mj = jnp.max(s, axis=0, keepdims=True)
            m = mj if m is None else jnp.maximum(m, mj)

        l = None
        acc = None
        for j in range(i + 1):
            p = jnp.exp2(s_ref[j * blk:(j + 1) * blk, :] - m)
            lj = jnp.sum(p, axis=0, keepdims=True)
            aj = _dot(vt_ref[:, j * blk:(j + 1) * blk], p.astype(BF16))
            l = lj if l is None else l + lj
            acc = aj if acc is None else acc + aj

        ot = acc / l
        merged = jnp.where(first_head_t, ot[:, 0:blk], ot[:, blk:2 * blk])
        o_ref[i * blk:(i + 1) * blk, :] = merged.T.astype(o_ref.dtype)


def _attention(qkv, q_norm_g, k_norm_g, batch, seq):
    T = qkv.shape[0]
    A = ATT_HEADS * ATT_HEAD_DIM
    npair = A // LANES
    reps = LANES // ATT_HEAD_DIM
    gq = jnp.tile(q_norm_g.astype(F32), reps).reshape(1, LANES)
    gk = jnp.tile(k_norm_g.astype(F32), reps).reshape(1, LANES)
    blk_spec = lambda off: pl.BlockSpec((seq, LANES), lambda b, p: (b, off + p))
    vec = pl.BlockSpec((1, LANES), lambda b, p: (0, 0))
    return pl.pallas_call(
        functools.partial(_attn_body, seq=seq, blk=MOBA_BLOCK, topk=MOBA_TOPK),
        grid=(batch, npair),
        in_specs=[blk_spec(0), blk_spec(npair), blk_spec(2 * npair), vec, vec],
        out_specs=pl.BlockSpec((seq, LANES), lambda b, p: (b, p)),
        out_shape=jax.ShapeDtypeStruct((T, A), BF16),
        scratch_shapes=[pltpu.VMEM((seq, LANES), BF16), pltpu.VMEM((LANES, seq), BF16),
                        pltpu.VMEM((seq, 2 * MOBA_BLOCK), F32)],
        compiler_params=_params(2, VMEM_LIMIT),
        name="moba_attn",
    )(qkv, qkv, qkv, gq, gk)


def _hgrn_head(q_ref, f_ref, i_ref, g_ref, lbl_ref, ng_ref, tri_ref, lvl_ref, o_ref, st_ref, *,
               rows, layer, head):
    R = rows
    cols = slice(head * HGRN_DIM, (head + 1) * HGRN_DIM)

    lbl = lbl_ref[:, cols]
    e = jnp.exp(lbl - jnp.max(lbl, axis=0, keepdims=True))
    p = e / jnp.sum(e, axis=0, keepdims=True)
    lb = jnp.maximum(jnp.sum(p[0:layer + 1], axis=0, keepdims=True) - p[0:1], 0.0)

    sig = _sigmoid(f_ref[:, cols])
    g = jnp.log(lb + (1.0 - lb) * sig)
    kin = (1.0 - lb) * (1.0 - sig)
    qa = _silu(q_ref[:, cols].astype(F32))
    vb = i_ref[:, cols]

    t_col = lax.broadcasted_iota(jnp.int32, (R, 1), 0)

    tri = tri_ref[...]
    g1, g2, g3 = _split3(g)
    b = (_dot(tri, g1) + _dot(tri, g2) + _dot(tri, g3)) * LOG2_E

    half = R // 2
    scores = [None, None]
    cross = None
    m = half
    while m >= 1:
        w = 2 * m
        if w >= SUBLANES:
            b3 = b.reshape(R // w, w, LANES)
            bref = jnp.broadcast_to(b3[:, m - 1:m, :], b3.shape).reshape(R, LANES)
        else:
            tl = t_col & (w - 1)
            bref = None
            for resid in range(w):
                shift = resid - (m - 1)
                cand = b if shift == 0 else pltpu.roll(b, shift % R, 0)
                bref = cand if bref is None else jnp.where(tl == resid, cand, bref)
        e = jnp.exp2(-jnp.abs(b - bref))
        qt = (qa * e).astype(BF16)
        kt = (kin * e).astype(BF16)
        if w == R:
            cross = _dot_nt(qt[half:R], kt[0:half])
        else:
            keep = lvl_ref[...] == (m.bit_length() - 1)
            for hh in range(2):
                part = _dot_nt(qt[hh * half:(hh + 1) * half], kt[hh * half:(hh + 1) * half])
                scores[hh] = jnp.where(keep, part, 0.0 if scores[hh] is None else scores[hh])
        m //= 2

    st = st_ref[head]
    o = _dot_nt((qa * jnp.exp2(b)).astype(BF16), st.astype(BF16))
    o = o + jnp.sum(qa * kin, axis=-1, keepdims=True) * vb.astype(F32)
    o_lo = o[0:half] + _dot(scores[0].astype(BF16), vb[0:half])
    o_hi = o[half:R] + _dot(scores[1].astype(BF16), vb[half:R]) + _dot(cross.astype(BF16), vb[0:half])
    o = jnp.concatenate([o_lo, o_hi], axis=0)

    b_end = b[R - 1:R, :]
    k_end = (kin * jnp.exp2(b_end - b)).astype(BF16)
    st_ref[head] = st * jnp.exp2(b_end) + _dot_tn(vb, k_end)

    on = o * lax.rsqrt(jnp.mean(o * o, axis=-1, keepdims=True) + RMS_EPS) * ng_ref[...]
    o_ref[:, cols] = (on * _silu(g_ref[:, cols].astype(F32))).astype(o_ref.dtype)


def _hgrn_body(q_ref, f_ref, i_ref, g_ref, lbl_ref, ng_ref, tri_ref, lvl_ref, o_ref, st_ref, *, rows, layer):
    @pl.when(pl.program_id(1) == 0)
    def _():
        st_ref[...] = jnp.zeros(st_ref.shape, F32)

    for head in range(st_ref.shape[0]):
        _hgrn_head(q_ref, f_ref, i_ref, g_ref, lbl_ref, ng_ref, tri_ref, lvl_ref, o_ref, st_ref,
                   rows=rows, layer=layer, head=head)


def _level_table(n):
    t = np.arange(n, dtype=np.int32)[:, None]
    s = np.arange(n, dtype=np.int32)[None, :]
    x = t ^ s
    hb = np.zeros((n, n), np.int32)
    for k in range(1, n.bit_length()):
        hb = hb + (x >= (1 << k)).astype(np.int32)
    return jnp.asarray(np.where(t > s, hb, -1).astype(np.int32))


def _hgrn(hqig, hf, lb_logits, norm_g, layer, batch, seq):
    T = hf.shape[0]
    H = HGRN_HEADS
    R = HGRN_ROWS
    W = H * HGRN_DIM
    nc = seq // R
    L = lb_logits.shape[0]
    blk = lambda off: pl.BlockSpec((R, W), lambda b, c: (b * nc + c, off))
    const = lambda b, c: (0, 0)
    tri = jnp.asarray(np.tril(np.ones((R, R), np.float32)), dtype=BF16)
    return pl.pallas_call(
        functools.partial(_hgrn_body, rows=R, layer=layer),
        grid=(batch, nc),
        in_specs=[blk(0), blk(0), blk(1), blk(2),
                  pl.BlockSpec((L, W), const), pl.BlockSpec((1, LANES), const),
                  pl.BlockSpec((R, R), const), pl.BlockSpec((R // 2, R // 2), const)],
        out_specs=blk(0),
        out_shape=jax.ShapeDtypeStruct((T, W), BF16),
        scratch_shapes=[pltpu.VMEM((H, HGRN_DIM, HGRN_DIM), F32)],
        compiler_params=_params(2, VMEM_LIMIT),
        name="hgrn2",
    )(hqig, hf, hqig, hqig, lb_logits.astype(F32), norm_g.astype(F32).reshape(1, LANES), tri,
      _level_table(R // 2))


def _conv_body(ab_ref, w_ref, bdw_ref, lg_ref, lb_ref, o_ref, ubuf, cbuf, ush, *, rows, width, chans):
    R, W, C, H = rows, width, chans, CONV_HALO
    c = pl.program_id(1)

    @pl.when(c == 0)
    def _():
        ubuf[0:H, :] = jnp.zeros((H, C), F32)

    a = ab_ref[:, 0:C].astype(F32)
    gate = ab_ref[:, C:2 * C].astype(F32)
    ubuf[H:H + R, :] = a * _sigmoid(gate)

    n_sh = ush.shape[1]
    for s in range(1, SUBLANES):
        ush[s - 1] = ubuf[s:s + n_sh, :]

    rsub = 64
    for lb in range(0, C, LANES):
        for rb in range(0, R, rsub):
            acc = jnp.zeros((rsub, LANES), F32)
            for j in range(W):
                start = H - (W - 1) + j + rb
                s = start % SUBLANES
                a0 = start - s
                if s == 0:
                    u = ubuf[a0:a0 + rsub, lb:lb + LANES]
                else:
                    u = ush[s - 1, a0:a0 + rsub, lb:lb + LANES]
                acc = acc + w_ref[j:j + 1, lb:lb + LANES] * u
            cbuf[rb:rb + rsub, lb:lb + LANES] = acc

    uf = cbuf[...] + bdw_ref[...]
    mu = jnp.mean(uf, axis=-1, keepdims=True)
    d = uf - mu
    var = jnp.mean(d * d, axis=-1, keepdims=True)
    y = d * lax.rsqrt(var + LN_EPS) * lg_ref[...] + lb_ref[...]
    o_ref[...] = _silu(y).astype(o_ref.dtype)
    ubuf[0:H, :] = ubuf[R:R + H, :]


def _conv(cab, conv_w, conv_b, ln_g, ln_b, batch, seq):
    T = cab.shape[0]
    C = cab.shape[1] // 2
    W = conv_w.shape[0]
    R = CONV_ROWS
    nc = seq // R
    vec = pl.BlockSpec((1, C), lambda b, c: (0, 0))
    return pl.pallas_call(
        functools.partial(_conv_body, rows=R, width=W, chans=C),
        grid=(batch, nc),
        in_specs=[pl.BlockSpec((R, 2 * C), lambda b, c: (b * nc + c, 0)),
                  pl.BlockSpec((W, C), lambda b, c: (0, 0)), vec, vec, vec],
        out_specs=pl.BlockSpec((R, C), lambda b, c: (b * nc + c, 0)),
        out_shape=jax.ShapeDtypeStruct((T, C), BF16),
        scratch_shapes=[pltpu.VMEM((R + CONV_HALO, C), F32), pltpu.VMEM((R, C), F32),
                        pltpu.VMEM((SUBLANES - 1, R + CONV_HALO - SUBLANES, C), F32)],
        compiler_params=_params(2, VMEM_LIMIT),
        name="conformer_conv",
    )(cab, conv_w.astype(F32), conv_b.astype(F32).reshape(1, C), ln_g.astype(F32).reshape(1, C),
      ln_b.astype(F32).reshape(1, C))


def _merge_body(att_ref, rec_ref, cv_ref, gl_ref, bg_ref, x_ref, wa_ref, wh_ref, wc_ref, wo_ref,
                fg_ref, wr_ref, br_ref, before_ref, xo_ref, route_ref, cnt_out_ref, mg_ref, cnt_ref, *,
                n_groups, per_group):
    D = x_ref.shape[1]
    att, rec, cv = att_ref[...], rec_ref[...], cv_ref[...]
    cw = 256
    for c in range(0, D, cw):
        def gate(k):
            return _sigmoid(gl_ref[:, k * D + c:k * D + c + cw].astype(F32) + bg_ref[:, k * D + c:k * D + c + cw])
        merged = (gate(0) * _dot(att, wa_ref[:, c:c + cw]) + gate(1) * _dot(rec, wh_ref[:, c:c + cw])
                  + gate(2) * _dot(cv, wc_ref[:, c:c + cw]))
        mg_ref[:, c:c + cw] = merged.astype(BF16)
    xn = x_ref[...] + _dot(mg_ref[...], wo_ref[...])
    xo_ref[...] = xn

    h = xn * lax.rsqrt(jnp.mean(xn * xn, axis=-1, keepdims=True) + RMS_EPS) * fg_ref[...]
    h_hi = h.astype(BF16)
    h_lo = (h - h_hi.astype(F32)).astype(BF16)
    wr = wr_ref[...]
    w_hi = wr.astype(BF16)
    w_lo = (wr - w_hi.astype(F32)).astype(BF16)
    logits = _dot(h_hi, w_hi) + _dot(h_hi, w_lo) + _dot(h_lo, w_hi) + br_ref[...]

    G, E = n_groups, per_group
    lane = lax.broadcasted_iota(jnp.int32, logits.shape, 1)
    big = jnp.int32(LANES)
    is_c = lane < G
    cl = jnp.where(is_c, logits, NEG_INF)
    cm = jnp.max(cl, axis=-1, keepdims=True)
    grp = jnp.min(jnp.where(cl == cm, lane, big), axis=-1, keepdims=True)
    se = jnp.sum(jnp.exp(jnp.where(is_c, logits - cm, NEG_INF)), axis=-1, keepdims=True)
    p_top = 1.0 / se
    lo = G + grp * E
    fl = jnp.where((lane >= lo) & (lane < lo + E), logits, NEG_INF)
    m1 = jnp.max(fl, axis=-1, keepdims=True)
    i1 = jnp.min(jnp.where(fl == m1, lane, big), axis=-1, keepdims=True)
    fl2 = jnp.where(lane == i1, NEG_INF, fl)
    m2 = jnp.max(fl2, axis=-1, keepdims=True)
    i2 = jnp.min(jnp.where(fl2 == m2, lane, big), axis=-1, keepdims=True)
    t = jnp.exp(m2 - m1)
    w1 = p_top / (1.0 + t)
    w2 = p_top * t / (1.0 + t)
    e1 = (i1 - G).astype(F32)
    e2 = (i2 - G).astype(F32)

    @pl.when(pl.program_id(0) == 0)
    def _():
        cnt_ref[...] = jnp.zeros(cnt_ref.shape, F32)

    hit1 = lane == (i1 - G)
    hit2 = lane == (i2 - G)
    hits = jnp.where(hit1 | hit2, 1.0, 0.0)
    prefix = _dot(before_ref[...], hits.astype(BF16)) + cnt_ref[...]
    rank1 = jnp.sum(jnp.where(hit1, prefix, 0.0), axis=-1, keepdims=True)
    rank2 = jnp.sum(jnp.where(hit2, prefix, 0.0), axis=-1, keepdims=True)
    cnt = cnt_ref[...] + jnp.sum(hits, axis=0, keepdims=True)
    cnt_ref[...] = cnt
    cnt_out_ref[...] = cnt

    route_ref[...] = jnp.where(lane == 0, e1, jnp.where(lane == 1, e2,
                               jnp.where(lane == 2, w1, jnp.where(lane == 3, w2,
                               jnp.where(lane == 4, rank1, jnp.where(lane == 5, rank2, 0.0))))))


def _merge(att, rec, cv, gl, b_gate, x2d, wa, wh, wc, wo, ffn_g, w_route, b_route, layer):
    T, D = x2d.shape
    tm = MERGE_TILE
    row = lambda i: (i, 0)
    const2 = lambda i: (0, 0)
    wsel = lambda i: (layer, 0, 0)
    W = att.shape[1]
    before = jnp.asarray(np.tril(np.ones((tm, tm), np.float32), -1), dtype=BF16)
    return pl.pallas_call(
        functools.partial(_merge_body, n_groups=N_GROUPS, per_group=EXPERTS_PER_GROUP),
        grid=(T // tm,),
        in_specs=[pl.BlockSpec((tm, W), row), pl.BlockSpec((tm, W), row), pl.BlockSpec((tm, W), row),
                  pl.BlockSpec((tm, 3 * D), row), pl.BlockSpec((1, 3 * D), const2),
                  pl.BlockSpec((tm, D), row),
                  pl.BlockSpec((None, W, D), wsel), pl.BlockSpec((None, W, D), wsel),
                  pl.BlockSpec((None, W, D), wsel), pl.BlockSpec((None, D, D), wsel),
                  pl.BlockSpec((1, D), const2), pl.BlockSpec((D, LANES), const2),
                  pl.BlockSpec((1, LANES), const2), pl.BlockSpec((tm, tm), const2)],
        out_specs=[pl.BlockSpec((tm, D), row), pl.BlockSpec((tm, LANES), row),
                   pl.BlockSpec((1, LANES), const2)],
        out_shape=[jax.ShapeDtypeStruct((T, D), F32), jax.ShapeDtypeStruct((T, LANES), F32),
                   jax.ShapeDtypeStruct((1, LANES), F32)],
        scratch_shapes=[pltpu.VMEM((tm, D), BF16), pltpu.VMEM((1, LANES), F32)],
        compiler_params=_params(1, VMEM_LIMIT),
        name="merge_route",
    )(att, rec, cv, gl, b_gate.astype(F32).reshape(1, 3 * D), x2d, wa, wh, wc, wo,
      ffn_g.astype(F32).reshape(1, D), w_route, b_route, before)


def _bulk_wait(view, sem):
    pltpu.make_async_copy(view, view, sem).wait()


def _dispatch_body(ps_ref, pn_ref, nu_ref, pos_ref, x_ref, fg_ref, xs_hbm, hbuf, zbuf, sem, zsem, *,
                   tile, n_tiles, n_experts, n_sorted_tiles):
    i = pl.program_id(0)
    slot = i % 2
    ztile = zbuf.shape[0]

    def pad_copies(e):
        out = []
        for k in range(ztile.bit_length() - 1):
            first = ps_ref[e] + ((pn_ref[e] >> (k + 1)) << (k + 1))
            copy = pltpu.make_async_copy(zbuf.at[pl.ds(0, 1 << k)], xs_hbm.at[pl.ds(first, 1 << k)], zsem)
            out.append((((pn_ref[e] >> k) & 1) == 1, copy))
        return out

    def tail_copy(t):
        return pltpu.make_async_copy(zbuf, xs_hbm.at[pl.ds(t * ztile, ztile)], zsem)

    def wait_slot(s):
        _bulk_wait(hbuf.at[s], sem.at[s])
        _bulk_wait(hbuf.at[s], sem.at[s])

    @pl.when(i == 0)
    def _():
        zbuf[...] = jnp.zeros(zbuf.shape, F32)
        for e in range(n_experts):
            for bit_set, copy in pad_copies(e):
                pl.when(bit_set)(copy.start)

        def start_tail(t, carry):
            tail_copy(t).start()
            return carry
        lax.fori_loop(nu_ref[0], n_sorted_tiles, start_tail, 0)

    def scatter_tile(s):
        @pl.when(i >= 2)
        def _():
            wait_slot(s)

        x = x_ref[...]
        h = x * lax.rsqrt(jnp.mean(x * x, axis=-1, keepdims=True) + RMS_EPS) * fg_ref[...]
        hbuf[s] = h.reshape(h.shape[0], 1, h.shape[1])

        def issue(g, carry):
            for k in range(DMA_UNROLL):
                r = g * DMA_UNROLL + k
                src = hbuf.at[s, pl.ds(r, 1)]
                pltpu.make_async_copy(src, xs_hbm.at[pl.ds(pos_ref[0, 0, r], 1)],
                                      sem.at[s]).start(priority=0)
                pltpu.make_async_copy(src, xs_hbm.at[pl.ds(pos_ref[0, 0, tile + r], 1)],
                                      sem.at[s]).start(priority=1)
            return carry

        lax.fori_loop(0, tile // DMA_UNROLL, issue, 0)

    for s in range(2):
        pl.when(slot == s)(functools.partial(scatter_tile, s))

    @pl.when(i == n_tiles - 1)
    def _():
        wait_slot((n_tiles - 1) % 2)
        if n_tiles >= 2:
            wait_slot(n_tiles % 2)
        for e in range(n_experts):
            for bit_set, copy in pad_copies(e):
                pl.when(bit_set)(copy.wait)

        def done_tail(t, carry):
            tail_copy(t).wait()
            return carry
        lax.fori_loop(nu_ref[0], n_sorted_tiles, done_tail, 0)


def _dispatch(x2d, ffn_g, pos_tiles, pad_start, pad_len, n_used, cap):
    T, D = x2d.shape
    tile = TOKEN_TILE
    nt = T // tile
    n_experts = pad_start.shape[0]
    grid_spec = pltpu.PrefetchScalarGridSpec(
        num_scalar_prefetch=3,
        grid=(nt,),
        in_specs=[pl.BlockSpec((1, 1, 2 * tile), lambda i, ps, pn, nu: (i, 0, 0), memory_space=pltpu.SMEM),
                  pl.BlockSpec((tile, D), lambda i, ps, pn, nu: (i, 0)),
                  pl.BlockSpec((1, D), lambda i, ps, pn, nu: (0, 0))],
        out_specs=pl.BlockSpec(memory_space=pl.ANY),
        scratch_shapes=[pltpu.VMEM((2, tile, 1, D), F32), pltpu.VMEM((EXPERT_TILE, 1, D), F32),
                        pltpu.SemaphoreType.DMA((2,)), pltpu.SemaphoreType.DMA(())],
    )
    return pl.pallas_call(
        functools.partial(_dispatch_body, tile=tile, n_tiles=nt, n_experts=n_experts,
                          n_sorted_tiles=cap // EXPERT_TILE),
        grid_spec=grid_spec,
        out_shape=jax.ShapeDtypeStruct((cap, 1, D), F32),
        compiler_params=_params(1, VMEM_LIMIT),
        name="moe_dispatch",
    )(pad_start, pad_len, n_used, pos_tiles, x2d, ffn_g.astype(F32).reshape(1, D))


def _expert_body(te_ref, nu_ref, nx_ref, sl_ref, x_ref, win_hbm, wout_hbm, y_ref, win_bf, wout_bf, x2_ref,
                 win_st, wout_st, wsem, *, layer):
    i = pl.program_id(0)
    ff = wout_bf.shape[0]
    used = i < nu_ref[0]
    new_expert = jnp.logical_or(i == 0, te_ref[i] != te_ref[jnp.maximum(i - 1, 0)])

    def weight_copies(e, slot):
        return (pltpu.make_async_copy(win_hbm.at[layer, e], win_st.at[slot], wsem.at[0, slot]),
                pltpu.make_async_copy(wout_hbm.at[layer, e], wout_st.at[slot], wsem.at[1, slot]))

    @pl.when(i == 0)
    def _():
        for c in weight_copies(te_ref[0], 0):
            c.start()

    def begin_run(s):
        for c in weight_copies(te_ref[i], s):
            c.wait()
        win_bf[...] = win_st[s].astype(BF16)
        wout_bf[...] = wout_st[s].astype(BF16)

        @pl.when(nx_ref[i] >= 0)
        def _():
            for c in weight_copies(nx_ref[i], 1 - s):
                c.start()

    for s in range(2):
        pl.when(jnp.logical_and(jnp.logical_and(used, new_expert), sl_ref[i] == s))(
            functools.partial(begin_run, s))

    @pl.when(used)
    def _():
        x2_ref[...] = x_ref[...].reshape(x2_ref.shape)
        gu = _dot(x2_ref[...].astype(BF16), win_bf[...])
        act = (_silu(gu[:, 0:ff]) * gu[:, ff:2 * ff]).astype(BF16)
        y = _dot(act, wout_bf[...])
        y_ref[...] = y.reshape(y_ref.shape)

    @pl.when(jnp.logical_not(used))
    def _():
        y_ref[...] = jnp.zeros(y_ref.shape, F32)


def _experts(x_sorted, w_exp_in, w_exp_out, tile_expert, n_used, next_expert, run_slot, layer):
    cap, _, D = x_sorted.shape
    tile = EXPERT_TILE
    nt = cap // tile
    ff = w_exp_out.shape[2]
    rows = lambda i, te, nu, nx, sl: (i, 0, 0)
    grid_spec = pltpu.PrefetchScalarGridSpec(
        num_scalar_prefetch=4,
        grid=(nt,),
        in_specs=[pl.BlockSpec((tile, 1, D), rows),
                  pl.BlockSpec(memory_space=pl.ANY), pl.BlockSpec(memory_space=pl.ANY)],
        out_specs=pl.BlockSpec((tile, 1, D), rows),
        scratch_shapes=[pltpu.VMEM((D, 2 * ff), BF16), pltpu.VMEM((ff, D), BF16),
                        pltpu.VMEM((tile, D), F32),
                        pltpu.VMEM((2, D, 2 * ff), F32), pltpu.VMEM((2, ff, D), F32),
                        pltpu.SemaphoreType.DMA((2, 2))],
    )
    return pl.pallas_call(
        functools.partial(_expert_body, layer=layer),
        grid_spec=grid_spec,
        out_shape=jax.ShapeDtypeStruct((cap, 1, D), F32),
        compiler_params=_params(1, VMEM_LIMIT),
        name="expert_ffn",
    )(tile_expert, n_used, next_expert, run_slot, x_sorted, w_exp_in, w_exp_out)


def _combine_body(pos0_ref, posn_ref, y_hbm, x_ref, route_ref, o_ref, ybuf, y2_ref, sem, *, tile, n_tiles):
    i = pl.program_id(0)
    slot = i % 2

    def gather(pos_ref, s):
        def issue(g, carry):
            for k in range(DMA_UNROLL):
                r = g * DMA_UNROLL + k
                pltpu.make_async_copy(y_hbm.at[pl.ds(pos_ref[0, 0, r], 1)],
                                      ybuf.at[s, pl.ds(r, 1)], sem.at[s]).start(priority=k % 2)
            return carry
        lax.fori_loop(0, 2 * tile // DMA_UNROLL, issue, 0)

    @pl.when(i == 0)
    def _():
        gather(pos0_ref, 0)

    def combine_tile(s):
        @pl.when(i + 1 < n_tiles)
        def _():
            gather(posn_ref, 1 - s)

        _bulk_wait(ybuf.at[s], sem.at[s])
        y2_ref[...] = ybuf[s].reshape(y2_ref.shape)
        route = route_ref[...]
        o_ref[...] = (x_ref[...] + route[:, 2:3] * y2_ref[0:tile, :]
                      + route[:, 3:4] * y2_ref[tile:2 * tile, :])

    for s in range(2):
        pl.when(slot == s)(functools.partial(combine_tile, s))


def _combine(x2d, route, y_sorted, pos_tiles):
    T, D = x2d.shape
    tile = TOKEN_TILE
    nt = T // tile
    return pl.pallas_call(
        functools.partial(_combine_body, tile=tile, n_tiles=nt),
        grid=(nt,),
        in_specs=[pl.BlockSpec((1, 1, 2 * tile), lambda i: (0, 0, 0), memory_space=pltpu.SMEM),
                  pl.BlockSpec((1, 1, 2 * tile), lambda i: (jnp.minimum(i + 1, nt - 1), 0, 0),
                               memory_space=pltpu.SMEM),
                  pl.BlockSpec(memory_space=pl.ANY),
                  pl.BlockSpec((tile, D), lambda i: (i, 0)),
                  pl.BlockSpec((tile, LANES), lambda i: (i, 0))],
        out_specs=pl.BlockSpec((tile, D), lambda i: (i, 0)),
        out_shape=jax.ShapeDtypeStruct((T, D), F32),
        scratch_shapes=[pltpu.VMEM((2, 2 * tile, 1, D), F32), pltpu.VMEM((2 * tile, D), F32),
                        pltpu.SemaphoreType.DMA((2,))],
        compiler_params=_params(1, VMEM_LIMIT),
        name="moe_combine",
    )(pos_tiles, pos_tiles, y_sorted, x2d, route)


def _positions_body(route_ref, offs_ref, pos_ref, *, tile):
    r = route_ref[...]
    lane = lax.broadcasted_iota(jnp.int32, r.shape, 1)
    lane_f = lane.astype(F32)
    offs = offs_ref[...]
    pos1 = jnp.sum(jnp.where(lane_f == r[:, 0:1], offs, 0.0), axis=-1, keepdims=True) + r[:, 4:5]
    pos2 = jnp.sum(jnp.where(lane_f == r[:, 1:2], offs, 0.0), axis=-1, keepdims=True) + r[:, 5:6]
    both = jnp.where(lane == 0, pos1, jnp.where(lane == 1, pos2, 0.0))
    for k in range(r.shape[0] // tile):
        t = both[k * tile:(k + 1) * tile, :].T
        pos_ref[k, :, 0:tile] = t[0:1, :].astype(jnp.int32)
        pos_ref[k, :, tile:2 * tile] = t[1:2, :].astype(jnp.int32)


def _positions(route, offs_row):
    T = route.shape[0]
    tile = TOKEN_TILE
    step = 4 * tile
    return pl.pallas_call(
        functools.partial(_positions_body, tile=tile),
        grid=(T // step,),
        in_specs=[pl.BlockSpec((step, LANES), lambda i: (i, 0)),
                  pl.BlockSpec((1, LANES), lambda i: (0, 0))],
        out_specs=pl.BlockSpec((step // tile, 1, 2 * tile), lambda i: (i, 0, 0)),
        out_shape=jax.ShapeDtypeStruct((T // tile, 1, 2 * tile), jnp.int32),
        compiler_params=_params(1, VMEM_LIMIT),
        name="moe_positions",
    )(route, offs_row)


def _dispatch_plan(route, counts, n_experts, tile, cap):
    cnt = counts[0, :n_experts].astype(jnp.int32)
    padded = ((cnt + tile - 1) // tile) * tile
    ends = jnp.cumsum(padded)
    offs = ends - padded
    nt = cap // tile
    n_used = (ends[-1] // tile).astype(jnp.int32)
    first_row = jnp.minimum(jnp.arange(nt, dtype=jnp.int32), n_used - 1) * tile
    te = jnp.sum((ends[None, :] <= first_row[:, None]).astype(jnp.int32), axis=1)
    te = jnp.minimum(te, n_experts - 1).astype(jnp.int32)
    run_start = jnp.concatenate([jnp.ones((1,), bool), te[1:] != te[:-1]])
    run_slot = ((jnp.cumsum(run_start.astype(jnp.int32)) - 1) % 2).astype(jnp.int32)
    ids = jnp.arange(n_experts, dtype=jnp.int32)
    later = (ids[None, :] > ids[:, None]) & (cnt > 0)[None, :]
    following = jnp.min(jnp.where(later, ids[None, :], n_experts), axis=1)
    following = jnp.where(following >= n_experts, -1, following).astype(jnp.int32)
    next_expert = jnp.sum(jnp.where(te[:, None] == ids[None, :], following[None, :], 0), axis=1).astype(jnp.int32)
    offs_row = jnp.pad(offs.astype(F32), (0, LANES - n_experts)).reshape(1, LANES)
    pos_tiles = _positions(route, offs_row)
    return (te, n_used.reshape(1), pos_tiles, (offs + cnt).astype(jnp.int32), (padded - cnt).astype(jnp.int32),
            next_expert, run_slot)


def kernel(x, mix_norm_g, w_in, b_gate, q_norm_g, k_norm_g, lb_logits, hgrn_norm_g, conv_w, conv_b,
           conv_ln_g, conv_ln_b, w_att_o, w_hgrn_o, w_conv_o, w_out, ffn_norm_g, w_coarse, b_coarse,
           w_fine, b_fine, w_exp_in, w_exp_out):
    B, S, D = x.shape
    L = w_in.shape[0]
    T = B * S
    n_experts = w_exp_in.shape[1]
    cap = 2 * T + n_experts * EXPERT_TILE

    w_in_bf = w_in.astype(BF16)
    wa_bf, wh_bf, wc_bf, wo_bf = (w.astype(BF16) for w in (w_att_o, w_hgrn_o, w_conv_o, w_out))
    pad = LANES - N_GROUPS - n_experts
    w_route = jnp.concatenate([w_coarse, w_fine, jnp.zeros((L, D, pad), F32)], axis=-1).astype(F32)
    b_route = jnp.concatenate([b_coarse, b_fine, jnp.zeros((L, pad), F32)], axis=-1).astype(F32)

    x2d = x.reshape(T, D)
    for l in range(L):
        qkv, hf, hqig, cab, gl = _inproj(x2d, mix_norm_g[l], w_in_bf, l)
        att = _attention(qkv, q_norm_g[l], k_norm_g[l], B, S)
        rec = _hgrn(hqig, hf, lb_logits, hgrn_norm_g[l], l, B, S)
        cv = _conv(cab, conv_w[l], conv_b[l], conv_ln_g[l], conv_ln_b[l], B, S)
        x_mid, route, counts = _merge(att, rec, cv, gl, b_gate[l], x2d, wa_bf, wh_bf, wc_bf, wo_bf,
                                      ffn_norm_g[l], w_route[l], b_route[l].reshape(1, LANES), l)
        te, n_used, pos_tiles, pad_start, pad_len, next_expert, run_slot = _dispatch_plan(
            route, counts, n_experts, EXPERT_TILE, cap)
        x_sorted = _dispatch(x_mid, ffn_norm_g[l], pos_tiles, pad_start, pad_len, n_used, cap)
        y_sorted = _experts(x_sorted, w_exp_in, w_exp_out, te, n_used, next_expert, run_slot, l)
        x2d = _combine(x_mid, route, y_sorted, pos_tiles)
    return x2d.reshape(B, S, D)
```

```python
import functools

import jax
import jax.numpy as jnp
import numpy as np
from jax import lax
from jax.experimental import pallas as pl
from jax.experimental.pallas import tpu as pltpu

F32 = jnp.float32
BF16 = jnp.bfloat16

LANES = 128
SUBLANES = 8
RMS_EPS = 1e-6
LN_EPS = 1e-5
NEG_INF = -1e30
LOG2_E = 1.4426950408889634

ATT_HEADS = 8
ATT_HEAD_DIM = 64
MOBA_BLOCK = 256
MOBA_TOPK = 3
HGRN_HEADS = 4
HGRN_DIM = 128
HGRN_ROWS = 256
CONV_WIDTH = 31
CONV_ROWS = 512
CONV_HALO = 32
N_GROUPS = 4
EXPERTS_PER_GROUP = 8
EXPERT_TILE = 256
TOKEN_TILE = 256
INPROJ_TILE = 512
MERGE_TILE = 512
DMA_UNROLL = 8
VMEM_LIMIT = 56 * 1024 * 1024


def _params(n_axes, vmem=None):
    return pltpu.CompilerParams(dimension_semantics=("arbitrary",) * n_axes,
                                vmem_limit_bytes=vmem)


def _dot(a, b):
    return jnp.dot(a, b, preferred_element_type=F32)


def _dot_nt(a, b):
    return lax.dot_general(a, b, (((1,), (1,)), ((), ())), preferred_element_type=F32)


def _dot_tn(a, b):
    return lax.dot_general(a, b, (((0,), (0,)), ((), ())), preferred_element_type=F32)


def _split3(x):
    x1 = x.astype(BF16)
    r1 = x - x1.astype(F32)
    x2 = r1.astype(BF16)
    x3 = (r1 - x2.astype(F32)).astype(BF16)
    return x1, x2, x3


def _sigmoid(x):
    return 1.0 / (1.0 + jnp.exp(-x))


def _silu(x):
    return x * _sigmoid(x)


def _inproj_body(x_ref, g_ref, w_ref, qkv_ref, hf_ref, hqig_ref, cv_ref, gl_ref, *, segs, chunk):
    x = x_ref[...]
    ms = jnp.mean(x * x, axis=-1, keepdims=True)
    h = (x * lax.rsqrt(ms + RMS_EPS) * g_ref[...]).astype(BF16)
    refs = (qkv_ref, hf_ref, hqig_ref, cv_ref, gl_ref)
    for ridx, dst0, src0, width in segs:
        ref = refs[ridx]
        for c in range(0, width, chunk):
            y = _dot(h, w_ref[:, src0 + c:src0 + c + chunk])
            ref[:, dst0 + c:dst0 + c + chunk] = y.astype(ref.dtype)


def _inproj(x2d, norm_g, w_in_bf, layer):
    T, D = x2d.shape
    A = ATT_HEADS * ATT_HEAD_DIM
    K = HGRN_HEADS * HGRN_DIM
    C = A
    segs = ((0, 0, 0, 3 * A),
            (2, 0, 3 * A, K),
            (1, 0, 3 * A + K, K),
            (2, K, 3 * A + 2 * K, K),
            (2, 2 * K, 3 * A + 3 * K, K),
            (3, 0, 3 * A + 4 * K, 2 * C),
            (4, 0, 3 * A + 4 * K + 2 * C, 3 * D))
    ncols = w_in_bf.shape[-1]
    tm = INPROJ_TILE
    row = lambda i: (i, 0)
    return pl.pallas_call(
        functools.partial(_inproj_body, segs=segs, chunk=512),
        grid=(T // tm,),
        in_specs=[pl.BlockSpec((tm, D), row),
                  pl.BlockSpec((1, D), lambda i: (0, 0)),
                  pl.BlockSpec((None, D, ncols), lambda i: (layer, 0, 0))],
        out_specs=[pl.BlockSpec((tm, 3 * A), row), pl.BlockSpec((tm, K), row),
                   pl.BlockSpec((tm, 3 * K), row), pl.BlockSpec((tm, 2 * C), row),
                   pl.BlockSpec((tm, 3 * D), row)],
        out_shape=[jax.ShapeDtypeStruct((T, 3 * A), BF16), jax.ShapeDtypeStruct((T, K), F32),
                   jax.ShapeDtypeStruct((T, 3 * K), BF16), jax.ShapeDtypeStruct((T, 2 * C), BF16),
                   jax.ShapeDtypeStruct((T, 3 * D), BF16)],
        compiler_params=_params(1, VMEM_LIMIT),
        name="inproj",
    )(x2d, norm_g.reshape(1, D), w_in_bf)


def _attn_body(q_ref, k_ref, v_ref, gq_ref, gk_ref, o_ref, kb_ref, vt_ref, s_ref, *, seq, blk, topk):
    nb = seq // blk
    dh = ATT_HEAD_DIM
    dh_sh = dh.bit_length() - 1
    scale = dh ** -0.5 * LOG2_E
    lane = lax.broadcasted_iota(jnp.int32, (1, LANES), 1)
    hr = lax.broadcasted_iota(jnp.int32, (LANES, LANES), 0) >> dh_sh
    hc = lax.broadcasted_iota(jnp.int32, (LANES, LANES), 1) >> dh_sh
    same_head = jnp.where(hr == hc, 1.0, 0.0).astype(BF16)

    def head_norm(x, g):
        x2 = x * x
        hi = x2.astype(BF16)
        lo = (x2 - hi.astype(F32)).astype(BF16)
        ssq = _dot(hi, same_head) + _dot(lo, same_head)
        return x * lax.rsqrt(ssq * (1.0 / dh) + RMS_EPS) * g

    kmeans = []
    for j in range(nb):
        kn = head_norm(k_ref[j * blk:(j + 1) * blk, :].astype(F32), gk_ref[...])
        kb_ref[j * blk:(j + 1) * blk, :] = kn.astype(BF16)
        kmeans.append(jnp.mean(kn, axis=0, keepdims=True))
        vt_ref[:, j * blk:(j + 1) * blk] = v_ref[j * blk:(j + 1) * blk, :].astype(F32).T.astype(BF16)
    kmean = jnp.concatenate(kmeans, axis=0)
    km_hi = kmean.astype(BF16)
    km_lo = (kmean - km_hi.astype(F32)).astype(BF16)
    first_head = lane < dh
    first_head_t = lax.broadcasted_iota(jnp.int32, (LANES, 1), 0) < dh
    blk_n = lax.broadcasted_iota(jnp.int32, (nb, 1), 0)

    key_l = lax.broadcasted_iota(jnp.int32, (blk, 2 * blk), 0)
    qry_l = lax.broadcasted_iota(jnp.int32, (blk, 2 * blk), 1) & (blk - 1)
    causal = key_l <= qry_l

    for i in range(nb):
        rows = head_norm(q_ref[i * blk:(i + 1) * blk, :].astype(F32), gq_ref[...])
        qi = jnp.concatenate([jnp.where(first_head, rows, 0.0), jnp.where(first_head, 0.0, rows)], axis=0)
        qs = (qi * scale).astype(BF16)

        sel = None
        if i > topk:
            q_hi = qi.astype(BF16)
            q_lo = (qi - q_hi.astype(F32)).astype(BF16)
            gate = _dot_nt(km_hi, q_hi) + _dot_nt(km_lo, q_hi) + _dot_nt(km_hi, q_lo)
            valid = blk_n < i
            gm = jnp.where(valid, gate, NEG_INF)
            rank = jnp.zeros(gm.shape, F32)
            for r in range(1, nb):
                gr = pltpu.roll(gm, r, 0)
                ahead = (gr > gm) | ((gr == gm) & (blk_n >= r))
                rank = rank + jnp.where(ahead, 1.0, 0.0)
            sel = jnp.where(valid & (rank < topk), 1.0, 0.0)

        m = None
        for j in range(i + 1):
            s = _dot_nt(kb_ref[j * blk:(j + 1) * blk, :], qs)
            if j == i:
                s = jnp.where(causal, s, NEG_INF)
            elif sel is not None:
                s = jnp.where(sel[j:j + 1, :] > 0.5, s, NEG_INF)
            s_ref[j * blk:(j + 1) * blk, :] = s
            mj = jnp.max(s, axis=0, keepdims=True)
            m = mj if m is None else jnp.maximum(m, mj)

        l = None
        acc = None
        for j in range(i + 1):
            p = jnp.exp2(s_ref[j * blk:(j + 1) * blk, :] - m)
            lj = jnp.sum(p, axis=0, keepdims=True)
            aj = _dot(vt_ref[:, j * blk:(j + 1) * blk], p.astype(BF16))
            l = lj if l is None else l + lj
            acc = aj if acc is None else acc + aj

        ot = acc / l
        merged = jnp.where(first_head_t, ot[:, 0:blk], ot[:, blk:2 * blk])
        o_ref[i * blk:(i + 1) * blk, :] = merged.T.astype(o_ref.dtype)


def _attention(qkv, q_norm_g, k_norm_g, batch, seq):
    T = qkv.shape[0]
    A = ATT_HEADS * ATT_HEAD_DIM
    npair = A // LANES
    reps = LANES // ATT_HEAD_DIM
    gq = jnp.tile(q_norm_g.astype(F32), reps).reshape(1, LANES)
    gk = jnp.tile(k_norm_g.astype(F32), reps).reshape(1, LANES)
    blk_spec = lambda off: pl.BlockSpec((seq, LANES), lambda b, p: (b, off + p))
    vec = pl.BlockSpec((1, LANES), lambda b, p: (0, 0))
    return pl.pallas_call(
        functools.partial(_attn_body, seq=seq, blk=MOBA_BLOCK, topk=MOBA_TOPK),
        grid=(batch, npair),
        in_specs=[blk_spec(0), blk_spec(npair), blk_spec(2 * npair), vec, vec],
        out_specs=pl.BlockSpec((seq, LANES), lambda b, p: (b, p)),
        out_shape=jax.ShapeDtypeStruct((T, A), BF16),
        scratch_shapes=[pltpu.VMEM((seq, LANES), BF16), pltpu.VMEM((LANES, seq), BF16),
                        pltpu.VMEM((seq, 2 * MOBA_BLOCK), F32)],
        compiler_params=_params(2, VMEM_LIMIT),
        name="moba_attn",
    )(qkv, qkv, qkv, gq, gk)


def _hgrn_head(q_ref, f_ref, i_ref, g_ref, lbl_ref, ng_ref, tri_ref, lvl_ref, o_ref, st_ref, *,
               rows, layer, head):
    R = rows
    cols = slice(head * HGRN_DIM, (head + 1) * HGRN_DIM)

    lbl = lbl_ref[:, cols]
    e = jnp.exp(lbl - jnp.max(lbl, axis=0, keepdims=True))
    p = e / jnp.sum(e, axis=0, keepdims=True)
    lb = jnp.maximum(jnp.sum(p[0:layer + 1], axis=0, keepdims=True) - p[0:1], 0.0)

    sig = _sigmoid(f_ref[:, cols])
    g = jnp.log(lb + (1.0 - lb) * sig)
    kin = (1.0 - lb) * (1.0 - sig)
    qa = _silu(q_ref[:, cols].astype(F32))
    vb = i_ref[:, cols]

    t_col = lax.broadcasted_iota(jnp.int32, (R, 1), 0)

    tri = tri_ref[...]
    g1, g2, g3 = _split3(g)
    b = (_dot(tri, g1) + _dot(tri, g2) + _dot(tri, g3)) * LOG2_E

    half = R // 2
    scores = [None, None]
    cross = None
    m = half
    while m >= 1:
        w = 2 * m
        if w >= SUBLANES:
            b3 = b.reshape(R // w, w, LANES)
            bref = jnp.broadcast_to(b3[:, m - 1:m, :], b3.shape).reshape(R, LANES)
        else:
            tl = t_col & (w - 1)
            bref = None
            for resid in range(w):
                shift = resid - (m - 1)
                cand = b if shift == 0 else pltpu.roll(b, shift % R, 0)
                bref = cand if bref is None else jnp.where(tl == resid, cand, bref)
        e = jnp.exp2(-jnp.abs(b - bref))
        qt = (qa * e).astype(BF16)
        kt = (kin * e).astype(BF16)
        if w == R:
            cross = _dot_nt(qt[half:R], kt[0:half])
        else:
            keep = lvl_ref[...] == (m.bit_length() - 1)
            for hh in range(2):
                part = _dot_nt(qt[hh * half:(hh + 1) * half], kt[hh * half:(hh + 1) * half])
                scores[hh] = jnp.where(keep, part, 0.0 if scores[hh] is None else scores[hh])
        m //= 2

    st = st_ref[head]
    o = _dot_nt((qa * jnp.exp2(b)).astype(BF16), st.astype(BF16))
    o = o + jnp.sum(qa * kin, axis=-1, keepdims=True) * vb.astype(F32)
    o_lo = o[0:half] + _dot(scores[0].astype(BF16), vb[0:half])
    o_hi = o[half:R] + _dot(scores[1].astype(BF16), vb[half:R]) + _dot(cross.astype(BF16), vb[0:half])
    o = jnp.concatenate([o_lo, o_hi], axis=0)

    b_end = b[R - 1:R, :]
    k_end = (kin * jnp.exp2(b_end - b)).astype(BF16)
    st_ref[head] = st * jnp.exp2(b_end) + _dot_tn(vb, k_end)

    on = o * lax.rsqrt(jnp.mean(o * o, axis=-1, keepdims=True) + RMS_EPS) * ng_ref[...]
    o_ref[:, cols] = (on * _silu(g_ref[:, cols].astype(F32))).astype(o_ref.dtype)


def _hgrn_body(q_ref, f_ref, i_ref, g_ref, lbl_ref, ng_ref, tri_ref, lvl_ref, o_ref, st_ref, *, rows, layer):
    @pl.when(pl.program_id(1) == 0)
    def _():
        st_ref[...] = jnp.zeros(st_ref.shape, F32)

    for head in range(st_ref.shape[0]):
        _hgrn_head(q_ref, f_ref, i_ref, g_ref, lbl_ref, ng_ref, tri_ref, lvl_ref, o_ref, st_ref,
                   rows=rows, layer=layer, head=head)


def _level_table(n):
    t = np.arange(n, dtype=np.int32)[:, None]
    s = np.arange(n, dtype=np.int32)[None, :]
    x = t ^ s
    hb = np.zeros((n, n), np.int32)
    for k in range(1, n.bit_length()):
        hb = hb + (x >= (1 << k)).astype(np.int32)
    return jnp.asarray(np.where(t > s, hb, -1).astype(np.int32))


def _hgrn(hqig, hf, lb_logits, norm_g, layer, batch, seq):
    T = hf.shape[0]
    H = HGRN_HEADS
    R = HGRN_ROWS
    W = H * HGRN_DIM
    nc = seq // R
    L = lb_logits.shape[0]
    blk = lambda off: pl.BlockSpec((R, W), lambda b, c: (b * nc + c, off))
    const = lambda b, c: (0, 0)
    tri = jnp.asarray(np.tril(np.ones((R, R), np.float32)), dtype=BF16)
    return pl.pallas_call(
        functools.partial(_hgrn_body, rows=R, layer=layer),
        grid=(batch, nc),
        in_specs=[blk(0), blk(0), blk(1), blk(2),
                  pl.BlockSpec((L, W), const), pl.BlockSpec((1, LANES), const),
                  pl.BlockSpec((R, R), const), pl.BlockSpec((R // 2, R // 2), const)],
        out_specs=blk(0),
        out_shape=jax.ShapeDtypeStruct((T, W), BF16),
        scratch_shapes=[pltpu.VMEM((H, HGRN_DIM, HGRN_DIM), F32)],
        compiler_params=_params(2, VMEM_LIMIT),
        name="hgrn2",
    )(hqig, hf, hqig, hqig, lb_logits.astype(F32), norm_g.astype(F32).reshape(1, LANES), tri,
      _level_table(R // 2))


def _conv_body(ab_ref, w_ref, bdw_ref, lg_ref, lb_ref, o_ref, ubuf, cbuf, ush, *, rows, width, chans):
    R, W, C, H = rows, width, chans, CONV_HALO
    c = pl.program_id(1)

    @pl.when(c == 0)
    def _():
        ubuf[0:H, :] = jnp.zeros((H, C), F32)

    a = ab_ref[:, 0:C].astype(F32)
    gate = ab_ref[:, C:2 * C].astype(F32)
    ubuf[H:H + R, :] = a * _sigmoid(gate)

    n_sh = ush.shape[1]
    for s in range(1, SUBLANES):
        ush[s - 1] = ubuf[s:s + n_sh, :]

    rsub = 64
    for lb in range(0, C, LANES):
        for rb in range(0, R, rsub):
            acc = jnp.zeros((rsub, LANES), F32)
            for j in range(W):
                start = H - (W - 1) + j + rb
                s = start % SUBLANES
                a0 = start - s
                if s == 0:
                    u = ubuf[a0:a0 + rsub, lb:lb + LANES]
                else:
                    u = ush[s - 1, a0:a0 + rsub, lb:lb + LANES]
                acc = acc + w_ref[j:j + 1, lb:lb + LANES] * u
            cbuf[rb:rb + rsub, lb:lb + LANES] = acc

    uf = cbuf[...] + bdw_ref[...]
    mu = jnp.mean(uf, axis=-1, keepdims=True)
    d = uf - mu
    var = jnp.mean(d * d, axis=-1, keepdims=True)
    y = d * lax.rsqrt(var + LN_EPS) * lg_ref[...] + lb_ref[...]
    o_ref[...] = _silu(y).astype(o_ref.dtype)
    ubuf[0:H, :] = ubuf[R:R + H, :]


def _conv(cab, conv_w, conv_b, ln_g, ln_b, batch, seq):
    T = cab.shape[0]
    C = cab.shape[1] // 2
    W = conv_w.shape[0]
    R = CONV_ROWS
    nc = seq // R
    vec = pl.BlockSpec((1, C), lambda b, c: (0, 0))
    return pl.pallas_call(
        functools.partial(_conv_body, rows=R, width=W, chans=C),
        grid=(batch, nc),
        in_specs=[pl.BlockSpec((R, 2 * C), lambda b, c: (b * nc + c, 0)),
                  pl.BlockSpec((W, C), lambda b, c: (0, 0)), vec, vec, vec],
        out_specs=pl.BlockSpec((R, C), lambda b, c: (b * nc + c, 0)),
        out_shape=jax.ShapeDtypeStruct((T, C), BF16),
        scratch_shapes=[pltpu.VMEM((R + CONV_HALO, C), F32), pltpu.VMEM((R, C), F32),
                        pltpu.VMEM((SUBLANES - 1, R + CONV_HALO - SUBLANES, C), F32)],
        compiler_params=_params(2, VMEM_LIMIT),
        name="conformer_conv",
    )(cab, conv_w.astype(F32), conv_b.astype(F32).reshape(1, C), ln_g.astype(F32).reshape(1, C),
      ln_b.astype(F32).reshape(1, C))


def _merge_body(att_ref, rec_ref, cv_ref, gl_ref, bg_ref, x_ref, wa_ref, wh_ref, wc_ref, wo_ref,
                fg_ref, wr_ref, br_ref, before_ref, xo_ref, route_ref, cnt_out_ref, mg_ref, cnt_ref, *,
                n_groups, per_group):
    D = x_ref.shape[1]
    att, rec, cv = att_ref[...], rec_ref[...], cv_ref[...]
    cw = 256
    for c in range(0, D, cw):
        def gate(k):
            return _sigmoid(gl_ref[:, k * D + c:k * D + c + cw].astype(F32) + bg_ref[:, k * D + c:k * D + c + cw])
        merged = (gate(0) * _dot(att, wa_ref[:, c:c + cw]) + gate(1) * _dot(rec, wh_ref[:, c:c + cw])
                  + gate(2) * _dot(cv, wc_ref[:, c:c + cw]))
        mg_ref[:, c:c + cw] = merged.astype(BF16)
    xn = x_ref[...] + _dot(mg_ref[...], wo_ref[...])
    xo_ref[...] = xn

    h = xn * lax.rsqrt(jnp.mean(xn * xn, axis=-1, keepdims=True) + RMS_EPS) * fg_ref[...]
    h_hi = h.astype(BF16)
    h_lo = (h - h_hi.astype(F32)).astype(BF16)
    wr = wr_ref[...]
    w_hi = wr.astype(BF16)
    w_lo = (wr - w_hi.astype(F32)).astype(BF16)
    logits = _dot(h_hi, w_hi) + _dot(h_hi, w_lo) + _dot(h_lo, w_hi) + br_ref[...]

    G, E = n_groups, per_group
    lane = lax.broadcasted_iota(jnp.int32, logits.shape, 1)
    big = jnp.int32(LANES)
    is_c = lane < G
    cl = jnp.where(is_c, logits, NEG_INF)
    cm = jnp.max(cl, axis=-1, keepdims=True)
    grp = jnp.min(jnp.where(cl == cm, lane, big), axis=-1, keepdims=True)
    se = jnp.sum(jnp.exp(jnp.where(is_c, logits - cm, NEG_INF)), axis=-1, keepdims=True)
    p_top = 1.0 / se
    lo = G + grp * E
    fl = jnp.where((lane >= lo) & (lane < lo + E), logits, NEG_INF)
    m1 = jnp.max(fl, axis=-1, keepdims=True)
    i1 = jnp.min(jnp.where(fl == m1, lane, big), axis=-1, keepdims=True)
    fl2 = jnp.where(lane == i1, NEG_INF, fl)
    m2 = jnp.max(fl2, axis=-1, keepdims=True)
    i2 = jnp.min(jnp.where(fl2 == m2, lane, big), axis=-1, keepdims=True)
    t = jnp.exp(m2 - m1)
    w1 = p_top / (1.0 + t)
    w2 = p_top * t / (1.0 + t)
    e1 = (i1 - G).astype(F32)
    e2 = (i2 - G).astype(F32)

    @pl.when(pl.program_id(0) == 0)
    def _():
        cnt_ref[...] = jnp.zeros(cnt_ref.shape, F32)

    hit1 = lane == (i1 - G)
    hit2 = lane == (i2 - G)
    hits = jnp.where(hit1 | hit2, 1.0, 0.0)
    prefix = _dot(before_ref[...], hits.astype(BF16)) + cnt_ref[...]
    rank1 = jnp.sum(jnp.where(hit1, prefix, 0.0), axis=-1, keepdims=True)
    rank2 = jnp.sum(jnp.where(hit2, prefix, 0.0), axis=-1, keepdims=True)
    cnt = cnt_ref[...] + jnp.sum(hits, axis=0, keepdims=True)
    cnt_ref[...] = cnt
    cnt_out_ref[...] = cnt

    route_ref[...] = jnp.where(lane == 0, e1, jnp.where(lane == 1, e2,
                               jnp.where(lane == 2, w1, jnp.where(lane == 3, w2,
                               jnp.where(lane == 4, rank1, jnp.where(lane == 5, rank2, 0.0))))))


def _merge(att, rec, cv, gl, b_gate, x2d, wa, wh, wc, wo, ffn_g, w_route, b_route, layer):
    T, D = x2d.shape
    tm = MERGE_TILE
    row = lambda i: (i, 0)
    const2 = lambda i: (0, 0)
    wsel = lambda i: (layer, 0, 0)
    W = att.shape[1]
    before = jnp.asarray(np.tril(np.ones((tm, tm), np.float32), -1), dtype=BF16)
    return pl.pallas_call(
        functools.partial(_merge_body, n_groups=N_GROUPS, per_group=EXPERTS_PER_GROUP),
        grid=(T // tm,),
        in_specs=[pl.BlockSpec((tm, W), row), pl.BlockSpec((tm, W), row), pl.BlockSpec((tm, W), row),
                  pl.BlockSpec((tm, 3 * D), row), pl.BlockSpec((1, 3 * D), const2),
                  pl.BlockSpec((tm, D), row),
                  pl.BlockSpec((None, W, D), wsel), pl.BlockSpec((None, W, D), wsel),
                  pl.BlockSpec((None, W, D), wsel), pl.BlockSpec((None, D, D), wsel),
                  pl.BlockSpec((1, D), const2), pl.BlockSpec((D, LANES), const2),
                  pl.BlockSpec((1, LANES), const2), pl.BlockSpec((tm, tm), const2)],
        out_specs=[pl.BlockSpec((tm, D), row), pl.BlockSpec((tm, LANES), row),
                   pl.BlockSpec((1, LANES), const2)],
        out_shape=[jax.ShapeDtypeStruct((T, D), F32), jax.ShapeDtypeStruct((T, LANES), F32),
                   jax.ShapeDtypeStruct((1, LANES), F32)],
        scratch_shapes=[pltpu.VMEM((tm, D), BF16), pltpu.VMEM((1, LANES), F32)],
        compiler_params=_params(1, VMEM_LIMIT),
        name="merge_route",
    )(att, rec, cv, gl, b_gate.astype(F32).reshape(1, 3 * D), x2d, wa, wh, wc, wo,
      ffn_g.astype(F32).reshape(1, D), w_route, b_route, before)


def _bulk_wait(view, sem):
    pltpu.make_async_copy(view, view, sem).wait()


def _dispatch_body(ps_ref, pn_ref, nu_ref, pos_ref, x_ref, fg_ref, xs_hbm, hbuf, zbuf, sem, zsem, *,
                   tile, n_tiles, n_experts, n_sorted_tiles):
    i = pl.program_id(0)
    slot = i % 2
    ztile = zbuf.shape[0]

    def pad_copies(e):
        out = []
        for k in range(ztile.bit_length() - 1):
            first = ps_ref[e] + ((pn_ref[e] >> (k + 1)) << (k + 1))
            copy = pltpu.make_async_copy(zbuf.at[pl.ds(0, 1 << k)], xs_hbm.at[pl.ds(first, 1 << k)], zsem)
            out.append((((pn_ref[e] >> k) & 1) == 1, copy))
        return out

    def tail_copy(t):
        return pltpu.make_async_copy(zbuf, xs_hbm.at[pl.ds(t * ztile, ztile)], zsem)

    def wait_slot(s):
        _bulk_wait(hbuf.at[s], sem.at[s])
        _bulk_wait(hbuf.at[s], sem.at[s])

    @pl.when(i == 0)
    def _():
        zbuf[...] = jnp.zeros(zbuf.shape, F32)
        for e in range(n_experts):
            for bit_set, copy in pad_copies(e):
                pl.when(bit_set)(copy.start)

        def start_tail(t, carry):
            tail_copy(t).start()
            return carry
        lax.fori_loop(nu_ref[0], n_sorted_tiles, start_tail, 0)

    def scatter_tile(s):
        @pl.when(i >= 2)
        def _():
            wait_slot(s)

        x = x_ref[...]
        h = x * lax.rsqrt(jnp.mean(x * x, axis=-1, keepdims=True) + RMS_EPS) * fg_ref[...]
        hbuf[s] = h.reshape(h.shape[0], 1, h.shape[1])

        def issue(g, carry):
            for k in range(DMA_UNROLL):
                r = g * DMA_UNROLL + k
                src = hbuf.at[s, pl.ds(r, 1)]
                pltpu.make_async_copy(src, xs_hbm.at[pl.ds(pos_ref[0, 0, r], 1)],
                                      sem.at[s]).start(priority=0)
                pltpu.make_async_copy(src, xs_hbm.at[pl.ds(pos_ref[0, 0, tile + r], 1)],
                                      sem.at[s]).start(priority=1)
            return carry

        lax.fori_loop(0, tile // DMA_UNROLL, issue, 0)

    for s in range(2):
        pl.when(slot == s)(functools.partial(scatter_tile, s))

    @pl.when(i == n_tiles - 1)
    def _():
        wait_slot((n_tiles - 1) % 2)
        if n_tiles >= 2:
            wait_slot(n_tiles % 2)
        for e in range(n_experts):
            for bit_set, copy in pad_copies(e):
                pl.when(bit_set)(copy.wait)

        def done_tail(t, carry):
            tail_copy(t).wait()
            return carry
        lax.fori_loop(nu_ref[0], n_sorted_tiles, done_tail, 0)


def _dispatch(x2d, ffn_g, pos_tiles, pad_start, pad_len, n_used, cap):
    T, D = x2d.shape
    tile = TOKEN_TILE
    nt = T // tile
    n_experts = pad_start.shape[0]
    grid_spec = pltpu.PrefetchScalarGridSpec(
        num_scalar_prefetch=3,
        grid=(nt,),
        in_specs=[pl.BlockSpec((1, 1, 2 * tile), lambda i, ps, pn, nu: (i, 0, 0), memory_space=pltpu.SMEM),
                  pl.BlockSpec((tile, D), lambda i, ps, pn, nu: (i, 0)),
                  pl.BlockSpec((1, D), lambda i, ps, pn, nu: (0, 0))],
        out_specs=pl.BlockSpec(memory_space=pl.ANY),
        scratch_shapes=[pltpu.VMEM((2, tile, 1, D), F32), pltpu.VMEM((EXPERT_TILE, 1, D), F32),
                        pltpu.SemaphoreType.DMA((2,)), pltpu.SemaphoreType.DMA(())],
    )
    return pl.pallas_call(
        functools.partial(_dispatch_body, tile=tile, n_tiles=nt, n_experts=n_experts,
                          n_sorted_tiles=cap // EXPERT_TILE),
        grid_spec=grid_spec,
        out_shape=jax.ShapeDtypeStruct((cap, 1, D), F32),
        compiler_params=_params(1, VMEM_LIMIT),
        name="moe_dispatch",
    )(pad_start, pad_len, n_used, pos_tiles, x2d, ffn_g.astype(F32).reshape(1, D))


def _expert_body(te_ref, nu_ref, nx_ref, sl_ref, x_ref, win_hbm, wout_hbm, y_ref, win_bf, wout_bf, x2_ref,
                 win_st, wout_st, wsem, *, layer):
    i = pl.program_id(0)
    ff = wout_bf.shape[0]
    used = i < nu_ref[0]
    new_expert = jnp.logical_or(i == 0, te_ref[i] != te_ref[jnp.maximum(i - 1, 0)])

    def weight_copies(e, slot):
        return (pltpu.make_async_copy(win_hbm.at[layer, e], win_st.at[slot], wsem.at[0, slot]),
                pltpu.make_async_copy(wout_hbm.at[layer, e], wout_st.at[slot], wsem.at[1, slot]))

    @pl.when(i == 0)
    def _():
        for c in weight_copies(te_ref[0], 0):
            c.start()

    def begin_run(s):
        for c in weight_copies(te_ref[i], s):
            c.wait()
        win_bf[...] = win_st[s].astype(BF16)
        wout_bf[...] = wout_st[s].astype(BF16)

        @pl.when(nx_ref[i] >= 0)
        def _():
            for c in weight_copies(nx_ref[i], 1 - s):
                c.start(priority=1)

    for s in range(2):
        pl.when(jnp.logical_and(jnp.logical_and(used, new_expert), sl_ref[i] == s))(
            functools.partial(begin_run, s))

    @pl.when(used)
    def _():
        x2_ref[...] = x_ref[...].reshape(x2_ref.shape)
        gu = _dot(x2_ref[...].astype(BF16), win_bf[...])
        act = (_silu(gu[:, 0:ff]) * gu[:, ff:2 * ff]).astype(BF16)
        y = _dot(act, wout_bf[...])
        y_ref[...] = y.reshape(y_ref.shape)

    @pl.when(jnp.logical_not(used))
    def _():
        y_ref[...] = jnp.zeros(y_ref.shape, F32)


def _experts(x_sorted, w_exp_in, w_exp_out, tile_expert, n_used, next_expert, run_slot, layer):
    cap, _, D = x_sorted.shape
    tile = EXPERT_TILE
    nt = cap // tile
    ff = w_exp_out.shape[2]
    rows = lambda i, te, nu, nx, sl: (i, 0, 0)
    grid_spec = pltpu.PrefetchScalarGridSpec(
        num_scalar_prefetch=4,
        grid=(nt,),
        in_specs=[pl.BlockSpec((tile, 1, D), rows),
                  pl.BlockSpec(memory_space=pl.ANY), pl.BlockSpec(memory_space=pl.ANY)],
        out_specs=pl.BlockSpec((tile, 1, D), rows),
        scratch_shapes=[pltpu.VMEM((D, 2 * ff), BF16), pltpu.VMEM((ff, D), BF16),
                        pltpu.VMEM((tile, D), F32),
                        pltpu.VMEM((2, D, 2 * ff), F32), pltpu.VMEM((2, ff, D), F32),
                        pltpu.SemaphoreType.DMA((2, 2))],
    )
    return pl.pallas_call(
        functools.partial(_expert_body, layer=layer),
        grid_spec=grid_spec,
        out_shape=jax.ShapeDtypeStruct((cap, 1, D), F32),
        compiler_params=_params(1, VMEM_LIMIT),
        name="expert_ffn",
    )(tile_expert, n_used, next_expert, run_slot, x_sorted, w_exp_in, w_exp_out)


def _combine_body(pos0_ref, posn_ref, y_hbm, x_ref, route_ref, o_ref, ybuf, y2_ref, sem, *, tile, n_tiles):
    i = pl.program_id(0)
    slot = i % 2

    def gather(pos_ref, s):
        def issue(g, carry):
            for k in range(DMA_UNROLL):
                r = g * DMA_UNROLL + k
                pltpu.make_async_copy(y_hbm.at[pl.ds(pos_ref[0, 0, r], 1)],
                                      ybuf.at[s, pl.ds(r, 1)], sem.at[s]).start(priority=k % 2)
            return carry
        lax.fori_loop(0, 2 * tile // DMA_UNROLL, issue, 0)

    @pl.when(i == 0)
    def _():
        gather(pos0_ref, 0)

    def combine_tile(s):
        @pl.when(i + 1 < n_tiles)
        def _():
            gather(posn_ref, 1 - s)

        _bulk_wait(ybuf.at[s], sem.at[s])
        y2_ref[...] = ybuf[s].reshape(y2_ref.shape)
        route = route_ref[...]
        o_ref[...] = (x_ref[...] + route[:, 2:3] * y2_ref[0:tile, :]
                      + route[:, 3:4] * y2_ref[tile:2 * tile, :])

    for s in range(2):
        pl.when(slot == s)(functools.partial(combine_tile, s))


def _combine(x2d, route, y_sorted, pos_tiles):
    T, D = x2d.shape
    tile = TOKEN_TILE
    nt = T // tile
    return pl.pallas_call(
        functools.partial(_combine_body, tile=tile, n_tiles=nt),
        grid=(nt,),
        in_specs=[pl.BlockSpec((1, 1, 2 * tile), lambda i: (0, 0, 0), memory_space=pltpu.SMEM),
                  pl.BlockSpec((1, 1, 2 * tile), lambda i: (jnp.minimum(i + 1, nt - 1), 0, 0),
                               memory_space=pltpu.SMEM),
                  pl.BlockSpec(memory_space=pl.ANY),
                  pl.BlockSpec((tile, D), lambda i: (i, 0)),
                  pl.BlockSpec((tile, LANES), lambda i: (i, 0))],
        out_specs=pl.BlockSpec((tile, D), lambda i: (i, 0)),
        out_shape=jax.ShapeDtypeStruct((T, D), F32),
        scratch_shapes=[pltpu.VMEM((2, 2 * tile, 1, D), F32), pltpu.VMEM((2 * tile, D), F32),
                        pltpu.SemaphoreType.DMA((2,))],
        compiler_params=_params(1, VMEM_LIMIT),
        name="moe_combine",
    )(pos_tiles, pos_tiles, y_sorted, x2d, route)


def _positions_body(route_ref, offs_ref, pos_ref, *, tile):
    r = route_ref[...]
    lane = lax.broadcasted_iota(jnp.int32, r.shape, 1)
    lane_f = lane.astype(F32)
    offs = offs_ref[...]
    pos1 = jnp.sum(jnp.where(lane_f == r[:, 0:1], offs, 0.0), axis=-1, keepdims=True) + r[:, 4:5]
    pos2 = jnp.sum(jnp.where(lane_f == r[:, 1:2], offs, 0.0), axis=-1, keepdims=True) + r[:, 5:6]
    both = jnp.where(lane == 0, pos1, jnp.where(lane == 1, pos2, 0.0))
    for k in range(r.shape[0] // tile):
        t = both[k * tile:(k + 1) * tile, :].T
        pos_ref[k, :, 0:tile] = t[0:1, :].astype(jnp.int32)
        pos_ref[k, :, tile:2 * tile] = t[1:2, :].astype(jnp.int32)


def _positions(route, offs_row):
    T = route.shape[0]
    tile = TOKEN_TILE
    step = 4 * tile
    return pl.pallas_call(
        functools.partial(_positions_body, tile=tile),
        grid=(T // step,),
        in_specs=[pl.BlockSpec((step, LANES), lambda i: (i, 0)),
                  pl.BlockSpec((1, LANES), lambda i: (0, 0))],
        out_specs=pl.BlockSpec((step // tile, 1, 2 * tile), lambda i: (i, 0, 0)),
        out_shape=jax.ShapeDtypeStruct((T // tile, 1, 2 * tile), jnp.int32),
        compiler_params=_params(1, VMEM_LIMIT),
        name="moe_positions",
    )(route, offs_row)


def _dispatch_plan(route, counts, n_experts, tile, cap):
    cnt = counts[0, :n_experts].astype(jnp.int32)
    padded = ((cnt + tile - 1) // tile) * tile
    ends = jnp.cumsum(padded)
    offs = ends - padded
    nt = cap // tile
    n_used = (ends[-1] // tile).astype(jnp.int32)
    first_row = jnp.minimum(jnp.arange(nt, dtype=jnp.int32), n_used - 1) * tile
    te = jnp.sum((ends[None, :] <= first_row[:, None]).astype(jnp.int32), axis=1)
    te = jnp.minimum(te, n_experts - 1).astype(jnp.int32)
    run_start = jnp.concatenate([jnp.ones((1,), bool), te[1:] != te[:-1]])
    run_slot = ((jnp.cumsum(run_start.astype(jnp.int32)) - 1) % 2).astype(jnp.int32)
    ids = jnp.arange(n_experts, dtype=jnp.int32)
    later = (ids[None, :] > ids[:, None]) & (cnt > 0)[None, :]
    following = jnp.min(jnp.where(later, ids[None, :], n_experts), axis=1)
    following = jnp.where(following >= n_experts, -1, following).astype(jnp.int32)
    next_expert = jnp.sum(jnp.where(te[:, None] == ids[None, :], following[None, :], 0), axis=1).astype(jnp.int32)
    offs_row = jnp.pad(offs.astype(F32), (0, LANES - n_experts)).reshape(1, LANES)
    pos_tiles = _positions(route, offs_row)
    return (te, n_used.reshape(1), pos_tiles, (offs + cnt).astype(jnp.int32), (padded - cnt).astype(jnp.int32),
            next_expert, run_slot)


def kernel(x, mix_norm_g, w_in, b_gate, q_norm_g, k_norm_g, lb_logits, hgrn_norm_g, conv_w, conv_b,
           conv_ln_g, conv_ln_b, w_att_o, w_hgrn_o, w_conv_o, w_out, ffn_norm_g, w_coarse, b_coarse,
           w_fine, b_fine, w_exp_in, w_exp_out):
    B, S, D = x.shape
    L = w_in.shape[0]
    T = B * S
    n_experts = w_exp_in.shape[1]
    cap = 2 * T + n_experts * EXPERT_TILE

    w_in_bf = w_in.astype(BF16)
    wa_bf, wh_bf, wc_bf, wo_bf = (w.astype(BF16) for w in (w_att_o, w_hgrn_o, w_conv_o, w_out))
    pad = LANES - N_GROUPS - n_experts
    w_route = jnp.concatenate([w_coarse, w_fine, jnp.zeros((L, D, pad), F32)], axis=-1).astype(F32)
    b_route = jnp.concatenate([b_coarse, b_fine, jnp.zeros((L, pad), F32)], axis=-1).astype(F32)

    x2d = x.reshape(T, D)
    for l in range(L):
        qkv, hf, hqig, cab, gl = _inproj(x2d, mix_norm_g[l], w_in_bf, l)
        att = _attention(qkv, q_norm_g[l], k_norm_g[l], B, S)
        rec = _hgrn(hqig, hf, lb_logits, hgrn_norm_g[l], l, B, S)
        cv = _conv(cab, conv_w[l], conv_b[l], conv_ln_g[l], conv_ln_b[l], B, S)
        x_mid, route, counts = _merge(att, rec, cv, gl, b_gate[l], x2d, wa_bf, wh_bf, wc_bf, wo_bf,
                                      ffn_norm_g[l], w_route[l], b_route[l].reshape(1, LANES), l)
        te, n_used, pos_tiles, pad_start, pad_len, next_expert, run_slot = _dispatch_plan(
            route, counts, n_experts, EXPERT_TILE, cap)
        x_sorted = _dispatch(x_mid, ffn_norm_g[l], pos_tiles, pad_start, pad_len, n_used, cap)
        y_sorted = _experts(x_sorted, w_exp_in, w_exp_out, te, n_used, next_expert, run_slot, l)
        x2d = _combine(x_mid, route, y_sorted, pos_tiles)
    return x2d.reshape(B, S, D)
```

```python
import functools

import jax
import jax.numpy as jnp
import numpy as np
from jax import lax
from jax.experimental import pallas as pl
from jax.experimental.pallas import tpu as pltpu

F32 = jnp.float32
BF16 = jnp.bfloat16

LANES = 128
SUBLANES = 8
RMS_EPS = 1e-6
LN_EPS = 1e-5
NEG_INF = -1e30
LOG2_E = 1.4426950408889634

ATT_HEADS = 8
ATT_HEAD_DIM = 64
MOBA_BLOCK = 256
MOBA_TOPK = 3
HGRN_HEADS = 4
HGRN_DIM = 128
HGRN_ROWS = 256
CONV_WIDTH = 31
CONV_ROWS = 512
CONV_HALO = 32
N_GROUPS = 4
EXPERTS_PER_GROUP = 8
EXPERT_TILE = 512
TOKEN_TILE = 256
INPROJ_TILE = 512
MERGE_TILE = 512
DMA_UNROLL = 8
VMEM_LIMIT = 56 * 1024 * 1024


def _params(n_axes, vmem=None):
    return pltpu.CompilerParams(dimension_semantics=("arbitrary",) * n_axes,
                                vmem_limit_bytes=vmem)


def _dot(a, b):
    return jnp.dot(a, b, preferred_element_type=F32)


def _dot_nt(a, b):
    return lax.dot_general(a, b, (((1,), (1,)), ((), ())), preferred_element_type=F32)


def _dot_tn(a, b):
    return lax.dot_general(a, b, (((0,), (0,)), ((), ())), preferred_element_type=F32)


def _split3(x):
    x1 = x.astype(BF16)
    r1 = x - x1.astype(F32)
    x2 = r1.astype(BF16)
    x3 = (r1 - x2.astype(F32)).astype(BF16)
    return x1, x2, x3


def _sigmoid(x):
    return 1.0 / (1.0 + jnp.exp(-x))


def _silu(x):
    return x * _sigmoid(x)


def _inproj_body(x_ref, g_ref, w_ref, qkv_ref, hf_ref, hqig_ref, cv_ref, gl_ref, *, segs, chunk):
    x = x_ref[...]
    ms = jnp.mean(x * x, axis=-1, keepdims=True)
    h = (x * lax.rsqrt(ms + RMS_EPS) * g_ref[...]).astype(BF16)
    refs = (qkv_ref, hf_ref, hqig_ref, cv_ref, gl_ref)
    for ridx, dst0, src0, width in segs:
        ref = refs[ridx]
        for c in range(0, width, chunk):
            y = _dot(h, w_ref[:, src0 + c:src0 + c + chunk])
            ref[:, dst0 + c:dst0 + c + chunk] = y.astype(ref.dtype)


def _inproj(x2d, norm_g, w_in_bf, layer):
    T, D = x2d.shape
    A = ATT_HEADS * ATT_HEAD_DIM
    K = HGRN_HEADS * HGRN_DIM
    C = A
    segs = ((0, 0, 0, 3 * A),
            (2, 0, 3 * A, K),
            (1, 0, 3 * A + K, K),
            (2, K, 3 * A + 2 * K, K),
            (2, 2 * K, 3 * A + 3 * K, K),
            (3, 0, 3 * A + 4 * K, 2 * C),
            (4, 0, 3 * A + 4 * K + 2 * C, 3 * D))
    ncols = w_in_bf.shape[-1]
    tm = INPROJ_TILE
    row = lambda i: (i, 0)
    return pl.pallas_call(
        functools.partial(_inproj_body, segs=segs, chunk=512),
        grid=(T // tm,),
        in_specs=[pl.BlockSpec((tm, D), row),
                  pl.BlockSpec((1, D), lambda i: (0, 0)),
                  pl.BlockSpec((None, D, ncols), lambda i: (layer, 0, 0))],
        out_specs=[pl.BlockSpec((tm, 3 * A), row), pl.BlockSpec((tm, K), row),
                   pl.BlockSpec((tm, 3 * K), row), pl.BlockSpec((tm, 2 * C), row),
                   pl.BlockSpec((tm, 3 * D), row)],
        out_shape=[jax.ShapeDtypeStruct((T, 3 * A), BF16), jax.ShapeDtypeStruct((T, K), F32),
                   jax.ShapeDtypeStruct((T, 3 * K), BF16), jax.ShapeDtypeStruct((T, 2 * C), BF16),
                   jax.ShapeDtypeStruct((T, 3 * D), BF16)],
        compiler_params=_params(1, VMEM_LIMIT),
        name="inproj",
    )(x2d, norm_g.reshape(1, D), w_in_bf)


def _attn_body(q_ref, k_ref, v_ref, gq_ref, gk_ref, o_ref, kb_ref, vt_ref, s_ref, *, seq, blk, topk):
    nb = seq // blk
    dh = ATT_HEAD_DIM
    dh_sh = dh.bit_length() - 1
    scale = dh ** -0.5 * LOG2_E
    lane = lax.broadcasted_iota(jnp.int32, (1, LANES), 1)
    hr = lax.broadcasted_iota(jnp.int32, (LANES, LANES), 0) >> dh_sh
    hc = lax.broadcasted_iota(jnp.int32, (LANES, LANES), 1) >> dh_sh
    same_head = jnp.where(hr == hc, 1.0, 0.0).astype(BF16)

    def head_norm(x, g):
        x2 = x * x
        hi = x2.astype(BF16)
        lo = (x2 - hi.astype(F32)).astype(BF16)
        ssq = _dot(hi, same_head) + _dot(lo, same_head)
        return x * lax.rsqrt(ssq * (1.0 / dh) + RMS_EPS) * g

    kmeans = []
    for j in range(nb):
        kn = head_norm(k_ref[j * blk:(j + 1) * blk, :].astype(F32), gk_ref[...])
        kb_ref[j * blk:(j + 1) * blk, :] = kn.astype(BF16)
        kmeans.append(jnp.mean(kn, axis=0, keepdims=True))
        vt_ref[:, j * blk:(j + 1) * blk] = v_ref[j * blk:(j + 1) * blk, :].astype(F32).T.astype(BF16)
    kmean = jnp.concatenate(kmeans, axis=0)
    km_hi = kmean.astype(BF16)
    km_lo = (kmean - km_hi.astype(F32)).astype(BF16)
    first_head = lane < dh
    first_head_t = lax.broadcasted_iota(jnp.int32, (LANES, 1), 0) < dh
    blk_n = lax.broadcasted_iota(jnp.int32, (nb, 1), 0)

    key_l = lax.broadcasted_iota(jnp.int32, (blk, 2 * blk), 0)
    qry_l = lax.broadcasted_iota(jnp.int32, (blk, 2 * blk), 1) & (blk - 1)
    causal = key_l <= qry_l

    for i in range(nb):
        rows = head_norm(q_ref[i * blk:(i + 1) * blk, :].astype(F32), gq_ref[...])
        qi = jnp.concatenate([jnp.where(first_head, rows, 0.0), jnp.where(first_head, 0.0, rows)], axis=0)
        qs = (qi * scale).astype(BF16)

        sel = None
        if i > topk:
            q_hi = qi.astype(BF16)
            q_lo = (qi - q_hi.astype(F32)).astype(BF16)
            gate = _dot_nt(km_hi, q_hi) + _dot_nt(km_lo, q_hi) + _dot_nt(km_hi, q_lo)
            valid = blk_n < i
            gm = jnp.where(valid, gate, NEG_INF)
            rank = jnp.zeros(gm.shape, F32)
            for r in range(1, nb):
                gr = pltpu.roll(gm, r, 0)
                ahead = (gr > gm) | ((gr == gm) & (blk_n >= r))
                rank = rank + jnp.where(ahead, 1.0, 0.0)
            sel = jnp.where(valid & (rank < topk), 1.0, 0.0)

        m = None
        for j in range(i + 1):
            s = _dot_nt(kb_ref[j * blk:(j + 1) * blk, :], qs)
            if j == i:
                s = jnp.where(causal, s, NEG_INF)
            elif sel is not None:
                s = jnp.where(sel[j:j + 1, :] > 0.5, s, NEG_INF)
            s_ref[j * blk:(j + 1) * blk, :] = s
            mj = jnp.max(s, axis=0, keepdims=True)
            m = mj if m is None else jnp.maximum(m, mj)

        l = None
        acc = None
        for j in range(i + 1):
            p = jnp.exp2(s_ref[j * blk:(j + 1) * blk, :] - m)
            lj = jnp.sum(p, axis=0, keepdims=True)
            aj = _dot(vt_ref[:, j * blk:(j + 1) * blk], p.astype(BF16))
            l = lj if l is None else l + lj
            acc = aj if acc is None else acc + aj

        ot = acc / l
        merged = jnp.where(first_head_t, ot[:, 0:blk], ot[:, blk:2 * blk])
        o_ref[i * blk:(i + 1) * blk, :] = merged.T.astype(o_ref.dtype)


def _attention(qkv, q_norm_g, k_norm_g, batch, seq):
    T = qkv.shape[0]
    A = ATT_HEADS * ATT_HEAD_DIM
    npair = A // LANES
    reps = LANES // ATT_HEAD_DIM
    gq = jnp.tile(q_norm_g.astype(F32), reps).reshape(1, LANES)
    gk = jnp.tile(k_norm_g.astype(F32), reps).reshape(1, LANES)
    blk_spec = lambda off: pl.BlockSpec((seq, LANES), lambda b, p: (b, off + p))
    vec = pl.BlockSpec((1, LANES), lambda b, p: (0, 0))
    return pl.pallas_call(
        functools.partial(_attn_body, seq=seq, blk=MOBA_BLOCK, topk=MOBA_TOPK),
        grid=(batch, npair),
        in_specs=[blk_spec(0), blk_spec(npair), blk_spec(2 * npair), vec, vec],
        out_specs=pl.BlockSpec((seq, LANES), lambda b, p: (b, p)),
        out_shape=jax.ShapeDtypeStruct((T, A), BF16),
        scratch_shapes=[pltpu.VMEM((seq, LANES), BF16), pltpu.VMEM((LANES, seq), BF16),
                        pltpu.VMEM((seq, 2 * MOBA_BLOCK), F32)],
        compiler_params=_params(2, VMEM_LIMIT),
        name="moba_attn",
    )(qkv, qkv, qkv, gq, gk)


def _hgrn_head(q_ref, f_ref, i_ref, g_ref, lbl_ref, ng_ref, tri_ref, lvl_ref, o_ref, st_ref, *,
               rows, layer, head):
    R = rows
    cols = slice(head * HGRN_DIM, (head + 1) * HGRN_DIM)

    lbl = lbl_ref[:, cols]
    e = jnp.exp(lbl - jnp.max(lbl, axis=0, keepdims=True))
    p = e / jnp.sum(e, axis=0, keepdims=True)
    lb = jnp.maximum(jnp.sum(p[0:layer + 1], axis=0, keepdims=True) - p[0:1], 0.0)

    sig = _sigmoid(f_ref[:, cols])
    g = jnp.log(lb + (1.0 - lb) * sig)
    kin = (1.0 - lb) * (1.0 - sig)
    qa = _silu(q_ref[:, cols].astype(F32))
    vb = i_ref[:, cols]

    t_col = lax.broadcasted_iota(jnp.int32, (R, 1), 0)

    tri = tri_ref[...]
    g1, g2, g3 = _split3(g)
    b = (_dot(tri, g1) + _dot(tri, g2) + _dot(tri, g3)) * LOG2_E

    half = R // 2
    scores = [None, None]
    cross = None
    m = half
    while m >= 1:
        w = 2 * m
        if w >= SUBLANES:
            b3 = b.reshape(R // w, w, LANES)
            bref = jnp.broadcast_to(b3[:, m - 1:m, :], b3.shape).reshape(R, LANES)
        else:
            tl = t_col & (w - 1)
            bref = None
            for resid in range(w):
                shift = resid - (m - 1)
                cand = b if shift == 0 else pltpu.roll(b, shift % R, 0)
                bref = cand if bref is None else jnp.where(tl == resid, cand, bref)
        e = jnp.exp2(-jnp.abs(b - bref))
        qt = (qa * e).astype(BF16)
        kt = (kin * e).astype(BF16)
        if w == R:
            cross = _dot_nt(qt[half:R], kt[0:half])
        else:
            keep = lvl_ref[...] == (m.bit_length() - 1)
            for hh in range(2):
                part = _dot_nt(qt[hh * half:(hh + 1) * half], kt[hh * half:(hh + 1) * half])
                scores[hh] = jnp.where(keep, part, 0.0 if scores[hh] is None else scores[hh])
        m //= 2

    st = st_ref[head]
    o = _dot_nt((qa * jnp.exp2(b)).astype(BF16), st.astype(BF16))
    o = o + jnp.sum(qa * kin, axis=-1, keepdims=True) * vb.astype(F32)
    o_lo = o[0:half] + _dot(scores[0].astype(BF16), vb[0:half])
    o_hi = o[half:R] + _dot(scores[1].astype(BF16), vb[half:R]) + _dot(cross.astype(BF16), vb[0:half])
    o = jnp.concatenate([o_lo, o_hi], axis=0)

    b_end = b[R - 1:R, :]
    k_end = (kin * jnp.exp2(b_end - b)).astype(BF16)
    st_ref[head] = st * jnp.exp2(b_end) + _dot_tn(vb, k_end)

    on = o * lax.rsqrt(jnp.mean(o * o, axis=-1, keepdims=True) + RMS_EPS) * ng_ref[...]
    o_ref[:, cols] = (on * _silu(g_ref[:, cols].astype(F32))).astype(o_ref.dtype)


def _hgrn_body(q_ref, f_ref, i_ref, g_ref, lbl_ref, ng_ref, tri_ref, lvl_ref, o_ref, st_ref, *, rows, layer):
    @pl.when(pl.program_id(1) == 0)
    def _():
        st_ref[...] = jnp.zeros(st_ref.shape, F32)

    for head in range(st_ref.shape[0]):
        _hgrn_head(q_ref, f_ref, i_ref, g_ref, lbl_ref, ng_ref, tri_ref, lvl_ref, o_ref, st_ref,
                   rows=rows, layer=layer, head=head)


def _level_table(n):
    t = np.arange(n, dtype=np.int32)[:, None]
    s = np.arange(n, dtype=np.int32)[None, :]
    x = t ^ s
    hb = np.zeros((n, n), np.int32)
    for k in range(1, n.bit_length()):
        hb = hb + (x >= (1 << k)).astype(np.int32)
    return jnp.asarray(np.where(t > s, hb, -1).astype(np.int32))


def _hgrn(hqig, hf, lb_logits, norm_g, layer, batch, seq):
    T = hf.shape[0]
    H = HGRN_HEADS
    R = HGRN_ROWS
    W = H * HGRN_DIM
    nc = seq // R
    L = lb_logits.shape[0]
    blk = lambda off: pl.BlockSpec((R, W), lambda b, c: (b * nc + c, off))
    const = lambda b, c: (0, 0)
    tri = jnp.asarray(np.tril(np.ones((R, R), np.float32)), dtype=BF16)
    return pl.pallas_call(
        functools.partial(_hgrn_body, rows=R, layer=layer),
        grid=(batch, nc),
        in_specs=[blk(0), blk(0), blk(1), blk(2),
                  pl.BlockSpec((L, W), const), pl.BlockSpec((1, LANES), const),
                  pl.BlockSpec((R, R), const), pl.BlockSpec((R // 2, R // 2), const)],
        out_specs=blk(0),
        out_shape=jax.ShapeDtypeStruct((T, W), BF16),
        scratch_shapes=[pltpu.VMEM((H, HGRN_DIM, HGRN_DIM), F32)],
        compiler_params=_params(2, VMEM_LIMIT),
        name="hgrn2",
    )(hqig, hf, hqig, hqig, lb_logits.astype(F32), norm_g.astype(F32).reshape(1, LANES), tri,
      _level_table(R // 2))


def _conv_body(ab_ref, w_ref, bdw_ref, lg_ref, lb_ref, o_ref, ubuf, cbuf, ush, *, rows, width, chans):
    R, W, C, H = rows, width, chans, CONV_HALO
    c = pl.program_id(1)

    @pl.when(c == 0)
    def _():
        ubuf[0:H, :] = jnp.zeros((H, C), F32)

    a = ab_ref[:, 0:C].astype(F32)
    gate = ab_ref[:, C:2 * C].astype(F32)
    ubuf[H:H + R, :] = a * _sigmoid(gate)

    n_sh = ush.shape[1]
    for s in range(1, SUBLANES):
        ush[s - 1] = ubuf[s:s + n_sh, :]

    rsub = 64
    for lb in range(0, C, LANES):
        for rb in range(0, R, rsub):
            acc = jnp.zeros((rsub, LANES), F32)
            for j in range(W):
                start = H - (W - 1) + j + rb
                s = start % SUBLANES
                a0 = start - s
                if s == 0:
                    u = ubuf[a0:a0 + rsub, lb:lb + LANES]
                else:
                    u = ush[s - 1, a0:a0 + rsub, lb:lb + LANES]
                acc = acc + w_ref[j:j + 1, lb:lb + LANES] * u
            cbuf[rb:rb + rsub, lb:lb + LANES] = acc

    uf = cbuf[...] + bdw_ref[...]
    mu = jnp.mean(uf, axis=-1, keepdims=True)
    d = uf - mu
    var = jnp.mean(d * d, axis=-1, keepdims=True)
    y = d * lax.rsqrt(var + LN_EPS) * lg_ref[...] + lb_ref[...]
    o_ref[...] = _silu(y).astype(o_ref.dtype)
    ubuf[0:H, :] = ubuf[R:R + H, :]


def _conv(cab, conv_w, conv_b, ln_g, ln_b, batch, seq):
    T = cab.shape[0]
    C = cab.shape[1] // 2
    W = conv_w.shape[0]
    R = CONV_ROWS
    nc = seq // R
    vec = pl.BlockSpec((1, C), lambda b, c: (0, 0))
    return pl.pallas_call(
        functools.partial(_conv_body, rows=R, width=W, chans=C),
        grid=(batch, nc),
        in_specs=[pl.BlockSpec((R, 2 * C), lambda b, c: (b * nc + c, 0)),
                  pl.BlockSpec((W, C), lambda b, c: (0, 0)), vec, vec, vec],
        out_specs=pl.BlockSpec((R, C), lambda b, c: (b * nc + c, 0)),
        out_shape=jax.ShapeDtypeStruct((T, C), BF16),
        scratch_shapes=[pltpu.VMEM((R + CONV_HALO, C), F32), pltpu.VMEM((R, C), F32),
                        pltpu.VMEM((SUBLANES - 1, R + CONV_HALO - SUBLANES, C), F32)],
        compiler_params=_params(2, VMEM_LIMIT),
        name="conformer_conv",
    )(cab, conv_w.astype(F32), conv_b.astype(F32).reshape(1, C), ln_g.astype(F32).reshape(1, C),
      ln_b.astype(F32).reshape(1, C))


def _merge_body(att_ref, rec_ref, cv_ref, gl_ref, bg_ref, x_ref, wa_ref, wh_ref, wc_ref, wo_ref,
                fg_ref, wr_ref, br_ref, before_ref, xo_ref, route_ref, cnt_out_ref, mg_ref, cnt_ref, *,
                n_groups, per_group):
    D = x_ref.shape[1]
    att, rec, cv = att_ref[...], rec_ref[...], cv_ref[...]
    cw = 256
    for c in range(0, D, cw):
        def gate(k):
            return _sigmoid(gl_ref[:, k * D + c:k * D + c + cw].astype(F32) + bg_ref[:, k * D + c:k * D + c + cw])
        merged = (gate(0) * _dot(att, wa_ref[:, c:c + cw]) + gate(1) * _dot(rec, wh_ref[:, c:c + cw])
                  + gate(2) * _dot(cv, wc_ref[:, c:c + cw]))
        mg_ref[:, c:c + cw] = merged.astype(BF16)
    xn = x_ref[...] + _dot(mg_ref[...], wo_ref[...])
    xo_ref[...] = xn

    h = xn * lax.rsqrt(jnp.mean(xn * xn, axis=-1, keepdims=True) + RMS_EPS) * fg_ref[...]
    h_hi = h.astype(BF16)
    h_lo = (h - h_hi.astype(F32)).astype(BF16)
    wr = wr_ref[...]
    w_hi = wr.astype(BF16)
    w_lo = (wr - w_hi.astype(F32)).astype(BF16)
    logits = _dot(h_hi, w_hi) + _dot(h_hi, w_lo) + _dot(h_lo, w_hi) + br_ref[...]

    G, E = n_groups, per_group
    lane = lax.broadcasted_iota(jnp.int32, logits.shape, 1)
    big = jnp.int32(LANES)
    is_c = lane < G
    cl = jnp.where(is_c, logits, NEG_INF)
    cm = jnp.max(cl, axis=-1, keepdims=True)
    grp = jnp.min(jnp.where(cl == cm, lane, big), axis=-1, keepdims=True)
    se = jnp.sum(jnp.exp(jnp.where(is_c, logits - cm, NEG_INF)), axis=-1, keepdims=True)
    p_top = 1.0 / se
    lo = G + grp * E
    fl = jnp.where((lane >= lo) & (lane < lo + E), logits, NEG_INF)
    m1 = jnp.max(fl, axis=-1, keepdims=True)
    i1 = jnp.min(jnp.where(fl == m1, lane, big), axis=-1, keepdims=True)
    fl2 = jnp.where(lane == i1, NEG_INF, fl)
    m2 = jnp.max(fl2, axis=-1, keepdims=True)
    i2 = jnp.min(jnp.where(fl2 == m2, lane, big), axis=-1, keepdims=True)
    t = jnp.exp(m2 - m1)
    w1 = p_top / (1.0 + t)
    w2 = p_top * t / (1.0 + t)
    e1 = (i1 - G).astype(F32)
    e2 = (i2 - G).astype(F32)

    @pl.when(pl.program_id(0) == 0)
    def _():
        cnt_ref[...] = jnp.zeros(cnt_ref.shape, F32)

    hit1 = lane == (i1 - G)
    hit2 = lane == (i2 - G)
    hits = jnp.where(hit1 | hit2, 1.0, 0.0)
    prefix = _dot(before_ref[...], hits.astype(BF16)) + cnt_ref[...]
    rank1 = jnp.sum(jnp.where(hit1, prefix, 0.0), axis=-1, keepdims=True)
    rank2 = jnp.sum(jnp.where(hit2, prefix, 0.0), axis=-1, keepdims=True)
    cnt = cnt_ref[...] + jnp.sum(hits, axis=0, keepdims=True)
    cnt_ref[...] = cnt
    cnt_out_ref[...] = cnt

    route_ref[...] = jnp.where(lane == 0, e1, jnp.where(lane == 1, e2,
                               jnp.where(lane == 2, w1, jnp.where(lane == 3, w2,
                               jnp.where(lane == 4, rank1, jnp.where(lane == 5, rank2, 0.0))))))


def _merge(att, rec, cv, gl, b_gate, x2d, wa, wh, wc, wo, ffn_g, w_route, b_route, layer):
    T, D = x2d.shape
    tm = MERGE_TILE
    row = lambda i: (i, 0)
    const2 = lambda i: (0, 0)
    wsel = lambda i: (layer, 0, 0)
    W = att.shape[1]
    before = jnp.asarray(np.tril(np.ones((tm, tm), np.float32), -1), dtype=BF16)
    return pl.pallas_call(
        functools.partial(_merge_body, n_groups=N_GROUPS, per_group=EXPERTS_PER_GROUP),
        grid=(T // tm,),
        in_specs=[pl.BlockSpec((tm, W), row), pl.BlockSpec((tm, W), row), pl.BlockSpec((tm, W), row),
                  pl.BlockSpec((tm, 3 * D), row), pl.BlockSpec((1, 3 * D), const2),
                  pl.BlockSpec((tm, D), row),
                  pl.BlockSpec((None, W, D), wsel), pl.BlockSpec((None, W, D), wsel),
                  pl.BlockSpec((None, W, D), wsel), pl.BlockSpec((None, D, D), wsel),
                  pl.BlockSpec((1, D), const2), pl.BlockSpec((D, LANES), const2),
                  pl.BlockSpec((1, LANES), const2), pl.BlockSpec((tm, tm), const2)],
        out_specs=[pl.BlockSpec((tm, D), row), pl.BlockSpec((tm, LANES), row),
                   pl.BlockSpec((1, LANES), const2)],
        out_shape=[jax.ShapeDtypeStruct((T, D), F32), jax.ShapeDtypeStruct((T, LANES), F32),
                   jax.ShapeDtypeStruct((1, LANES), F32)],
        scratch_shapes=[pltpu.VMEM((tm, D), BF16), pltpu.VMEM((1, LANES), F32)],
        compiler_params=_params(1, VMEM_LIMIT),
        name="merge_route",
    )(att, rec, cv, gl, b_gate.astype(F32).reshape(1, 3 * D), x2d, wa, wh, wc, wo,
      ffn_g.astype(F32).reshape(1, D), w_route, b_route, before)


def _bulk_wait(view, sem):
    pltpu.make_async_copy(view, view, sem).wait()


def _dispatch_body(ps_ref, pn_ref, nu_ref, pos_ref, x_ref, fg_ref, xs_hbm, hbuf, zbuf, sem, zsem, *,
                   tile, n_tiles, n_experts, n_sorted_tiles):
    i = pl.program_id(0)
    slot = i % 2
    ztile = zbuf.shape[0]

    def pad_copies(e):
        out = []
        for k in range(ztile.bit_length() - 1):
            first = ps_ref[e] + ((pn_ref[e] >> (k + 1)) << (k + 1))
            copy = pltpu.make_async_copy(zbuf.at[pl.ds(0, 1 << k)], xs_hbm.at[pl.ds(first, 1 << k)], zsem)
            out.append((((pn_ref[e] >> k) & 1) == 1, copy))
        return out

    def tail_copy(t):
        return pltpu.make_async_copy(zbuf, xs_hbm.at[pl.ds(t * ztile, ztile)], zsem)

    def wait_slot(s):
        _bulk_wait(hbuf.at[s], sem.at[s])
        _bulk_wait(hbuf.at[s], sem.at[s])

    @pl.when(i == 0)
    def _():
        zbuf[...] = jnp.zeros(zbuf.shape, F32)
        for e in range(n_experts):
            for bit_set, copy in pad_copies(e):
                pl.when(bit_set)(copy.start)

        def start_tail(t, carry):
            tail_copy(t).start()
            return carry
        lax.fori_loop(nu_ref[0], n_sorted_tiles, start_tail, 0)

    def scatter_tile(s):
        @pl.when(i >= 2)
        def _():
            wait_slot(s)

        x = x_ref[...]
        h = x * lax.rsqrt(jnp.mean(x * x, axis=-1, keepdims=True) + RMS_EPS) * fg_ref[...]
        hbuf[s] = h.reshape(h.shape[0], 1, h.shape[1])

        def issue(g, carry):
            for k in range(DMA_UNROLL):
                r = g * DMA_UNROLL + k
                src = hbuf.at[s, pl.ds(r, 1)]
                pltpu.make_async_copy(src, xs_hbm.at[pl.ds(pos_ref[0, 0, r], 1)],
                                      sem.at[s]).start(priority=0)
                pltpu.make_async_copy(src, xs_hbm.at[pl.ds(pos_ref[0, 0, tile + r], 1)],
                                      sem.at[s]).start(priority=1)
            return carry

        lax.fori_loop(0, tile // DMA_UNROLL, issue, 0)

    for s in range(2):
        pl.when(slot == s)(functools.partial(scatter_tile, s))

    @pl.when(i == n_tiles - 1)
    def _():
        wait_slot((n_tiles - 1) % 2)
        if n_tiles >= 2:
            wait_slot(n_tiles % 2)
        for e in range(n_experts):
            for bit_set, copy in pad_copies(e):
                pl.when(bit_set)(copy.wait)

        def done_tail(t, carry):
            tail_copy(t).wait()
            return carry
        lax.fori_loop(nu_ref[0], n_sorted_tiles, done_tail, 0)


def _dispatch(x2d, ffn_g, pos_tiles, pad_start, pad_len, n_used, cap):
    T, D = x2d.shape
    tile = TOKEN_TILE
    nt = T // tile
    n_experts = pad_start.shape[0]
    grid_spec = pltpu.PrefetchScalarGridSpec(
        num_scalar_prefetch=3,
        grid=(nt,),
        in_specs=[pl.BlockSpec((1, 1, 2 * tile), lambda i, ps, pn, nu: (i, 0, 0), memory_space=pltpu.SMEM),
                  pl.BlockSpec((tile, D), lambda i, ps, pn, nu: (i, 0)),
                  pl.BlockSpec((1, D), lambda i, ps, pn, nu: (0, 0))],
        out_specs=pl.BlockSpec(memory_space=pl.ANY),
        scratch_shapes=[pltpu.VMEM((2, tile, 1, D), F32), pltpu.VMEM((EXPERT_TILE, 1, D), F32),
                        pltpu.SemaphoreType.DMA((2,)), pltpu.SemaphoreType.DMA(())],
    )
    return pl.pallas_call(
        functools.partial(_dispatch_body, tile=tile, n_tiles=nt, n_experts=n_experts,
                          n_sorted_tiles=cap // EXPERT_TILE),
        grid_spec=grid_spec,
        out_shape=jax.ShapeDtypeStruct((cap, 1, D), F32),
        compiler_params=_params(1, VMEM_LIMIT),
        name="moe_dispatch",
    )(pad_start, pad_len, n_used, pos_tiles, x2d, ffn_g.astype(F32).reshape(1, D))


def _expert_body(te_ref, nu_ref, nx_ref, sl_ref, x_ref, win_hbm, wout_hbm, y_ref, win_bf, wout_bf, x2_ref,
                 win_st, wout_st, wsem, *, layer):
    i = pl.program_id(0)
    ff = wout_bf.shape[0]
    used = i < nu_ref[0]
    new_expert = jnp.logical_or(i == 0, te_ref[i] != te_ref[jnp.maximum(i - 1, 0)])

    def weight_copies(e, slot):
        return (pltpu.make_async_copy(win_hbm.at[layer, e], win_st.at[slot], wsem.at[0, slot]),
                pltpu.make_async_copy(wout_hbm.at[layer, e], wout_st.at[slot], wsem.at[1, slot]))

    @pl.when(i == 0)
    def _():
        for c in weight_copies(te_ref[0], 0):
            c.start()

    def begin_run(s):
        for c in weight_copies(te_ref[i], s):
            c.wait()
        win_bf[...] = win_st[s].astype(BF16)
        wout_bf[...] = wout_st[s].astype(BF16)

        @pl.when(nx_ref[i] >= 0)
        def _():
            for c in weight_copies(nx_ref[i], 1 - s):
                c.start()

    for s in range(2):
        pl.when(jnp.logical_and(jnp.logical_and(used, new_expert), sl_ref[i] == s))(
            functools.partial(begin_run, s))

    @pl.when(used)
    def _():
        x2_ref[...] = x_ref[...].reshape(x2_ref.shape)
        gu = _dot(x2_ref[...].astype(BF16), win_bf[...])
        act = (_silu(gu[:, 0:ff]) * gu[:, ff:2 * ff]).astype(BF16)
        y = _dot(act, wout_bf[...])
        y_ref[...] = y.reshape(y_ref.shape)

    @pl.when(jnp.logical_not(used))
    def _():
        y_ref[...] = jnp.zeros(y_ref.shape, F32)


def _experts(x_sorted, w_exp_in, w_exp_out, tile_expert, n_used, next_expert, run_slot, layer):
    cap, _, D = x_sorted.shape
    tile = EXPERT_TILE
    nt = cap // tile
    ff = w_exp_out.shape[2]
    rows = lambda i, te, nu, nx, sl: (i, 0, 0)
    grid_spec = pltpu.PrefetchScalarGridSpec(
        num_scalar_prefetch=4,
        grid=(nt,),
        in_specs=[pl.BlockSpec((tile, 1, D), rows),
                  pl.BlockSpec(memory_space=pl.ANY), pl.BlockSpec(memory_space=pl.ANY)],
        out_specs=pl.BlockSpec((tile, 1, D), rows),
        scratch_shapes=[pltpu.VMEM((D, 2 * ff), BF16), pltpu.VMEM((ff, D), BF16),
                        pltpu.VMEM((tile, D), F32),
                        pltpu.VMEM((2, D, 2 * ff), F32), pltpu.VMEM((2, ff, D), F32),
                        pltpu.SemaphoreType.DMA((2, 2))],
    )
    return pl.pallas_call(
        functools.partial(_expert_body, layer=layer),
        grid_spec=grid_spec,
        out_shape=jax.ShapeDtypeStruct((cap, 1, D), F32),
        compiler_params=_params(1, VMEM_LIMIT),
        name="expert_ffn",
    )(tile_expert, n_used, next_expert, run_slot, x_sorted, w_exp_in, w_exp_out)


def _combine_body(pos0_ref, posn_ref, y_hbm, x_ref, route_ref, o_ref, ybuf, y2_ref, sem, *, tile, n_tiles):
    i = pl.program_id(0)
    slot = i % 2

    def gather(pos_ref, s):
        def issue(g, carry):
            for k in range(DMA_UNROLL):
                r = g * DMA_UNROLL + k
                pltpu.make_async_copy(y_hbm.at[pl.ds(pos_ref[0, 0, r], 1)],
                                      ybuf.at[s, pl.ds(r, 1)], sem.at[s]).start(priority=k % 2)
            return carry
        lax.fori_loop(0, 2 * tile // DMA_UNROLL, issue, 0)

    @pl.when(i == 0)
    def _():
        gather(pos0_ref, 0)

    def combine_tile(s):
        @pl.when(i + 1 < n_tiles)
        def _():
            gather(posn_ref, 1 - s)

        _bulk_wait(ybuf.at[s], sem.at[s])
        y2_ref[...] = ybuf[s].reshape(y2_ref.shape)
        route = route_ref[...]
        o_ref[...] = (x_ref[...] + route[:, 2:3] * y2_ref[0:tile, :]
                      + route[:, 3:4] * y2_ref[tile:2 * tile, :])

    for s in range(2):
        pl.when(slot == s)(functools.partial(combine_tile, s))


def _combine(x2d, route, y_sorted, pos_tiles):
    T, D = x2d.shape
    tile = TOKEN_TILE
    nt = T // tile
    return pl.pallas_call(
        functools.partial(_combine_body, tile=tile, n_tiles=nt),
        grid=(nt,),
        in_specs=[pl.BlockSpec((1, 1, 2 * tile), lambda i: (0, 0, 0), memory_space=pltpu.SMEM),
                  pl.BlockSpec((1, 1, 2 * tile), lambda i: (jnp.minimum(i + 1, nt - 1), 0, 0),
                               memory_space=pltpu.SMEM),
                  pl.BlockSpec(memory_space=pl.ANY),
                  pl.BlockSpec((tile, D), lambda i: (i, 0)),
                  pl.BlockSpec((tile, LANES), lambda i: (i, 0))],
        out_specs=pl.BlockSpec((tile, D), lambda i: (i, 0)),
        out_shape=jax.ShapeDtypeStruct((T, D), F32),
        scratch_shapes=[pltpu.VMEM((2, 2 * tile, 1, D), F32), pltpu.VMEM((2 * tile, D), F32),
                        pltpu.SemaphoreType.DMA((2,))],
        compiler_params=_params(1, VMEM_LIMIT),
        name="moe_combine",
    )(pos_tiles, pos_tiles, y_sorted, x2d, route)


def _positions_body(route_ref, offs_ref, pos_ref, *, tile):
    r = route_ref[...]
    lane = lax.broadcasted_iota(jnp.int32, r.shape, 1)
    lane_f = lane.astype(F32)
    offs = offs_ref[...]
    pos1 = jnp.sum(jnp.where(lane_f == r[:, 0:1], offs, 0.0), axis=-1, keepdims=True) + r[:, 4:5]
    pos2 = jnp.sum(jnp.where(lane_f == r[:, 1:2], offs, 0.0), axis=-1, keepdims=True) + r[:, 5:6]
    both = jnp.where(lane == 0, pos1, jnp.where(lane == 1, pos2, 0.0))
    for k in range(r.shape[0] // tile):
        t = both[k * tile:(k + 1) * tile, :].T
        pos_ref[k, :, 0:tile] = t[0:1, :].astype(jnp.int32)
        pos_ref[k, :, tile:2 * tile] = t[1:2, :].astype(jnp.int32)


def _positions(route, offs_row):
    T = route.shape[0]
    tile = TOKEN_TILE
    step = 4 * tile
    return pl.pallas_call(
        functools.partial(_positions_body, tile=tile),
        grid=(T // step,),
        in_specs=[pl.BlockSpec((step, LANES), lambda i: (i, 0)),
                  pl.BlockSpec((1, LANES), lambda i: (0, 0))],
        out_specs=pl.BlockSpec((step // tile, 1, 2 * tile), lambda i: (i, 0, 0)),
        out_shape=jax.ShapeDtypeStruct((T // tile, 1, 2 * tile), jnp.int32),
        compiler_params=_params(1, VMEM_LIMIT),
        name="moe_positions",
    )(route, offs_row)


def _dispatch_plan(route, counts, n_experts, tile, cap):
    cnt = counts[0, :n_experts].astype(jnp.int32)
    padded = ((cnt + tile - 1) // tile) * tile
    ends = jnp.cumsum(padded)
    offs = ends - padded
    nt = cap // tile
    n_used = (ends[-1] // tile).astype(jnp.int32)
    first_row = jnp.minimum(jnp.arange(nt, dtype=jnp.int32), n_used - 1) * tile
    te = jnp.sum((ends[None, :] <= first_row[:, None]).astype(jnp.int32), axis=1)
    te = jnp.minimum(te, n_experts - 1).astype(jnp.int32)
    run_start = jnp.concatenate([jnp.ones((1,), bool), te[1:] != te[:-1]])
    run_slot = ((jnp.cumsum(run_start.astype(jnp.int32)) - 1) % 2).astype(jnp.int32)
    ids = jnp.arange(n_experts, dtype=jnp.int32)
    later = (ids[None, :] > ids[:, None]) & (cnt > 0)[None, :]
    following = jnp.min(jnp.where(later, ids[None, :], n_experts), axis=1)
    following = jnp.where(following >= n_experts, -1, following).astype(jnp.int32)
    next_expert = jnp.sum(jnp.where(te[:, None] == ids[None, :], following[None, :], 0), axis=1).astype(jnp.int32)
    offs_row = jnp.pad(offs.astype(F32), (0, LANES - n_experts)).reshape(1, LANES)
    pos_tiles = _positions(route, offs_row)
    return (te, n_used.reshape(1), pos_tiles, (offs + cnt).astype(jnp.int32), (padded - cnt).astype(jnp.int32),
            next_expert, run_slot)


def kernel(x, mix_norm_g, w_in, b_gate, q_norm_g, k_norm_g, lb_logits, hgrn_norm_g, conv_w, conv_b,
           conv_ln_g, conv_ln_b, w_att_o, w_hgrn_o, w_conv_o, w_out, ffn_norm_g, w_coarse, b_coarse,
           w_fine, b_fine, w_exp_in, w_exp_out):
    B, S, D = x.shape
    L = w_in.shape[0]
    T = B * S
    n_experts = w_exp_in.shape[1]
    cap = 2 * T + n_experts * EXPERT_TILE

    w_in_bf = w_in.astype(BF16)
    wa_bf, wh_bf, wc_bf, wo_bf = (w.astype(BF16) for w in (w_att_o, w_hgrn_o, w_conv_o, w_out))
    pad = LANES - N_GROUPS - n_experts
    w_route = jnp.concatenate([w_coarse, w_fine, jnp.zeros((L, D, pad), F32)], axis=-1).astype(F32)
    b_route = jnp.concatenate([b_coarse, b_fine, jnp.zeros((L, pad), F32)], axis=-1).astype(F32)

    x2d = x.reshape(T, D)
    for l in range(L):
        qkv, hf, hqig, cab, gl = _inproj(x2d, mix_norm_g[l], w_in_bf, l)
        att = _attention(qkv, q_norm_g[l], k_norm_g[l], B, S)
        rec = _hgrn(hqig, hf, lb_logits, hgrn_norm_g[l], l, B, S)
        cv = _conv(cab, conv_w[l], conv_b[l], conv_ln_g[l], conv_ln_b[l], B, S)
        x_mid, route, counts = _merge(att, rec, cv, gl, b_gate[l], x2d, wa_bf, wh_bf, wc_bf, wo_bf,
                                      ffn_norm_g[l], w_route[l], b_route[l].reshape(1, LANES), l)
        te, n_used, pos_tiles, pad_start, pad_len, next_expert, run_slot = _dispatch_plan(
            route, counts, n_experts, EXPERT_TILE, cap)
        x_sorted = _dispatch(x_mid, ffn_norm_g[l], pos_tiles, pad_start, pad_len, n_used, cap)
        y_sorted = _experts(x_sorted, w_exp_in, w_exp_out, te, n_used, next_expert, run_slot, l)
        x2d = _combine(x_mid, route, y_sorted, pos_tiles)
    return x2d.reshape(B, S, D)
```

```python
import functools

import jax
import jax.numpy as jnp
import numpy as np
from jax import lax
from jax.experimental import pallas as pl
from jax.experimental.pallas import tpu as pltpu

F32 = jnp.float32
BF16 = jnp.bfloat16

LANES = 128
SUBLANES = 8
RMS_EPS = 1e-6
LN_EPS = 1e-5
NEG_INF = -1e30
LOG2_E = 1.4426950408889634

ATT_HEADS = 8
ATT_HEAD_DIM = 64
MOBA_BLOCK = 256
MOBA_TOPK = 3
HGRN_HEADS = 4
HGRN_DIM = 128
HGRN_ROWS = 256
CONV_WIDTH = 31
CONV_ROWS = 512
CONV_HALO = 32
N_GROUPS = 4
EXPERTS_PER_GROUP = 8
EXPERT_TILE = 512
TOKEN_TILE = 512
INPROJ_TILE = 512
MERGE_TILE = 512
DMA_UNROLL = 8
VMEM_LIMIT = 56 * 1024 * 1024


def _params(n_axes, vmem=None):
    return pltpu.CompilerParams(dimension_semantics=("arbitrary",) * n_axes,
                                vmem_limit_bytes=vmem)


def _dot(a, b):
    return jnp.dot(a, b, preferred_element_type=F32)


def _dot_nt(a, b):
    return lax.dot_general(a, b, (((1,), (1,)), ((), ())), preferred_element_type=F32)


def _dot_tn(a, b):
    return lax.dot_general(a, b, (((0,), (0,)), ((), ())), preferred_element_type=F32)


def _split3(x):
    x1 = x.astype(BF16)
    r1 = x - x1.astype(F32)
    x2 = r1.astype(BF16)
    x3 = (r1 - x2.astype(F32)).astype(BF16)
    return x1, x2, x3


def _sigmoid(x):
    return 1.0 / (1.0 + jnp.exp(-x))


def _silu(x):
    return x * _sigmoid(x)


def _inproj_body(x_ref, g_ref, w_ref, qkv_ref, hf_ref, hqig_ref, cv_ref, gl_ref, *, segs, chunk):
    x = x_ref[...]
    ms = jnp.mean(x * x, axis=-1, keepdims=True)
    h = (x * lax.rsqrt(ms + RMS_EPS) * g_ref[...]).astype(BF16)
    refs = (qkv_ref, hf_ref, hqig_ref, cv_ref, gl_ref)
    for ridx, dst0, src0, width in segs:
        ref = refs[ridx]
        for c in range(0, width, chunk):
            y = _dot(h, w_ref[:, src0 + c:src0 + c + chunk])
            ref[:, dst0 + c:dst0 + c + chunk] = y.astype(ref.dtype)


def _inproj(x2d, norm_g, w_in_bf, layer):
    T, D = x2d.shape
    A = ATT_HEADS * ATT_HEAD_DIM
    K = HGRN_HEADS * HGRN_DIM
    C = A
    segs = ((0, 0, 0, 3 * A),
            (2, 0, 3 * A, K),
            (1, 0, 3 * A + K, K),
            (2, K, 3 * A + 2 * K, K),
            (2, 2 * K, 3 * A + 3 * K, K),
            (3, 0, 3 * A + 4 * K, 2 * C),
            (4, 0, 3 * A + 4 * K + 2 * C, 3 * D))
    ncols = w_in_bf.shape[-1]
    tm = INPROJ_TILE
    row = lambda i: (i, 0)
    return pl.pallas_call(
        functools.partial(_inproj_body, segs=segs, chunk=512),
        grid=(T // tm,),
        in_specs=[pl.BlockSpec((tm, D), row),
                  pl.BlockSpec((1, D), lambda i: (0, 0)),
                  pl.BlockSpec((None, D, ncols), lambda i: (layer, 0, 0))],
        out_specs=[pl.BlockSpec((tm, 3 * A), row), pl.BlockSpec((tm, K), row),
                   pl.BlockSpec((tm, 3 * K), row), pl.BlockSpec((tm, 2 * C), row),
                   pl.BlockSpec((tm, 3 * D), row)],
        out_shape=[jax.ShapeDtypeStruct((T, 3 * A), BF16), jax.ShapeDtypeStruct((T, K), F32),
                   jax.ShapeDtypeStruct((T, 3 * K), BF16), jax.ShapeDtypeStruct((T, 2 * C), BF16),
                   jax.ShapeDtypeStruct((T, 3 * D), BF16)],
        compiler_params=_params(1, VMEM_LIMIT),
        name="inproj",
    )(x2d, norm_g.reshape(1, D), w_in_bf)


def _attn_body(q_ref, k_ref, v_ref, gq_ref, gk_ref, o_ref, kb_ref, vt_ref, s_ref, *, seq, blk, topk):
    nb = seq // blk
    dh = ATT_HEAD_DIM
    dh_sh = dh.bit_length() - 1
    scale = dh ** -0.5 * LOG2_E
    lane = lax.broadcasted_iota(jnp.int32, (1, LANES), 1)
    hr = lax.broadcasted_iota(jnp.int32, (LANES, LANES), 0) >> dh_sh
    hc = lax.broadcasted_iota(jnp.int32, (LANES, LANES), 1) >> dh_sh
    same_head = jnp.where(hr == hc, 1.0, 0.0).astype(BF16)

    def head_norm(x, g):
        x2 = x * x
        hi = x2.astype(BF16)
        lo = (x2 - hi.astype(F32)).astype(BF16)
        ssq = _dot(hi, same_head) + _dot(lo, same_head)
        return x * lax.rsqrt(ssq * (1.0 / dh) + RMS_EPS) * g

    kmeans = []
    for j in range(nb):
        kn = head_norm(k_ref[j * blk:(j + 1) * blk, :].astype(F32), gk_ref[...])
        kb_ref[j * blk:(j + 1) * blk, :] = kn.astype(BF16)
        kmeans.append(jnp.mean(kn, axis=0, keepdims=True))
        vt_ref[:, j * blk:(j + 1) * blk] = v_ref[j * blk:(j + 1) * blk, :].astype(F32).T.astype(BF16)
    kmean = jnp.concatenate(kmeans, axis=0)
    km_hi = kmean.astype(BF16)
    km_lo = (kmean - km_hi.astype(F32)).astype(BF16)
    first_head = lane < dh
    first_head_t = lax.broadcasted_iota(jnp.int32, (LANES, 1), 0) < dh
    blk_n = lax.broadcasted_iota(jnp.int32, (nb, 1), 0)

    key_l = lax.broadcasted_iota(jnp.int32, (blk, 2 * blk), 0)
    qry_l = lax.broadcasted_iota(jnp.int32, (blk, 2 * blk), 1) & (blk - 1)
    causal = key_l <= qry_l

    for i in range(nb):
        rows = head_norm(q_ref[i * blk:(i + 1) * blk, :].astype(F32), gq_ref[...])
        qi = jnp.concatenate([jnp.where(first_head, rows, 0.0), jnp.where(first_head, 0.0, rows)], axis=0)
        qs = (qi * scale).astype(BF16)

        sel = None
        if i > topk:
            q_hi = qi.astype(BF16)
            q_lo = (qi - q_hi.astype(F32)).astype(BF16)
            gate = _dot_nt(km_hi, q_hi) + _dot_nt(km_lo, q_hi) + _dot_nt(km_hi, q_lo)
            valid = blk_n < i
            gm = jnp.where(valid, gate, NEG_INF)
            rank = jnp.zeros(gm.shape, F32)
            for r in range(1, nb):
                gr = pltpu.roll(gm, r, 0)
                ahead = (gr > gm) | ((gr == gm) & (blk_n >= r))
                rank = rank + jnp.where(ahead, 1.0, 0.0)
            sel = jnp.where(valid & (rank < topk), 1.0, 0.0)

        m = None
        for j in range(i + 1):
            s = _dot_nt(kb_ref[j * blk:(j + 1) * blk, :], qs)
            if j == i:
                s = jnp.where(causal, s, NEG_INF)
            elif sel is not None:
                s = jnp.where(sel[j:j + 1, :] > 0.5, s, NEG_INF)
            s_ref[j * blk:(j + 1) * blk, :] = s
            mj = jnp.max(s, axis=0, keepdims=True)
            m = mj if m is None else jnp.maximum(m, mj)

        l = None
        acc = None
        for j in range(i + 1):
            p = jnp.exp2(s_ref[j * blk:(j + 1) * blk, :] - m)
            lj = jnp.sum(p, axis=0, keepdims=True)
            aj = _dot(vt_ref[:, j * blk:(j + 1) * blk], p.astype(BF16))
            l = lj if l is None else l + lj
            acc = aj if acc is None else acc + aj

        ot = acc / l
        merged = jnp.where(first_head_t, ot[:, 0:blk], ot[:, blk:2 * blk])
        o_ref[i * blk:(i + 1) * blk, :] = merged.T.astype(o_ref.dtype)


def _attention(qkv, q_norm_g, k_norm_g, batch, seq):
    T = qkv.shape[0]
    A = ATT_HEADS * ATT_HEAD_DIM
    npair = A // LANES
    reps = LANES // ATT_HEAD_DIM
    gq = jnp.tile(q_norm_g.astype(F32), reps).reshape(1, LANES)
    gk = jnp.tile(k_norm_g.astype(F32), reps).reshape(1, LANES)
    blk_spec = lambda off: pl.BlockSpec((seq, LANES), lambda b, p: (b, off + p))
    vec = pl.BlockSpec((1, LANES), lambda b, p: (0, 0))
    return pl.pallas_call(
        functools.partial(_attn_body, seq=seq, blk=MOBA_BLOCK, topk=MOBA_TOPK),
        grid=(batch, npair),
        in_specs=[blk_spec(0), blk_spec(npair), blk_spec(2 * npair), vec, vec],
        out_specs=pl.BlockSpec((seq, LANES), lambda b, p: (b, p)),
        out_shape=jax.ShapeDtypeStruct((T, A), BF16),
        scratch_shapes=[pltpu.VMEM((seq, LANES), BF16), pltpu.VMEM((LANES, seq), BF16),
                        pltpu.VMEM((seq, 2 * MOBA_BLOCK), F32)],
        compiler_params=_params(2, VMEM_LIMIT),
        name="moba_attn",
    )(qkv, qkv, qkv, gq, gk)


def _hgrn_head(q_ref, f_ref, i_ref, g_ref, lbl_ref, ng_ref, tri_ref, lvl_ref, o_ref, st_ref, *,
               rows, layer, head):
    R = rows
    cols = slice(head * HGRN_DIM, (head + 1) * HGRN_DIM)

    lbl = lbl_ref[:, cols]
    e = jnp.exp(lbl - jnp.max(lbl, axis=0, keepdims=True))
    p = e / jnp.sum(e, axis=0, keepdims=True)
    lb = jnp.maximum(jnp.sum(p[0:layer + 1], axis=0, keepdims=True) - p[0:1], 0.0)

    sig = _sigmoid(f_ref[:, cols])
    g = jnp.log(lb + (1.0 - lb) * sig)
    kin = (1.0 - lb) * (1.0 - sig)
    qa = _silu(q_ref[:, cols].astype(F32))
    vb = i_ref[:, cols]

    t_col = lax.broadcasted_iota(jnp.int32, (R, 1), 0)

    tri = tri_ref[...]
    g1, g2, g3 = _split3(g)
    b = (_dot(tri, g1) + _dot(tri, g2) + _dot(tri, g3)) * LOG2_E

    half = R // 2
    scores = [None, None]
    cross = None
    m = half
    while m >= 1:
        w = 2 * m
        if w >= SUBLANES:
            b3 = b.reshape(R // w, w, LANES)
            bref = jnp.broadcast_to(b3[:, m - 1:m, :], b3.shape).reshape(R, LANES)
        else:
            tl = t_col & (w - 1)
            bref = None
            for resid in range(w):
                shift = resid - (m - 1)
                cand = b if shift == 0 else pltpu.roll(b, shift % R, 0)
                bref = cand if bref is None else jnp.where(tl == resid, cand, bref)
        e = jnp.exp2(-jnp.abs(b - bref))
        qt = (qa * e).astype(BF16)
        kt = (kin * e).astype(BF16)
        if w == R:
            cross = _dot_nt(qt[half:R], kt[0:half])
        else:
            keep = lvl_ref[...] == (m.bit_length() - 1)
            for hh in range(2):
                part = _dot_nt(qt[hh * half:(hh + 1) * half], kt[hh * half:(hh + 1) * half])
                scores[hh] = jnp.where(keep, part, 0.0 if scores[hh] is None else scores[hh])
        m //= 2

    st = st_ref[head]
    o = _dot_nt((qa * jnp.exp2(b)).astype(BF16), st.astype(BF16))
    o = o + jnp.sum(qa * kin, axis=-1, keepdims=True) * vb.astype(F32)
    o_lo = o[0:half] + _dot(scores[0].astype(BF16), vb[0:half])
    o_hi = o[half:R] + _dot(scores[1].astype(BF16), vb[half:R]) + _dot(cross.astype(BF16), vb[0:half])
    o = jnp.concatenate([o_lo, o_hi], axis=0)

    b_end = b[R - 1:R, :]
    k_end = (kin * jnp.exp2(b_end - b)).astype(BF16)
    st_ref[head] = st * jnp.exp2(b_end) + _dot_tn(vb, k_end)

    on = o * lax.rsqrt(jnp.mean(o * o, axis=-1, keepdims=True) + RMS_EPS) * ng_ref[...]
    o_ref[:, cols] = (on * _silu(g_ref[:, cols].astype(F32))).astype(o_ref.dtype)


def _hgrn_body(q_ref, f_ref, i_ref, g_ref, lbl_ref, ng_ref, tri_ref, lvl_ref, o_ref, st_ref, *, rows, layer):
    @pl.when(pl.program_id(1) == 0)
    def _():
        st_ref[...] = jnp.zeros(st_ref.shape, F32)

    for head in range(st_ref.shape[0]):
        _hgrn_head(q_ref, f_ref, i_ref, g_ref, lbl_ref, ng_ref, tri_ref, lvl_ref, o_ref, st_ref,
                   rows=rows, layer=layer, head=head)


def _level_table(n):
    t = np.arange(n, dtype=np.int32)[:, None]
    s = np.arange(n, dtype=np.int32)[None, :]
    x = t ^ s
    hb = np.zeros((n, n), np.int32)
    for k in range(1, n.bit_length()):
        hb = hb + (x >= (1 << k)).astype(np.int32)
    return jnp.asarray(np.where(t > s, hb, -1).astype(np.int32))


def _hgrn(hqig, hf, lb_logits, norm_g, layer, batch, seq):
    T = hf.shape[0]
    H = HGRN_HEADS
    R = HGRN_ROWS
    W = H * HGRN_DIM
    nc = seq // R
    L = lb_logits.shape[0]
    blk = lambda off: pl.BlockSpec((R, W), lambda b, c: (b * nc + c, off))
    const = lambda b, c: (0, 0)
    tri = jnp.asarray(np.tril(np.ones((R, R), np.float32)), dtype=BF16)
    return pl.pallas_call(
        functools.partial(_hgrn_body, rows=R, layer=layer),
        grid=(batch, nc),
        in_specs=[blk(0), blk(0), blk(1), blk(2),
                  pl.BlockSpec((L, W), const), pl.BlockSpec((1, LANES), const),
                  pl.BlockSpec((R, R), const), pl.BlockSpec((R // 2, R // 2), const)],
        out_specs=blk(0),
        out_shape=jax.ShapeDtypeStruct((T, W), BF16),
        scratch_shapes=[pltpu.VMEM((H, HGRN_DIM, HGRN_DIM), F32)],
        compiler_params=_params(2, VMEM_LIMIT),
        name="hgrn2",
    )(hqig, hf, hqig, hqig, lb_logits.astype(F32), norm_g.astype(F32).reshape(1, LANES), tri,
      _level_table(R // 2))


def _conv_body(ab_ref, w_ref, bdw_ref, lg_ref, lb_ref, o_ref, ubuf, cbuf, ush, *, rows, width, chans):
    R, W, C, H = rows, width, chans, CONV_HALO
    c = pl.program_id(1)

    @pl.when(c == 0)
    def _():
        ubuf[0:H, :] = jnp.zeros((H, C), F32)

    a = ab_ref[:, 0:C].astype(F32)
    gate = ab_ref[:, C:2 * C].astype(F32)
    ubuf[H:H + R, :] = a * _sigmoid(gate)

    n_sh = ush.shape[1]
    for s in range(1, SUBLANES):
        ush[s - 1] = ubuf[s:s + n_sh, :]

    rsub = 64
    for lb in range(0, C, LANES):
        for rb in range(0, R, rsub):
            acc = jnp.zeros((rsub, LANES), F32)
            for j in range(W):
                start = H - (W - 1) + j + rb
                s = start % SUBLANES
                a0 = start - s
                if s == 0:
                    u = ubuf[a0:a0 + rsub, lb:lb + LANES]
                else:
                    u = ush[s - 1, a0:a0 + rsub, lb:lb + LANES]
                acc = acc + w_ref[j:j + 1, lb:lb + LANES] * u
            cbuf[rb:rb + rsub, lb:lb + LANES] = acc

    uf = cbuf[...] + bdw_ref[...]
    mu = jnp.mean(uf, axis=-1, keepdims=True)
    d = uf - mu
    var = jnp.mean(d * d, axis=-1, keepdims=True)
    y = d * lax.rsqrt(var + LN_EPS) * lg_ref[...] + lb_ref[...]
    o_ref[...] = _silu(y).astype(o_ref.dtype)
    ubuf[0:H, :] = ubuf[R:R + H, :]


def _conv(cab, conv_w, conv_b, ln_g, ln_b, batch, seq):
    T = cab.shape[0]
    C = cab.shape[1] // 2
    W = conv_w.shape[0]
    R = CONV_ROWS
    nc = seq // R
    vec = pl.BlockSpec((1, C), lambda b, c: (0, 0))
    return pl.pallas_call(
        functools.partial(_conv_body, rows=R, width=W, chans=C),
        grid=(batch, nc),
        in_specs=[pl.BlockSpec((R, 2 * C), lambda b, c: (b * nc + c, 0)),
                  pl.BlockSpec((W, C), lambda b, c: (0, 0)), vec, vec, vec],
        out_specs=pl.BlockSpec((R, C), lambda b, c: (b * nc + c, 0)),
        out_shape=jax.ShapeDtypeStruct((T, C), BF16),
        scratch_shapes=[pltpu.VMEM((R + CONV_HALO, C), F32), pltpu.VMEM((R, C), F32),
                        pltpu.VMEM((SUBLANES - 1, R + CONV_HALO - SUBLANES, C), F32)],
        compiler_params=_params(2, VMEM_LIMIT),
        name="conformer_conv",
    )(cab, conv_w.astype(F32), conv_b.astype(F32).reshape(1, C), ln_g.astype(F32).reshape(1, C),
      ln_b.astype(F32).reshape(1, C))


def _merge_body(att_ref, rec_ref, cv_ref, gl_ref, bg_ref, x_ref, wa_ref, wh_ref, wc_ref, wo_ref,
                fg_ref, wr_ref, br_ref, before_ref, xo_ref, route_ref, cnt_out_ref, mg_ref, cnt_ref, *,
                n_groups, per_group):
    D = x_ref.shape[1]
    att, rec, cv = att_ref[...], rec_ref[...], cv_ref[...]
    cw = 256
    for c in range(0, D, cw):
        def gate(k):
            return _sigmoid(gl_ref[:, k * D + c:k * D + c + cw].astype(F32) + bg_ref[:, k * D + c:k * D + c + cw])
        merged = (gate(0) * _dot(att, wa_ref[:, c:c + cw]) + gate(1) * _dot(rec, wh_ref[:, c:c + cw])
                  + gate(2) * _dot(cv, wc_ref[:, c:c + cw]))
        mg_ref[:, c:c + cw] = merged.astype(BF16)
    xn = x_ref[...] + _dot(mg_ref[...], wo_ref[...])
    xo_ref[...] = xn

    h = xn * lax.rsqrt(jnp.mean(xn * xn, axis=-1, keepdims=True) + RMS_EPS) * fg_ref[...]
    h_hi = h.astype(BF16)
    h_lo = (h - h_hi.astype(F32)).astype(BF16)
    wr = wr_ref[...]
    w_hi = wr.astype(BF16)
    w_lo = (wr - w_hi.astype(F32)).astype(BF16)
    logits = _dot(h_hi, w_hi) + _dot(h_hi, w_lo) + _dot(h_lo, w_hi) + br_ref[...]

    G, E = n_groups, per_group
    lane = lax.broadcasted_iota(jnp.int32, logits.shape, 1)
    big = jnp.int32(LANES)
    is_c = lane < G
    cl = jnp.where(is_c, logits, NEG_INF)
    cm = jnp.max(cl, axis=-1, keepdims=True)
    grp = jnp.min(jnp.where(cl == cm, lane, big), axis=-1, keepdims=True)
    se = jnp.sum(jnp.exp(jnp.where(is_c, logits - cm, NEG_INF)), axis=-1, keepdims=True)
    p_top = 1.0 / se
    lo = G + grp * E
    fl = jnp.where((lane >= lo) & (lane < lo + E), logits, NEG_INF)
    m1 = jnp.max(fl, axis=-1, keepdims=True)
    i1 = jnp.min(jnp.where(fl == m1, lane, big), axis=-1, keepdims=True)
    fl2 = jnp.where(lane == i1, NEG_INF, fl)
    m2 = jnp.max(fl2, axis=-1, keepdims=True)
    i2 = jnp.min(jnp.where(fl2 == m2, lane, big), axis=-1, keepdims=True)
    t = jnp.exp(m2 - m1)
    w1 = p_top / (1.0 + t)
    w2 = p_top * t / (1.0 + t)
    e1 = (i1 - G).astype(F32)
    e2 = (i2 - G).astype(F32)

    @pl.when(pl.program_id(0) == 0)
    def _():
        cnt_ref[...] = jnp.zeros(cnt_ref.shape, F32)

    hit1 = lane == (i1 - G)
    hit2 = lane == (i2 - G)
    hits = jnp.where(hit1 | hit2, 1.0, 0.0)
    prefix = _dot(before_ref[...], hits.astype(BF16)) + cnt_ref[...]
    rank1 = jnp.sum(jnp.where(hit1, prefix, 0.0), axis=-1, keepdims=True)
    rank2 = jnp.sum(jnp.where(hit2, prefix, 0.0), axis=-1, keepdims=True)
    cnt = cnt_ref[...] + jnp.sum(hits, axis=0, keepdims=True)
    cnt_ref[...] = cnt
    cnt_out_ref[...] = cnt

    route_ref[...] = jnp.where(lane == 0, e1, jnp.where(lane == 1, e2,
                               jnp.where(lane == 2, w1, jnp.where(lane == 3, w2,
                               jnp.where(lane == 4, rank1, jnp.where(lane == 5, rank2, 0.0))))))


def _merge(att, rec, cv, gl, b_gate, x2d, wa, wh, wc, wo, ffn_g, w_route, b_route, layer):
    T, D = x2d.shape
    tm = MERGE_TILE
    row = lambda i: (i, 0)
    const2 = lambda i: (0, 0)
    wsel = lambda i: (layer, 0, 0)
    W = att.shape[1]
    before = jnp.asarray(np.tril(np.ones((tm, tm), np.float32), -1), dtype=BF16)
    return pl.pallas_call(
        functools.partial(_merge_body, n_groups=N_GROUPS, per_group=EXPERTS_PER_GROUP),
        grid=(T // tm,),
        in_specs=[pl.BlockSpec((tm, W), row), pl.BlockSpec((tm, W), row), pl.BlockSpec((tm, W), row),
                  pl.BlockSpec((tm, 3 * D), row), pl.BlockSpec((1, 3 * D), const2),
                  pl.BlockSpec((tm, D), row),
                  pl.BlockSpec((None, W, D), wsel), pl.BlockSpec((None, W, D), wsel),
                  pl.BlockSpec((None, W, D), wsel), pl.BlockSpec((None, D, D), wsel),
                  pl.BlockSpec((1, D), const2), pl.BlockSpec((D, LANES), const2),
                  pl.BlockSpec((1, LANES), const2), pl.BlockSpec((tm, tm), const2)],
        out_specs=[pl.BlockSpec((tm, D), row), pl.BlockSpec((tm, LANES), row),
                   pl.BlockSpec((1, LANES), const2)],
        out_shape=[jax.ShapeDtypeStruct((T, D), F32), jax.ShapeDtypeStruct((T, LANES), F32),
                   jax.ShapeDtypeStruct((1, LANES), F32)],
        scratch_shapes=[pltpu.VMEM((tm, D), BF16), pltpu.VMEM((1, LANES), F32)],
        compiler_params=_params(1, VMEM_LIMIT),
        name="merge_route",
    )(att, rec, cv, gl, b_gate.astype(F32).reshape(1, 3 * D), x2d, wa, wh, wc, wo,
      ffn_g.astype(F32).reshape(1, D), w_route, b_route, before)


def _bulk_wait(view, sem):
    pltpu.make_async_copy(view, view, sem).wait()


def _dispatch_body(ps_ref, pn_ref, nu_ref, pos_ref, x_ref, fg_ref, xs_hbm, hbuf, zbuf, sem, zsem, *,
                   tile, n_tiles, n_experts, n_sorted_tiles):
    i = pl.program_id(0)
    slot = i % 2
    ztile = zbuf.shape[0]

    def pad_copies(e):
        out = []
        for k in range(ztile.bit_length() - 1):
            first = ps_ref[e] + ((pn_ref[e] >> (k + 1)) << (k + 1))
            copy = pltpu.make_async_copy(zbuf.at[pl.ds(0, 1 << k)], xs_hbm.at[pl.ds(first, 1 << k)], zsem)
            out.append((((pn_ref[e] >> k) & 1) == 1, copy))
        return out

    def tail_copy(t):
        return pltpu.make_async_copy(zbuf, xs_hbm.at[pl.ds(t * ztile, ztile)], zsem)

    def wait_slot(s):
        _bulk_wait(hbuf.at[s], sem.at[s])
        _bulk_wait(hbuf.at[s], sem.at[s])

    @pl.when(i == 0)
    def _():
        zbuf[...] = jnp.zeros(zbuf.shape, F32)
        for e in range(n_experts):
            for bit_set, copy in pad_copies(e):
                pl.when(bit_set)(copy.start)

        def start_tail(t, carry):
            tail_copy(t).start()
            return carry
        lax.fori_loop(nu_ref[0], n_sorted_tiles, start_tail, 0)

    def scatter_tile(s):
        @pl.when(i >= 2)
        def _():
            wait_slot(s)

        x = x_ref[...]
        h = x * lax.rsqrt(jnp.mean(x * x, axis=-1, keepdims=True) + RMS_EPS) * fg_ref[...]
        hbuf[s] = h.reshape(h.shape[0], 1, h.shape[1])

        def issue(g, carry):
            for k in range(DMA_UNROLL):
                r = g * DMA_UNROLL + k
                src = hbuf.at[s, pl.ds(r, 1)]
                pltpu.make_async_copy(src, xs_hbm.at[pl.ds(pos_ref[0, 0, r], 1)],
                                      sem.at[s]).start(priority=0)
                pltpu.make_async_copy(src, xs_hbm.at[pl.ds(pos_ref[0, 0, tile + r], 1)],
                                      sem.at[s]).start(priority=1)
            return carry

        lax.fori_loop(0, tile // DMA_UNROLL, issue, 0)

    for s in range(2):
        pl.when(slot == s)(functools.partial(scatter_tile, s))

    @pl.when(i == n_tiles - 1)
    def _():
        wait_slot((n_tiles - 1) % 2)
        if n_tiles >= 2:
            wait_slot(n_tiles % 2)
        for e in range(n_experts):
            for bit_set, copy in pad_copies(e):
                pl.when(bit_set)(copy.wait)

        def done_tail(t, carry):
            tail_copy(t).wait()
            return carry
        lax.fori_loop(nu_ref[0], n_sorted_tiles, done_tail, 0)


def _dispatch(x2d, ffn_g, pos_tiles, pad_start, pad_len, n_used, cap):
    T, D = x2d.shape
    tile = TOKEN_TILE
    nt = T // tile
    n_experts = pad_start.shape[0]
    grid_spec = pltpu.PrefetchScalarGridSpec(
        num_scalar_prefetch=3,
        grid=(nt,),
        in_specs=[pl.BlockSpec((1, 1, 2 * tile), lambda i, ps, pn, nu: (i, 0, 0), memory_space=pltpu.SMEM),
                  pl.BlockSpec((tile, D), lambda i, ps, pn, nu: (i, 0)),
                  pl.BlockSpec((1, D), lambda i, ps, pn, nu: (0, 0))],
        out_specs=pl.BlockSpec(memory_space=pl.ANY),
        scratch_shapes=[pltpu.VMEM((2, tile, 1, D), F32), pltpu.VMEM((EXPERT_TILE, 1, D), F32),
                        pltpu.SemaphoreType.DMA((2,)), pltpu.SemaphoreType.DMA(())],
    )
    return pl.pallas_call(
        functools.partial(_dispatch_body, tile=tile, n_tiles=nt, n_experts=n_experts,
                          n_sorted_tiles=cap // EXPERT_TILE),
        grid_spec=grid_spec,
        out_shape=jax.ShapeDtypeStruct((cap, 1, D), F32),
        compiler_params=_params(1, VMEM_LIMIT),
        name="moe_dispatch",
    )(pad_start, pad_len, n_used, pos_tiles, x2d, ffn_g.astype(F32).reshape(1, D))


def _expert_body(te_ref, nu_ref, nx_ref, sl_ref, x_ref, win_hbm, wout_hbm, y_ref, win_bf, wout_bf, x2_ref,
                 win_st, wout_st, wsem, *, layer):
    i = pl.program_id(0)
    ff = wout_bf.shape[0]
    used = i < nu_ref[0]
    new_expert = jnp.logical_or(i == 0, te_ref[i] != te_ref[jnp.maximum(i - 1, 0)])

    def weight_copies(e, slot):
        return (pltpu.make_async_copy(win_hbm.at[layer, e], win_st.at[slot], wsem.at[0, slot]),
                pltpu.make_async_copy(wout_hbm.at[layer, e], wout_st.at[slot], wsem.at[1, slot]))

    @pl.when(i == 0)
    def _():
        for c in weight_copies(te_ref[0], 0):
            c.start()

    def begin_run(s):
        for c in weight_copies(te_ref[i], s):
            c.wait()
        win_bf[...] = win_st[s].astype(BF16)
        wout_bf[...] = wout_st[s].astype(BF16)

        @pl.when(nx_ref[i] >= 0)
        def _():
            for c in weight_copies(nx_ref[i], 1 - s):
                c.start()

    for s in range(2):
        pl.when(jnp.logical_and(jnp.logical_and(used, new_expert), sl_ref[i] == s))(
            functools.partial(begin_run, s))

    @pl.when(used)
    def _():
        x2_ref[...] = x_ref[...].reshape(x2_ref.shape)
        gu = _dot(x2_ref[...].astype(BF16), win_bf[...])
        act = (_silu(gu[:, 0:ff]) * gu[:, ff:2 * ff]).astype(BF16)
        y = _dot(act, wout_bf[...])
        y_ref[...] = y.reshape(y_ref.shape)

    @pl.when(jnp.logical_not(used))
    def _():
        y_ref[...] = jnp.zeros(y_ref.shape, F32)


def _experts(x_sorted, w_exp_in, w_exp_out, tile_expert, n_used, next_expert, run_slot, layer):
    cap, _, D = x_sorted.shape
    tile = EXPERT_TILE
    nt = cap // tile
    ff = w_exp_out.shape[2]
    rows = lambda i, te, nu, nx, sl: (i, 0, 0)
    grid_spec = pltpu.PrefetchScalarGridSpec(
        num_scalar_prefetch=4,
        grid=(nt,),
        in_specs=[pl.BlockSpec((tile, 1, D), rows),
                  pl.BlockSpec(memory_space=pl.ANY), pl.BlockSpec(memory_space=pl.ANY)],
        out_specs=pl.BlockSpec((tile, 1, D), rows),
        scratch_shapes=[pltpu.VMEM((D, 2 * ff), BF16), pltpu.VMEM((ff, D), BF16),
                        pltpu.VMEM((tile, D), F32),
                        pltpu.VMEM((2, D, 2 * ff), F32), pltpu.VMEM((2, ff, D), F32),
                        pltpu.SemaphoreType.DMA((2, 2))],
    )
    return pl.pallas_call(
        functools.partial(_expert_body, layer=layer),
        grid_spec=grid_spec,
        out_shape=jax.ShapeDtypeStruct((cap, 1, D), F32),
        compiler_params=_params(1, VMEM_LIMIT),
        name="expert_ffn",
    )(tile_expert, n_used, next_expert, run_slot, x_sorted, w_exp_in, w_exp_out)


def _combine_body(pos0_ref, posn_ref, y_hbm, x_ref, route_ref, o_ref, ybuf, y2_ref, sem, *, tile, n_tiles):
    i = pl.program_id(0)
    slot = i % 2

    def gather(pos_ref, s):
        def issue(g, carry):
            for k in range(DMA_UNROLL):
                r = g * DMA_UNROLL + k
                pltpu.make_async_copy(y_hbm.at[pl.ds(pos_ref[0, 0, r], 1)],
                                      ybuf.at[s, pl.ds(r, 1)], sem.at[s]).start(priority=k % 2)
            return carry
        lax.fori_loop(0, 2 * tile // DMA_UNROLL, issue, 0)

    @pl.when(i == 0)
    def _():
        gather(pos0_ref, 0)

    def combine_tile(s):
        @pl.when(i + 1 < n_tiles)
        def _():
            gather(posn_ref, 1 - s)

        _bulk_wait(ybuf.at[s], sem.at[s])
        y2_ref[...] = ybuf[s].reshape(y2_ref.shape)
        route = route_ref[...]
        o_ref[...] = (x_ref[...] + route[:, 2:3] * y2_ref[0:tile, :]
                      + route[:, 3:4] * y2_ref[tile:2 * tile, :])

    for s in range(2):
        pl.when(slot == s)(functools.partial(combine_tile, s))


def _combine(x2d, route, y_sorted, pos_tiles):
    T, D = x2d.shape
    tile = TOKEN_TILE
    nt = T // tile
    return pl.pallas_call(
        functools.partial(_combine_body, tile=tile, n_tiles=nt),
        grid=(nt,),
        in_specs=[pl.BlockSpec((1, 1, 2 * tile), lambda i: (0, 0, 0), memory_space=pltpu.SMEM),
                  pl.BlockSpec((1, 1, 2 * tile), lambda i: (jnp.minimum(i + 1, nt - 1), 0, 0),
                               memory_space=pltpu.SMEM),
                  pl.BlockSpec(memory_space=pl.ANY),
                  pl.BlockSpec((tile, D), lambda i: (i, 0)),
                  pl.BlockSpec((tile, LANES), lambda i: (i, 0))],
        out_specs=pl.BlockSpec((tile, D), lambda i: (i, 0)),
        out_shape=jax.ShapeDtypeStruct((T, D), F32),
        scratch_shapes=[pltpu.VMEM((2, 2 * tile, 1, D), F32), pltpu.VMEM((2 * tile, D), F32),
                        pltpu.SemaphoreType.DMA((2,))],
        compiler_params=_params(1, VMEM_LIMIT),
        name="moe_combine",
    )(pos_tiles, pos_tiles, y_sorted, x2d, route)


def _positions_body(route_ref, offs_ref, pos_ref, *, tile):
    r = route_ref[...]
    lane = lax.broadcasted_iota(jnp.int32, r.shape, 1)
    lane_f = lane.astype(F32)
    offs = offs_ref[...]
    pos1 = jnp.sum(jnp.where(lane_f == r[:, 0:1], offs, 0.0), axis=-1, keepdims=True) + r[:, 4:5]
    pos2 = jnp.sum(jnp.where(lane_f == r[:, 1:2], offs, 0.0), axis=-1, keepdims=True) + r[:, 5:6]
    both = jnp.where(lane == 0, pos1, jnp.where(lane == 1, pos2, 0.0))
    for k in range(r.shape[0] // tile):
        t = both[k * tile:(k + 1) * tile, :].T
        pos_ref[k, :, 0:tile] = t[0:1, :].astype(jnp.int32)
        pos_ref[k, :, tile:2 * tile] = t[1:2, :].astype(jnp.int32)


def _positions(route, offs_row):
    T = route.shape[0]
    tile = TOKEN_TILE
    step = 4 * tile
    return pl.pallas_call(
        functools.partial(_positions_body, tile=tile),
        grid=(T // step,),
        in_specs=[pl.BlockSpec((step, LANES), lambda i: (i, 0)),
                  pl.BlockSpec((1, LANES), lambda i: (0, 0))],
        out_specs=pl.BlockSpec((step // tile, 1, 2 * tile), lambda i: (i, 0, 0)),
        out_shape=jax.ShapeDtypeStruct((T // tile, 1, 2 * tile), jnp.int32),
        compiler_params=_params(1, VMEM_LIMIT),
        name="moe_positions",
    )(route, offs_row)


def _dispatch_plan(route, counts, n_experts, tile, cap):
    cnt = counts[0, :n_experts].astype(jnp.int32)
    padded = ((cnt + tile - 1) // tile) * tile
    ends = jnp.cumsum(padded)
    offs = ends - padded
    nt = cap // tile
    n_used = (ends[-1] // tile).astype(jnp.int32)
    first_row = jnp.minimum(jnp.arange(nt, dtype=jnp.int32), n_used - 1) * tile
    te = jnp.sum((ends[None, :] <= first_row[:, None]).astype(jnp.int32), axis=1)
    te = jnp.minimum(te, n_experts - 1).astype(jnp.int32)
    run_start = jnp.concatenate([jnp.ones((1,), bool), te[1:] != te[:-1]])
    run_slot = ((jnp.cumsum(run_start.astype(jnp.int32)) - 1) % 2).astype(jnp.int32)
    ids = jnp.arange(n_experts, dtype=jnp.int32)
    later = (ids[None, :] > ids[:, None]) & (cnt > 0)[None, :]
    following = jnp.min(jnp.where(later, ids[None, :], n_experts), axis=1)
    following = jnp.where(following >= n_experts, -1, following).astype(jnp.int32)
    next_expert = jnp.sum(jnp.where(te[:, None] == ids[None, :], following[None, :], 0), axis=1).astype(jnp.int32)
    offs_row = jnp.pad(offs.astype(F32), (0, LANES - n_experts)).reshape(1, LANES)
    pos_tiles = _positions(route, offs_row)
    return (te, n_used.reshape(1), pos_tiles, (offs + cnt).astype(jnp.int32), (padded - cnt).astype(jnp.int32),
            next_expert, run_slot)


def kernel(x, mix_norm_g, w_in, b_gate, q_norm_g, k_norm_g, lb_logits, hgrn_norm_g, conv_w, conv_b,
           conv_ln_g, conv_ln_b, w_att_o, w_hgrn_o, w_conv_o, w_out, ffn_norm_g, w_coarse, b_coarse,
           w_fine, b_fine, w_exp_in, w_exp_out):
    B, S, D = x.shape
    L = w_in.shape[0]
    T = B * S
    n_experts = w_exp_in.shape[1]
    cap = 2 * T + n_experts * EXPERT_TILE

    w_in_bf = w_in.astype(BF16)
    wa_bf, wh_bf, wc_bf, wo_bf = (w.astype(BF16) for w in (w_att_o, w_hgrn_o, w_conv_o, w_out))
    pad = LANES - N_GROUPS - n_experts
    w_route = jnp.concatenate([w_coarse, w_fine, jnp.zeros((L, D, pad), F32)], axis=-1).astype(F32)
    b_route = jnp.concatenate([b_coarse, b_fine, jnp.zeros((L, pad), F32)], axis=-1).astype(F32)

    x2d = x.reshape(T, D)
    for l in range(L):
        qkv, hf, hqig, cab, gl = _inproj(x2d, mix_norm_g[l], w_in_bf, l)
        att = _attention(qkv, q_norm_g[l], k_norm_g[l], B, S)
        rec = _hgrn(hqig, hf, lb_logits, hgrn_norm_g[l], l, B, S)
        cv = _conv(cab, conv_w[l], conv_b[l], conv_ln_g[l], conv_ln_b[l], B, S)
        x_mid, route, counts = _merge(att, rec, cv, gl, b_gate[l], x2d, wa_bf, wh_bf, wc_bf, wo_bf,
                                      ffn_norm_g[l], w_route[l], b_route[l].reshape(1, LANES), l)
        te, n_used, pos_tiles, pad_start, pad_len, next_expert, run_slot = _dispatch_plan(
            route, counts, n_experts, EXPERT_TILE, cap)
        x_sorted = _dispatch(x_mid, ffn_norm_g[l], pos_tiles, pad_start, pad_len, n_used, cap)
        y_sorted = _experts(x_sorted, w_exp_in, w_exp_out, te, n_used, next_expert, run_slot, l)
        x2d = _combine(x_mid, route, y_sorted, pos_tiles)
    return x2d.reshape(B, S, D)
```

```python
import functools

import jax
import jax.numpy as jnp
import numpy as np
from jax import lax
from jax.experimental import pallas as pl
from jax.experimental.pallas import tpu as pltpu

F32 = jnp.float32
BF16 = jnp.bfloat16

LANES = 128
SUBLANES = 8
RMS_EPS = 1e-6
LN_EPS = 1e-5
NEG_INF = -1e30
LOG2_E = 1.4426950408889634

ATT_HEADS = 8
ATT_HEAD_DIM = 64
MOBA_BLOCK = 256
MOBA_TOPK = 3
HGRN_HEADS = 4
HGRN_DIM = 128
HGRN_ROWS = 256
HGRN_STEP_CHUNKS = 2
CONV_WIDTH = 31
CONV_ROWS = 512
CONV_HALO = 32
N_GROUPS = 4
EXPERTS_PER_GROUP = 8
EXPERT_TILE = 512
TOKEN_TILE = 512
INPROJ_TILE = 512
MERGE_TILE = 512
DMA_UNROLL = 8
VMEM_LIMIT = 56 * 1024 * 1024


def _params(n_axes, vmem=None):
    return pltpu.CompilerParams(dimension_semantics=("arbitrary",) * n_axes,
                                vmem_limit_bytes=vmem)


def _dot(a, b):
    return jnp.dot(a, b, preferred_element_type=F32)


def _dot_nt(a, b):
    return lax.dot_general(a, b, (((1,), (1,)), ((), ())), preferred_element_type=F32)


def _dot_tn(a, b):
    return lax.dot_general(a, b, (((0,), (0,)), ((), ())), preferred_element_type=F32)


def _split3(x):
    x1 = x.astype(BF16)
    r1 = x - x1.astype(F32)
    x2 = r1.astype(BF16)
    x3 = (r1 - x2.astype(F32)).astype(BF16)
    return x1, x2, x3


def _sigmoid(x):
    return 1.0 / (1.0 + jnp.exp(-x))


def _silu(x):
    return x * _sigmoid(x)


def _inproj_body(x_ref, g_ref, w_ref, qkv_ref, hf_ref, hqig_ref, cv_ref, gl_ref, *, segs, chunk):
    x = x_ref[...]
    ms = jnp.mean(x * x, axis=-1, keepdims=True)
    h = (x * lax.rsqrt(ms + RMS_EPS) * g_ref[...]).astype(BF16)
    refs = (qkv_ref, hf_ref, hqig_ref, cv_ref, gl_ref)
    for ridx, dst0, src0, width in segs:
        ref = refs[ridx]
        for c in range(0, width, chunk):
            y = _dot(h, w_ref[:, src0 + c:src0 + c + chunk])
            ref[:, dst0 + c:dst0 + c + chunk] = y.astype(ref.dtype)


def _inproj(x2d, norm_g, w_in_bf, layer):
    T, D = x2d.shape
    A = ATT_HEADS * ATT_HEAD_DIM
    K = HGRN_HEADS * HGRN_DIM
    C = A
    segs = ((0, 0, 0, 3 * A),
            (2, 0, 3 * A, K),
            (1, 0, 3 * A + K, K),
            (2, K, 3 * A + 2 * K, K),
            (2, 2 * K, 3 * A + 3 * K, K),
            (3, 0, 3 * A + 4 * K, 2 * C),
            (4, 0, 3 * A + 4 * K + 2 * C, 3 * D))
    ncols = w_in_bf.shape[-1]
    tm = INPROJ_TILE
    row = lambda i: (i, 0)
    return pl.pallas_call(
        functools.partial(_inproj_body, segs=segs, chunk=512),
        grid=(T // tm,),
        in_specs=[pl.BlockSpec((tm, D), row),
                  pl.BlockSpec((1, D), lambda i: (0, 0)),
                  pl.BlockSpec((None, D, ncols), lambda i: (layer, 0, 0))],
        out_specs=[pl.BlockSpec((tm, 3 * A), row), pl.BlockSpec((tm, K), row),
                   pl.BlockSpec((tm, 3 * K), row), pl.BlockSpec((tm, 2 * C), row),
                   pl.BlockSpec((tm, 3 * D), row)],
        out_shape=[jax.ShapeDtypeStruct((T, 3 * A), BF16), jax.ShapeDtypeStruct((T, K), F32),
                   jax.ShapeDtypeStruct((T, 3 * K), BF16), jax.ShapeDtypeStruct((T, 2 * C), BF16),
                   jax.ShapeDtypeStruct((T, 3 * D), BF16)],
        compiler_params=_params(1, VMEM_LIMIT),
        name="inproj",
    )(x2d, norm_g.reshape(1, D), w_in_bf)


def _attn_body(q_ref, k_ref, v_ref, gq_ref, gk_ref, o_ref, kb_ref, vt_ref, s_ref, *, seq, blk, topk):
    nb = seq // blk
    dh = ATT_HEAD_DIM
    dh_sh = dh.bit_length() - 1
    scale = dh ** -0.5 * LOG2_E
    lane = lax.broadcasted_iota(jnp.int32, (1, LANES), 1)
    hr = lax.broadcasted_iota(jnp.int32, (LANES, LANES), 0) >> dh_sh
    hc = lax.broadcasted_iota(jnp.int32, (LANES, LANES), 1) >> dh_sh
    same_head = jnp.where(hr == hc, 1.0, 0.0).astype(BF16)

    def head_norm(x, g):
        x2 = x * x
        hi = x2.astype(BF16)
        lo = (x2 - hi.astype(F32)).astype(BF16)
        ssq = _dot(hi, same_head) + _dot(lo, same_head)
        return x * lax.rsqrt(ssq * (1.0 / dh) + RMS_EPS) * g

    kmeans = []
    for j in range(nb):
        kn = head_norm(k_ref[j * blk:(j + 1) * blk, :].astype(F32), gk_ref[...])
        kb_ref[j * blk:(j + 1) * blk, :] = kn.astype(BF16)
        kmeans.append(jnp.mean(kn, axis=0, keepdims=True))
        vt_ref[:, j * blk:(j + 1) * blk] = v_ref[j * blk:(j + 1) * blk, :].astype(F32).T.astype(BF16)
    kmean = jnp.concatenate(kmeans, axis=0)
    km_hi = kmean.astype(BF16)
    km_lo = (kmean - km_hi.astype(F32)).astype(BF16)
    first_head = lane < dh
    first_head_t = lax.broadcasted_iota(jnp.int32, (LANES, 1), 0) < dh
    blk_n = lax.broadcasted_iota(jnp.int32, (nb, 1), 0)

    key_l = lax.broadcasted_iota(jnp.int32, (blk, 2 * blk), 0)
    qry_l = lax.broadcasted_iota(jnp.int32, (blk, 2 * blk), 1) & (blk - 1)
    causal = key_l <= qry_l

    for i in range(nb):
        rows = head_norm(q_ref[i * blk:(i + 1) * blk, :].astype(F32), gq_ref[...])
        qi = jnp.concatenate([jnp.where(first_head, rows, 0.0), jnp.where(first_head, 0.0, rows)], axis=0)
        qs = (qi * scale).astype(BF16)

        sel = None
        if i > topk:
            q_hi = qi.astype(BF16)
            q_lo = (qi - q_hi.astype(F32)).astype(BF16)
            gate = _dot_nt(km_hi, q_hi) + _dot_nt(km_lo, q_hi) + _dot_nt(km_hi, q_lo)
            valid = blk_n < i
            gm = jnp.where(valid, gate, NEG_INF)
            rank = jnp.zeros(gm.shape, F32)
            for r in range(1, nb):
                gr = pltpu.roll(gm, r, 0)
                ahead = (gr > gm) | ((gr == gm) & (blk_n >= r))
                rank = rank + jnp.where(ahead, 1.0, 0.0)
            sel = jnp.where(valid & (rank < topk), 1.0, 0.0)

        m = None
        for j in range(i + 1):
            s = _dot_nt(kb_ref[j * blk:(j + 1) * blk, :], qs)
            if j == i:
                s = jnp.where(causal, s, NEG_INF)
            elif sel is not None:
                s = jnp.where(sel[j:j + 1, :] > 0.5, s, NEG_INF)
            s_ref[j * blk:(j + 1) * blk, :] = s
            mj = jnp.max(s, axis=0, keepdims=True)
            m = mj if m is None else jnp.maximum(m, mj)

        l = None
        acc = None
        for j in range(i + 1):
            p = jnp.exp2(s_ref[j * blk:(j + 1) * blk, :] - m)
            lj = jnp.sum(p, axis=0, keepdims=True)
            aj = _dot(vt_ref[:, j * blk:(j + 1) * blk], p.astype(BF16))
            l = lj if l is None else l + lj
            acc = aj if acc is None else acc + aj

        ot = acc / l
        merged = jnp.where(first_head_t, ot[:, 0:blk], ot[:, blk:2 * blk])
        o_ref[i * blk:(i + 1) * blk, :] = merged.T.astype(o_ref.dtype)


def _attention(qkv, q_norm_g, k_norm_g, batch, seq):
    T = qkv.shape[0]
    A = ATT_HEADS * ATT_HEAD_DIM
    npair = A // LANES
    reps = LANES // ATT_HEAD_DIM
    gq = jnp.tile(q_norm_g.astype(F32), reps).reshape(1, LANES)
    gk = jnp.tile(k_norm_g.astype(F32), reps).reshape(1, LANES)
    blk_spec = lambda off: pl.BlockSpec((seq, LANES), lambda b, p: (b, off + p))
    vec = pl.BlockSpec((1, LANES), lambda b, p: (0, 0))
    return pl.pallas_call(
        functools.partial(_attn_body, seq=seq, blk=MOBA_BLOCK, topk=MOBA_TOPK),
        grid=(batch, npair),
        in_specs=[blk_spec(0), blk_spec(npair), blk_spec(2 * npair), vec, vec],
        out_specs=pl.BlockSpec((seq, LANES), lambda b, p: (b, p)),
        out_shape=jax.ShapeDtypeStruct((T, A), BF16),
        scratch_shapes=[pltpu.VMEM((seq, LANES), BF16), pltpu.VMEM((LANES, seq), BF16),
                        pltpu.VMEM((seq, 2 * MOBA_BLOCK), F32)],
        compiler_params=_params(2, VMEM_LIMIT),
        name="moba_attn",
    )(qkv, qkv, qkv, gq, gk)


def _hgrn_head(q_ref, f_ref, i_ref, g_ref, lbl_ref, ng_ref, tri_ref, lvl_ref, o_ref, st_ref, *,
               rows, layer, head, row0):
    R = rows
    cols = slice(head * HGRN_DIM, (head + 1) * HGRN_DIM)
    rws = slice(row0, row0 + R)

    lbl = lbl_ref[:, cols]
    e = jnp.exp(lbl - jnp.max(lbl, axis=0, keepdims=True))
    p = e / jnp.sum(e, axis=0, keepdims=True)
    lb = jnp.maximum(jnp.sum(p[0:layer + 1], axis=0, keepdims=True) - p[0:1], 0.0)

    sig = _sigmoid(f_ref[rws, cols])
    g = jnp.log(lb + (1.0 - lb) * sig)
    kin = (1.0 - lb) * (1.0 - sig)
    qa = _silu(q_ref[rws, cols].astype(F32))
    vb = i_ref[rws, cols]

    t_col = lax.broadcasted_iota(jnp.int32, (R, 1), 0)

    tri = tri_ref[...]
    g1, g2, g3 = _split3(g)
    b = (_dot(tri, g1) + _dot(tri, g2) + _dot(tri, g3)) * LOG2_E

    half = R // 2
    scores = [None, None]
    cross = None
    m = half
    while m >= 1:
        w = 2 * m
        if w >= SUBLANES:
            b3 = b.reshape(R // w, w, LANES)
            bref = jnp.broadcast_to(b3[:, m - 1:m, :], b3.shape).reshape(R, LANES)
        else:
            tl = t_col & (w - 1)
            bref = None
            for resid in range(w):
                shift = resid - (m - 1)
                cand = b if shift == 0 else pltpu.roll(b, shift % R, 0)
                bref = cand if bref is None else jnp.where(tl == resid, cand, bref)
        e = jnp.exp2(-jnp.abs(b - bref))
        qt = (qa * e).astype(BF16)
        kt = (kin * e).astype(BF16)
        if w == R:
            cross = _dot_nt(qt[half:R], kt[0:half])
        else:
            keep = lvl_ref[...] == (m.bit_length() - 1)
            for hh in range(2):
                part = _dot_nt(qt[hh * half:(hh + 1) * half], kt[hh * half:(hh + 1) * half])
                scores[hh] = jnp.where(keep, part, 0.0 if scores[hh] is None else scores[hh])
        m //= 2

    st = st_ref[head]
    o = _dot_nt((qa * jnp.exp2(b)).astype(BF16), st.astype(BF16))
    o = o + jnp.sum(qa * kin, axis=-1, keepdims=True) * vb.astype(F32)
    o_lo = o[0:half] + _dot(scores[0].astype(BF16), vb[0:half])
    o_hi = o[half:R] + _dot(scores[1].astype(BF16), vb[half:R]) + _dot(cross.astype(BF16), vb[0:half])
    o = jnp.concatenate([o_lo, o_hi], axis=0)

    b_end = b[R - 1:R, :]
    k_end = (kin * jnp.exp2(b_end - b)).astype(BF16)
    st_ref[head] = st * jnp.exp2(b_end) + _dot_tn(vb, k_end)

    on = o * lax.rsqrt(jnp.mean(o * o, axis=-1, keepdims=True) + RMS_EPS) * ng_ref[...]
    o_ref[rws, cols] = (on * _silu(g_ref[rws, cols].astype(F32))).astype(o_ref.dtype)


def _hgrn_body(q_ref, f_ref, i_ref, g_ref, lbl_ref, ng_ref, tri_ref, lvl_ref, o_ref, st_ref, *, rows, layer):
    @pl.when(pl.program_id(1) == 0)
    def _():
        st_ref[...] = jnp.zeros(st_ref.shape, F32)

    for chunk in range(q_ref.shape[0] // rows):
        for head in range(st_ref.shape[0]):
            _hgrn_head(q_ref, f_ref, i_ref, g_ref, lbl_ref, ng_ref, tri_ref, lvl_ref, o_ref, st_ref,
                       rows=rows, layer=layer, head=head, row0=chunk * rows)


def _level_table(n):
    t = np.arange(n, dtype=np.int32)[:, None]
    s = np.arange(n, dtype=np.int32)[None, :]
    x = t ^ s
    hb = np.zeros((n, n), np.int32)
    for k in range(1, n.bit_length()):
        hb = hb + (x >= (1 << k)).astype(np.int32)
    return jnp.asarray(np.where(t > s, hb, -1).astype(np.int32))


def _hgrn(hqig, hf, lb_logits, norm_g, layer, batch, seq):
    T = hf.shape[0]
    H = HGRN_HEADS
    R = HGRN_ROWS
    W = H * HGRN_DIM
    nc = seq // (R * HGRN_STEP_CHUNKS)
    L = lb_logits.shape[0]
    blk = lambda off: pl.BlockSpec((R * HGRN_STEP_CHUNKS, W), lambda b, c: (b * nc + c, off))
    const = lambda b, c: (0, 0)
    tri = jnp.asarray(np.tril(np.ones((R, R), np.float32)), dtype=BF16)
    return pl.pallas_call(
        functools.partial(_hgrn_body, rows=R, layer=layer),
        grid=(batch, nc),
        in_specs=[blk(0), blk(0), blk(1), blk(2),
                  pl.BlockSpec((L, W), const), pl.BlockSpec((1, LANES), const),
                  pl.BlockSpec((R, R), const), pl.BlockSpec((R // 2, R // 2), const)],
        out_specs=blk(0),
        out_shape=jax.ShapeDtypeStruct((T, W), BF16),
        scratch_shapes=[pltpu.VMEM((H, HGRN_DIM, HGRN_DIM), F32)],
        compiler_params=_params(2, VMEM_LIMIT),
        name="hgrn2",
    )(hqig, hf, hqig, hqig, lb_logits.astype(F32), norm_g.astype(F32).reshape(1, LANES), tri,
      _level_table(R // 2))


def _conv_body(ab_ref, w_ref, bdw_ref, lg_ref, lb_ref, o_ref, ubuf, cbuf, ush, *, rows, width, chans):
    R, W, C, H = rows, width, chans, CONV_HALO
    c = pl.program_id(1)

    @pl.when(c == 0)
    def _():
        ubuf[0:H, :] = jnp.zeros((H, C), F32)

    a = ab_ref[:, 0:C].astype(F32)
    gate = ab_ref[:, C:2 * C].astype(F32)
    ubuf[H:H + R, :] = a * _sigmoid(gate)

    n_sh = ush.shape[1]
    for s in range(1, SUBLANES):
        ush[s - 1] = ubuf[s:s + n_sh, :]

    rsub = 64
    for lb in range(0, C, LANES):
        for rb in range(0, R, rsub):
            acc = jnp.zeros((rsub, LANES), F32)
            for j in range(W):
                start = H - (W - 1) + j + rb
                s = start % SUBLANES
                a0 = start - s
                if s == 0:
                    u = ubuf[a0:a0 + rsub, lb:lb + LANES]
                else:
                    u = ush[s - 1, a0:a0 + rsub, lb:lb + LANES]
                acc = acc + w_ref[j:j + 1, lb:lb + LANES] * u
            cbuf[rb:rb + rsub, lb:lb + LANES] = acc

    uf = cbuf[...] + bdw_ref[...]
    mu = jnp.mean(uf, axis=-1, keepdims=True)
    d = uf - mu
    var = jnp.mean(d * d, axis=-1, keepdims=True)
    y = d * lax.rsqrt(var + LN_EPS) * lg_ref[...] + lb_ref[...]
    o_ref[...] = _silu(y).astype(o_ref.dtype)
    ubuf[0:H, :] = ubuf[R:R + H, :]


def _conv(cab, conv_w, conv_b, ln_g, ln_b, batch, seq):
    T = cab.shape[0]
    C = cab.shape[1] // 2
    W = conv_w.shape[0]
    R = CONV_ROWS
    nc = seq // R
    vec = pl.BlockSpec((1, C), lambda b, c: (0, 0))
    return pl.pallas_call(
        functools.partial(_conv_body, rows=R, width=W, chans=C),
        grid=(batch, nc),
        in_specs=[pl.BlockSpec((R, 2 * C), lambda b, c: (b * nc + c, 0)),
                  pl.BlockSpec((W, C), lambda b, c: (0, 0)), vec, vec, vec],
        out_specs=pl.BlockSpec((R, C), lambda b, c: (b * nc + c, 0)),
        out_shape=jax.ShapeDtypeStruct((T, C), BF16),
        scratch_shapes=[pltpu.VMEM((R + CONV_HALO, C), F32), pltpu.VMEM((R, C), F32),
                        pltpu.VMEM((SUBLANES - 1, R + CONV_HALO - SUBLANES, C), F32)],
        compiler_params=_params(2, VMEM_LIMIT),
        name="conformer_conv",
    )(cab, conv_w.astype(F32), conv_b.astype(F32).reshape(1, C), ln_g.astype(F32).reshape(1, C),
      ln_b.astype(F32).reshape(1, C))


def _merge_body(att_ref, rec_ref, cv_ref, gl_ref, bg_ref, x_ref, wa_ref, wh_ref, wc_ref, wo_ref,
                fg_ref, wr_ref, br_ref, before_ref, xo_ref, route_ref, cnt_out_ref, mg_ref, cnt_ref, *,
                n_groups, per_group):
    D = x_ref.shape[1]
    att, rec, cv = att_ref[...], rec_ref[...], cv_ref[...]
    cw = 256
    for c in range(0, D, cw):
        def gate(k):
            return _sigmoid(gl_ref[:, k * D + c:k * D + c + cw].astype(F32) + bg_ref[:, k * D + c:k * D + c + cw])
        merged = (gate(0) * _dot(att, wa_ref[:, c:c + cw]) + gate(1) * _dot(rec, wh_ref[:, c:c + cw])
                  + gate(2) * _dot(cv, wc_ref[:, c:c + cw]))
        mg_ref[:, c:c + cw] = merged.astype(BF16)
    xn = x_ref[...] + _dot(mg_ref[...], wo_ref[...])
    xo_ref[...] = xn

    h = xn * lax.rsqrt(jnp.mean(xn * xn, axis=-1, keepdims=True) + RMS_EPS) * fg_ref[...]
    h_hi = h.astype(BF16)
    h_lo = (h - h_hi.astype(F32)).astype(BF16)
    wr = wr_ref[...]
    w_hi = wr.astype(BF16)
    w_lo = (wr - w_hi.astype(F32)).astype(BF16)
    logits = _dot(h_hi, w_hi) + _dot(h_hi, w_lo) + _dot(h_lo, w_hi) + br_ref[...]

    G, E = n_groups, per_group
    lane = lax.broadcasted_iota(jnp.int32, logits.shape, 1)
    big = jnp.int32(LANES)
    is_c = lane < G
    cl = jnp.where(is_c, logits, NEG_INF)
    cm = jnp.max(cl, axis=-1, keepdims=True)
    grp = jnp.min(jnp.where(cl == cm, lane, big), axis=-1, keepdims=True)
    se = jnp.sum(jnp.exp(jnp.where(is_c, logits - cm, NEG_INF)), axis=-1, keepdims=True)
    p_top = 1.0 / se
    lo = G + grp * E
    fl = jnp.where((lane >= lo) & (lane < lo + E), logits, NEG_INF)
    m1 = jnp.max(fl, axis=-1, keepdims=True)
    i1 = jnp.min(jnp.where(fl == m1, lane, big), axis=-1, keepdims=True)
    fl2 = jnp.where(lane == i1, NEG_INF, fl)
    m2 = jnp.max(fl2, axis=-1, keepdims=True)
    i2 = jnp.min(jnp.where(fl2 == m2, lane, big), axis=-1, keepdims=True)
    t = jnp.exp(m2 - m1)
    w1 = p_top / (1.0 + t)
    w2 = p_top * t / (1.0 + t)
    e1 = (i1 - G).astype(F32)
    e2 = (i2 - G).astype(F32)

    @pl.when(pl.program_id(0) == 0)
    def _():
        cnt_ref[...] = jnp.zeros(cnt_ref.shape, F32)

    hit1 = lane == (i1 - G)
    hit2 = lane == (i2 - G)
    hits = jnp.where(hit1 | hit2, 1.0, 0.0)
    prefix = _dot(before_ref[...], hits.astype(BF16)) + cnt_ref[...]
    rank1 = jnp.sum(jnp.where(hit1, prefix, 0.0), axis=-1, keepdims=True)
    rank2 = jnp.sum(jnp.where(hit2, prefix, 0.0), axis=-1, keepdims=True)
    cnt = cnt_ref[...] + jnp.sum(hits, axis=0, keepdims=True)
    cnt_ref[...] = cnt
    cnt_out_ref[...] = cnt

    route_ref[...] = jnp.where(lane == 0, e1, jnp.where(lane == 1, e2,
                               jnp.where(lane == 2, w1, jnp.where(lane == 3, w2,
                               jnp.where(lane == 4, rank1, jnp.where(lane == 5, rank2, 0.0))))))


def _merge(att, rec, cv, gl, b_gate, x2d, wa, wh, wc, wo, ffn_g, w_route, b_route, layer):
    T, D = x2d.shape
    tm = MERGE_TILE
    row = lambda i: (i, 0)
    const2 = lambda i: (0, 0)
    wsel = lambda i: (layer, 0, 0)
    W = att.shape[1]
    before = jnp.asarray(np.tril(np.ones((tm, tm), np.float32), -1), dtype=BF16)
    return pl.pallas_call(
        functools.partial(_merge_body, n_groups=N_GROUPS, per_group=EXPERTS_PER_GROUP),
        grid=(T // tm,),
        in_specs=[pl.BlockSpec((tm, W), row), pl.BlockSpec((tm, W), row), pl.BlockSpec((tm, W), row),
                  pl.BlockSpec((tm, 3 * D), row), pl.BlockSpec((1, 3 * D), const2),
                  pl.BlockSpec((tm, D), row),
                  pl.BlockSpec((None, W, D), wsel), pl.BlockSpec((None, W, D), wsel),
                  pl.BlockSpec((None, W, D), wsel), pl.BlockSpec((None, D, D), wsel),
                  pl.BlockSpec((1, D), const2), pl.BlockSpec((D, LANES), const2),
                  pl.BlockSpec((1, LANES), const2), pl.BlockSpec((tm, tm), const2)],
        out_specs=[pl.BlockSpec((tm, D), row), pl.BlockSpec((tm, LANES), row),
                   pl.BlockSpec((1, LANES), const2)],
        out_shape=[jax.ShapeDtypeStruct((T, D), F32), jax.ShapeDtypeStruct((T, LANES), F32),
                   jax.ShapeDtypeStruct((1, LANES), F32)],
        scratch_shapes=[pltpu.VMEM((tm, D), BF16), pltpu.VMEM((1, LANES), F32)],
        compiler_params=_params(1, VMEM_LIMIT),
        name="merge_route",
    )(att, rec, cv, gl, b_gate.astype(F32).reshape(1, 3 * D), x2d, wa, wh, wc, wo,
      ffn_g.astype(F32).reshape(1, D), w_route, b_route, before)


def _bulk_wait(view, sem):
    pltpu.make_async_copy(view, view, sem).wait()


def _dispatch_body(ps_ref, pn_ref, nu_ref, pos_ref, x_ref, fg_ref, xs_hbm, hbuf, zbuf, sem, zsem, *,
                   tile, n_tiles, n_experts, n_sorted_tiles):
    i = pl.program_id(0)
    slot = i % 2
    ztile = zbuf.shape[0]

    def pad_copies(e):
        out = []
        for k in range(ztile.bit_length() - 1):
            first = ps_ref[e] + ((pn_ref[e] >> (k + 1)) << (k + 1))
            copy = pltpu.make_async_copy(zbuf.at[pl.ds(0, 1 << k)], xs_hbm.at[pl.ds(first, 1 << k)], zsem)
            out.append((((pn_ref[e] >> k) & 1) == 1, copy))
        return out

    def tail_copy(t):
        return pltpu.make_async_copy(zbuf, xs_hbm.at[pl.ds(t * ztile, ztile)], zsem)

    def wait_slot(s):
        _bulk_wait(hbuf.at[s], sem.at[s])
        _bulk_wait(hbuf.at[s], sem.at[s])

    @pl.when(i == 0)
    def _():
        zbuf[...] = jnp.zeros(zbuf.shape, F32)
        for e in range(n_experts):
            for bit_set, copy in pad_copies(e):
                pl.when(bit_set)(copy.start)

        def start_tail(t, carry):
            tail_copy(t).start()
            return carry
        lax.fori_loop(nu_ref[0], n_sorted_tiles, start_tail, 0)

    def scatter_tile(s):
        @pl.when(i >= 2)
        def _():
            wait_slot(s)

        x = x_ref[...]
        h = x * lax.rsqrt(jnp.mean(x * x, axis=-1, keepdims=True) + RMS_EPS) * fg_ref[...]
        hbuf[s] = h.reshape(h.shape[0], 1, h.shape[1])

        def issue(g, carry):
            for k in range(DMA_UNROLL):
                r = g * DMA_UNROLL + k
                src = hbuf.at[s, pl.ds(r, 1)]
                pltpu.make_async_copy(src, xs_hbm.at[pl.ds(pos_ref[0, 0, r], 1)],
                                      sem.at[s]).start(priority=0)
                pltpu.make_async_copy(src, xs_hbm.at[pl.ds(pos_ref[0, 0, tile + r], 1)],
                                      sem.at[s]).start(priority=1)
            return carry

        lax.fori_loop(0, tile // DMA_UNROLL, issue, 0)

    for s in range(2):
        pl.when(slot == s)(functools.partial(scatter_tile, s))

    @pl.when(i == n_tiles - 1)
    def _():
        wait_slot((n_tiles - 1) % 2)
        if n_tiles >= 2:
            wait_slot(n_tiles % 2)
        for e in range(n_experts):
            for bit_set, copy in pad_copies(e):
                pl.when(bit_set)(copy.wait)

        def done_tail(t, carry):
            tail_copy(t).wait()
            return carry
        lax.fori_loop(nu_ref[0], n_sorted_tiles, done_tail, 0)


def _dispatch(x2d, ffn_g, pos_tiles, pad_start, pad_len, n_used, cap):
    T, D = x2d.shape
    tile = TOKEN_TILE
    nt = T // tile
    n_experts = pad_start.shape[0]
    grid_spec = pltpu.PrefetchScalarGridSpec(
        num_scalar_prefetch=3,
        grid=(nt,),
        in_specs=[pl.BlockSpec((1, 1, 2 * tile), lambda i, ps, pn, nu: (i, 0, 0), memory_space=pltpu.SMEM),
                  pl.BlockSpec((tile, D), lambda i, ps, pn, nu: (i, 0)),
                  pl.BlockSpec((1, D), lambda i, ps, pn, nu: (0, 0))],
        out_specs=pl.BlockSpec(memory_space=pl.ANY),
        scratch_shapes=[pltpu.VMEM((2, tile, 1, D), F32), pltpu.VMEM((EXPERT_TILE, 1, D), F32),
                        pltpu.SemaphoreType.DMA((2,)), pltpu.SemaphoreType.DMA(())],
    )
    return pl.pallas_call(
        functools.partial(_dispatch_body, tile=tile, n_tiles=nt, n_experts=n_experts,
                          n_sorted_tiles=cap // EXPERT_TILE),
        grid_spec=grid_spec,
        out_shape=jax.ShapeDtypeStruct((cap, 1, D), F32),
        compiler_params=_params(1, VMEM_LIMIT),
        name="moe_dispatch",
    )(pad_start, pad_len, n_used, pos_tiles, x2d, ffn_g.astype(F32).reshape(1, D))


def _expert_body(te_ref, nu_ref, nx_ref, sl_ref, x_ref, win_hbm, wout_hbm, y_ref, win_bf, wout_bf, x2_ref,
                 win_st, wout_st, wsem, *, layer):
    i = pl.program_id(0)
    ff = wout_bf.shape[0]
    used = i < nu_ref[0]
    new_expert = jnp.logical_or(i == 0, te_ref[i] != te_ref[jnp.maximum(i - 1, 0)])

    def weight_copies(e, slot):
        return (pltpu.make_async_copy(win_hbm.at[layer, e], win_st.at[slot], wsem.at[0, slot]),
                pltpu.make_async_copy(wout_hbm.at[layer, e], wout_st.at[slot], wsem.at[1, slot]))

    @pl.when(i == 0)
    def _():
        for c in weight_copies(te_ref[0], 0):
            c.start()

    def begin_run(s):
        for c in weight_copies(te_ref[i], s):
            c.wait()
        win_bf[...] = win_st[s].astype(BF16)
        wout_bf[...] = wout_st[s].astype(BF16)

        @pl.when(nx_ref[i] >= 0)
        def _():
            for c in weight_copies(nx_ref[i], 1 - s):
                c.start()

    for s in range(2):
        pl.when(jnp.logical_and(jnp.logical_and(used, new_expert), sl_ref[i] == s))(
            functools.partial(begin_run, s))

    @pl.when(used)
    def _():
        x2_ref[...] = x_ref[...].reshape(x2_ref.shape)
        gu = _dot(x2_ref[...].astype(BF16), win_bf[...])
        act = (_silu(gu[:, 0:ff]) * gu[:, ff:2 * ff]).astype(BF16)
        y = _dot(act, wout_bf[...])
        y_ref[...] = y.reshape(y_ref.shape)

    @pl.when(jnp.logical_not(used))
    def _():
        y_ref[...] = jnp.zeros(y_ref.shape, F32)


def _experts(x_sorted, w_exp_in, w_exp_out, tile_expert, n_used, next_expert, run_slot, layer):
    cap, _, D = x_sorted.shape
    tile = EXPERT_TILE
    nt = cap // tile
    ff = w_exp_out.shape[2]
    rows = lambda i, te, nu, nx, sl: (i, 0, 0)
    grid_spec = pltpu.PrefetchScalarGridSpec(
        num_scalar_prefetch=4,
        grid=(nt,),
        in_specs=[pl.BlockSpec((tile, 1, D), rows),
                  pl.BlockSpec(memory_space=pl.ANY), pl.BlockSpec(memory_space=pl.ANY)],
        out_specs=pl.BlockSpec((tile, 1, D), rows),
        scratch_shapes=[pltpu.VMEM((D, 2 * ff), BF16), pltpu.VMEM((ff, D), BF16),
                        pltpu.VMEM((tile, D), F32),
                        pltpu.VMEM((2, D, 2 * ff), F32), pltpu.VMEM((2, ff, D), F32),
                        pltpu.SemaphoreType.DMA((2, 2))],
    )
    return pl.pallas_call(
        functools.partial(_expert_body, layer=layer),
        grid_spec=grid_spec,
        out_shape=jax.ShapeDtypeStruct((cap, 1, D), F32),
        compiler_params=_params(1, VMEM_LIMIT),
        name="expert_ffn",
    )(tile_expert, n_used, next_expert, run_slot, x_sorted, w_exp_in, w_exp_out)


def _combine_body(pos0_ref, posn_ref, y_hbm, x_ref, route_ref, o_ref, ybuf, y2_ref, sem, *, tile, n_tiles):
    i = pl.program_id(0)
    slot = i % 2

    def gather(pos_ref, s):
        def issue(g, carry):
            for k in range(DMA_UNROLL):
                r = g * DMA_UNROLL + k
                pltpu.make_async_copy(y_hbm.at[pl.ds(pos_ref[0, 0, r], 1)],
                                      ybuf.at[s, pl.ds(r, 1)], sem.at[s]).start(priority=k % 2)
            return carry
        lax.fori_loop(0, 2 * tile // DMA_UNROLL, issue, 0)

    @pl.when(i == 0)
    def _():
        gather(pos0_ref, 0)

    def combine_tile(s):
        @pl.when(i + 1 < n_tiles)
        def _():
            gather(posn_ref, 1 - s)

        _bulk_wait(ybuf.at[s], sem.at[s])
        y2_ref[...] = ybuf[s].reshape(y2_ref.shape)
        route = route_ref[...]
        o_ref[...] = (x_ref[...] + route[:, 2:3] * y2_ref[0:tile, :]
                      + route[:, 3:4] * y2_ref[tile:2 * tile, :])

    for s in range(2):
        pl.when(slot == s)(functools.partial(combine_tile, s))


def _combine(x2d, route, y_sorted, pos_tiles):
    T, D = x2d.shape
    tile = TOKEN_TILE
    nt = T // tile
    return pl.pallas_call(
        functools.partial(_combine_body, tile=tile, n_tiles=nt),
        grid=(nt,),
        in_specs=[pl.BlockSpec((1, 1, 2 * tile), lambda i: (0, 0, 0), memory_space=pltpu.SMEM),
                  pl.BlockSpec((1, 1, 2 * tile), lambda i: (jnp.minimum(i + 1, nt - 1), 0, 0),
                               memory_space=pltpu.SMEM),
                  pl.BlockSpec(memory_space=pl.ANY),
                  pl.BlockSpec((tile, D), lambda i: (i, 0)),
                  pl.BlockSpec((tile, LANES), lambda i: (i, 0))],
        out_specs=pl.BlockSpec((tile, D), lambda i: (i, 0)),
        out_shape=jax.ShapeDtypeStruct((T, D), F32),
        scratch_shapes=[pltpu.VMEM((2, 2 * tile, 1, D), F32), pltpu.VMEM((2 * tile, D), F32),
                        pltpu.SemaphoreType.DMA((2,))],
        compiler_params=_params(1, VMEM_LIMIT),
        name="moe_combine",
    )(pos_tiles, pos_tiles, y_sorted, x2d, route)


def _positions_body(route_ref, offs_ref, pos_ref, *, tile):
    r = route_ref[...]
    lane = lax.broadcasted_iota(jnp.int32, r.shape, 1)
    lane_f = lane.astype(F32)
    offs = offs_ref[...]
    pos1 = jnp.sum(jnp.where(lane_f == r[:, 0:1], offs, 0.0), axis=-1, keepdims=True) + r[:, 4:5]
    pos2 = jnp.sum(jnp.where(lane_f == r[:, 1:2], offs, 0.0), axis=-1, keepdims=True) + r[:, 5:6]
    both = jnp.where(lane == 0, pos1, jnp.where(lane == 1, pos2, 0.0))
    for k in range(r.shape[0] // tile):
        t = both[k * tile:(k + 1) * tile, :].T
        pos_ref[k, :, 0:tile] = t[0:1, :].astype(jnp.int32)
        pos_ref[k, :, tile:2 * tile] = t[1:2, :].astype(jnp.int32)


def _positions(route, offs_row):
    T = route.shape[0]
    tile = TOKEN_TILE
    step = 4 * tile
    return pl.pallas_call(
        functools.partial(_positions_body, tile=tile),
        grid=(T // step,),
        in_specs=[pl.BlockSpec((step, LANES), lambda i: (i, 0)),
                  pl.BlockSpec((1, LANES), lambda i: (0, 0))],
        out_specs=pl.BlockSpec((step // tile, 1, 2 * tile), lambda i: (i, 0, 0)),
        out_shape=jax.ShapeDtypeStruct((T // tile, 1, 2 * tile), jnp.int32),
        compiler_params=_params(1, VMEM_LIMIT),
        name="moe_positions",
    )(route, offs_row)


def _dispatch_plan(route, counts, n_experts, tile, cap):
    cnt = counts[0, :n_experts].astype(jnp.int32)
    padded = ((cnt + tile - 1) // tile) * tile
    ends = jnp.cumsum(padded)
    offs = ends - padded
    nt = cap // tile
    n_used = (ends[-1] // tile).astype(jnp.int32)
    first_row = jnp.minimum(jnp.arange(nt, dtype=jnp.int32), n_used - 1) * tile
    te = jnp.sum((ends[None, :] <= first_row[:, None]).astype(jnp.int32), axis=1)
    te = jnp.minimum(te, n_experts - 1).astype(jnp.int32)
    run_start = jnp.concatenate([jnp.ones((1,), bool), te[1:] != te[:-1]])
    run_slot = ((jnp.cumsum(run_start.astype(jnp.int32)) - 1) % 2).astype(jnp.int32)
    ids = jnp.arange(n_experts, dtype=jnp.int32)
    later = (ids[None, :] > ids[:, None]) & (cnt > 0)[None, :]
    following = jnp.min(jnp.where(later, ids[None, :], n_experts), axis=1)
    following = jnp.where(following >= n_experts, -1, following).astype(jnp.int32)
    next_expert = jnp.sum(jnp.where(te[:, None] == ids[None, :], following[None, :], 0), axis=1).astype(jnp.int32)
    offs_row = jnp.pad(offs.astype(F32), (0, LANES - n_experts)).reshape(1, LANES)
    pos_tiles = _positions(route, offs_row)
    return (te, n_used.reshape(1), pos_tiles, (offs + cnt).astype(jnp.int32), (padded - cnt).astype(jnp.int32),
            next_expert, run_slot)


def kernel(x, mix_norm_g, w_in, b_gate, q_norm_g, k_norm_g, lb_logits, hgrn_norm_g, conv_w, conv_b,
           conv_ln_g, conv_ln_b, w_att_o, w_hgrn_o, w_conv_o, w_out, ffn_norm_g, w_coarse, b_coarse,
           w_fine, b_fine, w_exp_in, w_exp_out):
    B, S, D = x.shape
    L = w_in.shape[0]
    T = B * S
    n_experts = w_exp_in.shape[1]
    cap = 2 * T + n_experts * EXPERT_TILE

    w_in_bf = w_in.astype(BF16)
    wa_bf, wh_bf, wc_bf, wo_bf = (w.astype(BF16) for w in (w_att_o, w_hgrn_o, w_conv_o, w_out))
    pad = LANES - N_GROUPS - n_experts
    w_route = jnp.concatenate([w_coarse, w_fine, jnp.zeros((L, D, pad), F32)], axis=-1).astype(F32)
    b_route = jnp.concatenate([b_coarse, b_fine, jnp.zeros((L, pad), F32)], axis=-1).astype(F32)

    x2d = x.reshape(T, D)
    for l in range(L):
        qkv, hf, hqig, cab, gl = _inproj(x2d, mix_norm_g[l], w_in_bf, l)
        att = _attention(qkv, q_norm_g[l], k_norm_g[l], B, S)
        rec = _hgrn(hqig, hf, lb_logits, hgrn_norm_g[l], l, B, S)
        cv = _conv(cab, conv_w[l], conv_b[l], conv_ln_g[l], conv_ln_b[l], B, S)
        x_mid, route, counts = _merge(att, rec, cv, gl, b_gate[l], x2d, wa_bf, wh_bf, wc_bf, wo_bf,
                                      ffn_norm_g[l], w_route[l], b_route[l].reshape(1, LANES), l)
        te, n_used, pos_tiles, pad_start, pad_len, next_expert, run_slot = _dispatch_plan(
            route, counts, n_experts, EXPERT_TILE, cap)
        x_sorted = _dispatch(x_mid, ffn_norm_g[l], pos_tiles, pad_start, pad_len, n_used, cap)
        y_sorted = _experts(x_sorted, w_exp_in, w_exp_out, te, n_used, next_expert, run_slot, l)
        x2d = _combine(x_mid, route, y_sorted, pos_tiles)
    return x2d.reshape(B, S, D)
```

```python
import functools

import jax
import jax.numpy as jnp
import numpy as np
from jax import lax
from jax.experimental import pallas as pl
from jax.experimental.pallas import tpu as pltpu

F32 = jnp.float32
BF16 = jnp.bfloat16

LANES = 128
SUBLANES = 8
RMS_EPS = 1e-6
LN_EPS = 1e-5
NEG_INF = -1e30
LOG2_E = 1.4426950408889634

ATT_HEADS = 8
ATT_HEAD_DIM = 64
MOBA_BLOCK = 256
MOBA_TOPK = 3
HGRN_HEADS = 4
HGRN_DIM = 128
HGRN_ROWS = 256
HGRN_STEP_CHUNKS = 4
CONV_WIDTH = 31
CONV_ROWS = 1024
CONV_HALO = 32
N_GROUPS = 4
EXPERTS_PER_GROUP = 8
EXPERT_TILE = 512
TOKEN_TILE = 512
INPROJ_TILE = 512
MERGE_TILE = 512
DMA_UNROLL = 8
VMEM_LIMIT = 56 * 1024 * 1024


def _params(n_axes, vmem=None):
    return pltpu.CompilerParams(dimension_semantics=("arbitrary",) * n_axes,
                                vmem_limit_bytes=vmem)


def _dot(a, b):
    return jnp.dot(a, b, preferred_element_type=F32)


def _dot_nt(a, b):
    return lax.dot_general(a, b, (((1,), (1,)), ((), ())), preferred_element_type=F32)


def _dot_tn(a, b):
    return lax.dot_general(a, b, (((0,), (0,)), ((), ())), preferred_element_type=F32)


def _split3(x):
    x1 = x.astype(BF16)
    r1 = x - x1.astype(F32)
    x2 = r1.astype(BF16)
    x3 = (r1 - x2.astype(F32)).astype(BF16)
    return x1, x2, x3


def _sigmoid(x):
    return 1.0 / (1.0 + jnp.exp(-x))


def _silu(x):
    return x * _sigmoid(x)


def _inproj_body(x_ref, g_ref, w_ref, qkv_ref, hf_ref, hqig_ref, cv_ref, gl_ref, *, segs, chunk):
    x = x_ref[...]
    ms = jnp.mean(x * x, axis=-1, keepdims=True)
    h = (x * lax.rsqrt(ms + RMS_EPS) * g_ref[...]).astype(BF16)
    refs = (qkv_ref, hf_ref, hqig_ref, cv_ref, gl_ref)
    for ridx, dst0, src0, width in segs:
        ref = refs[ridx]
        for c in range(0, width, chunk):
            y = _dot(h, w_ref[:, src0 + c:src0 + c + chunk])
            ref[:, dst0 + c:dst0 + c + chunk] = y.astype(ref.dtype)


def _inproj(x2d, norm_g, w_in_bf, layer):
    T, D = x2d.shape
    A = ATT_HEADS * ATT_HEAD_DIM
    K = HGRN_HEADS * HGRN_DIM
    C = A
    segs = ((0, 0, 0, 3 * A),
            (2, 0, 3 * A, K),
            (1, 0, 3 * A + K, K),
            (2, K, 3 * A + 2 * K, K),
            (2, 2 * K, 3 * A + 3 * K, K),
            (3, 0, 3 * A + 4 * K, 2 * C),
            (4, 0, 3 * A + 4 * K + 2 * C, 3 * D))
    ncols = w_in_bf.shape[-1]
    tm = INPROJ_TILE
    row = lambda i: (i, 0)
    return pl.pallas_call(
        functools.partial(_inproj_body, segs=segs, chunk=512),
        grid=(T // tm,),
        in_specs=[pl.BlockSpec((tm, D), row),
                  pl.BlockSpec((1, D), lambda i: (0, 0)),
                  pl.BlockSpec((None, D, ncols), lambda i: (layer, 0, 0))],
        out_specs=[pl.BlockSpec((tm, 3 * A), row), pl.BlockSpec((tm, K), row),
                   pl.BlockSpec((tm, 3 * K), row), pl.BlockSpec((tm, 2 * C), row),
                   pl.BlockSpec((tm, 3 * D), row)],
        out_shape=[jax.ShapeDtypeStruct((T, 3 * A), BF16), jax.ShapeDtypeStruct((T, K), F32),
                   jax.ShapeDtypeStruct((T, 3 * K), BF16), jax.ShapeDtypeStruct((T, 2 * C), BF16),
                   jax.ShapeDtypeStruct((T, 3 * D), BF16)],
        compiler_params=_params(1, VMEM_LIMIT),
        name="inproj",
    )(x2d, norm_g.reshape(1, D), w_in_bf)


def _attn_body(q_ref, k_ref, v_ref, gq_ref, gk_ref, o_ref, kb_ref, vt_ref, s_ref, *, seq, blk, topk):
    nb = seq // blk
    dh = ATT_HEAD_DIM
    dh_sh = dh.bit_length() - 1
    scale = dh ** -0.5 * LOG2_E
    lane = lax.broadcasted_iota(jnp.int32, (1, LANES), 1)
    hr = lax.broadcasted_iota(jnp.int32, (LANES, LANES), 0) >> dh_sh
    hc = lax.broadcasted_iota(jnp.int32, (LANES, LANES), 1) >> dh_sh
    same_head = jnp.where(hr == hc, 1.0, 0.0).astype(BF16)

    def head_norm(x, g):
        x2 = x * x
        hi = x2.astype(BF16)
        lo = (x2 - hi.astype(F32)).astype(BF16)
        ssq = _dot(hi, same_head) + _dot(lo, same_head)
        return x * lax.rsqrt(ssq * (1.0 / dh) + RMS_EPS) * g

    kmeans = []
    for j in range(nb):
        kn = head_norm(k_ref[j * blk:(j + 1) * blk, :].astype(F32), gk_ref[...])
        kb_ref[j * blk:(j + 1) * blk, :] = kn.astype(BF16)
        kmeans.append(jnp.mean(kn, axis=0, keepdims=True))
        vt_ref[:, j * blk:(j + 1) * blk] = v_ref[j * blk:(j + 1) * blk, :].astype(F32).T.astype(BF16)
    kmean = jnp.concatenate(kmeans, axis=0)
    km_hi = kmean.astype(BF16)
    km_lo = (kmean - km_hi.astype(F32)).astype(BF16)
    first_head = lane < dh
    first_head_t = lax.broadcasted_iota(jnp.int32, (LANES, 1), 0) < dh
    blk_n = lax.broadcasted_iota(jnp.int32, (nb, 1), 0)

    key_l = lax.broadcasted_iota(jnp.int32, (blk, 2 * blk), 0)
    qry_l = lax.broadcasted_iota(jnp.int32, (blk, 2 * blk), 1) & (blk - 1)
    causal = key_l <= qry_l

    for i in range(nb):
        rows = head_norm(q_ref[i * blk:(i + 1) * blk, :].astype(F32), gq_ref[...])
        qi = jnp.concatenate([jnp.where(first_head, rows, 0.0), jnp.where(first_head, 0.0, rows)], axis=0)
        qs = (qi * scale).astype(BF16)

        sel = None
        if i > topk:
            q_hi = qi.astype(BF16)
            q_lo = (qi - q_hi.astype(F32)).astype(BF16)
            gate = _dot_nt(km_hi, q_hi) + _dot_nt(km_lo, q_hi) + _dot_nt(km_hi, q_lo)
            valid = blk_n < i
            gm = jnp.where(valid, gate, NEG_INF)
            rank = jnp.zeros(gm.shape, F32)
            for r in range(1, nb):
                gr = pltpu.roll(gm, r, 0)
                ahead = (gr > gm) | ((gr == gm) & (blk_n >= r))
                rank = rank + jnp.where(ahead, 1.0, 0.0)
            sel = jnp.where(valid & (rank < topk), 1.0, 0.0)

        m = None
        for j in range(i + 1):
            s = _dot_nt(kb_ref[j * blk:(j + 1) * blk, :], qs)
            if j == i:
                s = jnp.where(causal, s, NEG_INF)
            elif sel is not None:
                s = jnp.where(sel[j:j + 1, :] > 0.5, s, NEG_INF)
            s_ref[j * blk:(j + 1) * blk, :] = s
            mj = jnp.max(s, axis=0, keepdims=True)
            m = mj if m is None else jnp.maximum(m, mj)

        l = None
        acc = None
        for j in range(i + 1):
            p = jnp.exp2(s_ref[j * blk:(j + 1) * blk, :] - m)
            lj = jnp.sum(p, axis=0, keepdims=True)
            aj = _dot(vt_ref[:, j * blk:(j + 1) * blk], p.astype(BF16))
            l = lj if l is None else l + lj
            acc = aj if acc is None else acc + aj

        ot = acc / l
        merged = jnp.where(first_head_t, ot[:, 0:blk], ot[:, blk:2 * blk])
        o_ref[i * blk:(i + 1) * blk, :] = merged.T.astype(o_ref.dtype)


def _attention(qkv, q_norm_g, k_norm_g, batch, seq):
    T = qkv.shape[0]
    A = ATT_HEADS * ATT_HEAD_DIM
    npair = A // LANES
    reps = LANES // ATT_HEAD_DIM
    gq = jnp.tile(q_norm_g.astype(F32), reps).reshape(1, LANES)
    gk = jnp.tile(k_norm_g.astype(F32), reps).reshape(1, LANES)
    blk_spec = lambda off: pl.BlockSpec((seq, LANES), lambda b, p: (b, off + p))
    vec = pl.BlockSpec((1, LANES), lambda b, p: (0, 0))
    return pl.pallas_call(
        functools.partial(_attn_body, seq=seq, blk=MOBA_BLOCK, topk=MOBA_TOPK),
        grid=(batch, npair),
        in_specs=[blk_spec(0), blk_spec(npair), blk_spec(2 * npair), vec, vec],
        out_specs=pl.BlockSpec((seq, LANES), lambda b, p: (b, p)),
        out_shape=jax.ShapeDtypeStruct((T, A), BF16),
        scratch_shapes=[pltpu.VMEM((seq, LANES), BF16), pltpu.VMEM((LANES, seq), BF16),
                        pltpu.VMEM((seq, 2 * MOBA_BLOCK), F32)],
        compiler_params=_params(2, VMEM_LIMIT),
        name="moba_attn",
    )(qkv, qkv, qkv, gq, gk)


def _hgrn_head(q_ref, f_ref, i_ref, g_ref, lbl_ref, ng_ref, tri_ref, lvl_ref, o_ref, st_ref, *,
               rows, layer, head, row0):
    R = rows
    cols = slice(head * HGRN_DIM, (head + 1) * HGRN_DIM)
    rws = slice(row0, row0 + R)

    lbl = lbl_ref[:, cols]
    e = jnp.exp(lbl - jnp.max(lbl, axis=0, keepdims=True))
    p = e / jnp.sum(e, axis=0, keepdims=True)
    lb = jnp.maximum(jnp.sum(p[0:layer + 1], axis=0, keepdims=True) - p[0:1], 0.0)

    sig = _sigmoid(f_ref[rws, cols])
    g = jnp.log(lb + (1.0 - lb) * sig)
    kin = (1.0 - lb) * (1.0 - sig)
    qa = _silu(q_ref[rws, cols].astype(F32))
    vb = i_ref[rws, cols]

    t_col = lax.broadcasted_iota(jnp.int32, (R, 1), 0)

    tri = tri_ref[...]
    g1, g2, g3 = _split3(g)
    b = (_dot(tri, g1) + _dot(tri, g2) + _dot(tri, g3)) * LOG2_E

    half = R // 2
    scores = [None, None]
    cross = None
    m = half
    while m >= 1:
        w = 2 * m
        if w >= SUBLANES:
            b3 = b.reshape(R // w, w, LANES)
            bref = jnp.broadcast_to(b3[:, m - 1:m, :], b3.shape).reshape(R, LANES)
        else:
            tl = t_col & (w - 1)
            bref = None
            for resid in range(w):
                shift = resid - (m - 1)
                cand = b if shift == 0 else pltpu.roll(b, shift % R, 0)
                bref = cand if bref is None else jnp.where(tl == resid, cand, bref)
        e = jnp.exp2(-jnp.abs(b - bref))
        qt = (qa * e).astype(BF16)
        kt = (kin * e).astype(BF16)
        if w == R:
            cross = _dot_nt(qt[half:R], kt[0:half])
        else:
            keep = lvl_ref[...] == (m.bit_length() - 1)
            for hh in range(2):
                part = _dot_nt(qt[hh * half:(hh + 1) * half], kt[hh * half:(hh + 1) * half])
                scores[hh] = jnp.where(keep, part, 0.0 if scores[hh] is None else scores[hh])
        m //= 2

    st = st_ref[head]
    o = _dot_nt((qa * jnp.exp2(b)).astype(BF16), st.astype(BF16))
    o = o + jnp.sum(qa * kin, axis=-1, keepdims=True) * vb.astype(F32)
    o_lo = o[0:half] + _dot(scores[0].astype(BF16), vb[0:half])
    o_hi = o[half:R] + _dot(scores[1].astype(BF16), vb[half:R]) + _dot(cross.astype(BF16), vb[0:half])
    o = jnp.concatenate([o_lo, o_hi], axis=0)

    b_end = b[R - 1:R, :]
    k_end = (kin * jnp.exp2(b_end - b)).astype(BF16)
    st_ref[head] = st * jnp.exp2(b_end) + _dot_tn(vb, k_end)

    on = o * lax.rsqrt(jnp.mean(o * o, axis=-1, keepdims=True) + RMS_EPS) * ng_ref[...]
    o_ref[rws, cols] = (on * _silu(g_ref[rws, cols].astype(F32))).astype(o_ref.dtype)


def _hgrn_body(q_ref, f_ref, i_ref, g_ref, lbl_ref, ng_ref, tri_ref, lvl_ref, o_ref, st_ref, *, rows, layer):
    @pl.when(pl.program_id(1) == 0)
    def _():
        st_ref[...] = jnp.zeros(st_ref.shape, F32)

    for chunk in range(q_ref.shape[0] // rows):
        for head in range(st_ref.shape[0]):
            _hgrn_head(q_ref, f_ref, i_ref, g_ref, lbl_ref, ng_ref, tri_ref, lvl_ref, o_ref, st_ref,
                       rows=rows, layer=layer, head=head, row0=chunk * rows)


def _level_table(n):
    t = np.arange(n, dtype=np.int32)[:, None]
    s = np.arange(n, dtype=np.int32)[None, :]
    x = t ^ s
    hb = np.zeros((n, n), np.int32)
    for k in range(1, n.bit_length()):
        hb = hb + (x >= (1 << k)).astype(np.int32)
    return jnp.asarray(np.where(t > s, hb, -1).astype(np.int32))


def _hgrn(hqig, hf, lb_logits, norm_g, layer, batch, seq):
    T = hf.shape[0]
    H = HGRN_HEADS
    R = HGRN_ROWS
    W = H * HGRN_DIM
    nc = seq // (R * HGRN_STEP_CHUNKS)
    L = lb_logits.shape[0]
    blk = lambda off: pl.BlockSpec((R * HGRN_STEP_CHUNKS, W), lambda b, c: (b * nc + c, off))
    const = lambda b, c: (0, 0)
    tri = jnp.asarray(np.tril(np.ones((R, R), np.float32)), dtype=BF16)
    return pl.pallas_call(
        functools.partial(_hgrn_body, rows=R, layer=layer),
        grid=(batch, nc),
        in_specs=[blk(0), blk(0), blk(1), blk(2),
                  pl.BlockSpec((L, W), const), pl.BlockSpec((1, LANES), const),
                  pl.BlockSpec((R, R), const), pl.BlockSpec((R // 2, R // 2), const)],
        out_specs=blk(0),
        out_shape=jax.ShapeDtypeStruct((T, W), BF16),
        scratch_shapes=[pltpu.VMEM((H, HGRN_DIM, HGRN_DIM), F32)],
        compiler_params=_params(2, VMEM_LIMIT),
        name="hgrn2",
    )(hqig, hf, hqig, hqig, lb_logits.astype(F32), norm_g.astype(F32).reshape(1, LANES), tri,
      _level_table(R // 2))


def _conv_body(ab_ref, w_ref, bdw_ref, lg_ref, lb_ref, o_ref, ubuf, cbuf, ush, *, rows, width, chans):
    R, W, C, H = rows, width, chans, CONV_HALO
    c = pl.program_id(1)

    @pl.when(c == 0)
    def _():
        ubuf[0:H, :] = jnp.zeros((H, C), F32)

    a = ab_ref[:, 0:C].astype(F32)
    gate = ab_ref[:, C:2 * C].astype(F32)
    ubuf[H:H + R, :] = a * _sigmoid(gate)

    n_sh = ush.shape[1]
    for s in range(1, SUBLANES):
        ush[s - 1] = ubuf[s:s + n_sh, :]

    rsub = 64
    for lb in range(0, C, LANES):
        for rb in range(0, R, rsub):
            acc = jnp.zeros((rsub, LANES), F32)
            for j in range(W):
                start = H - (W - 1) + j + rb
                s = start % SUBLANES
                a0 = start - s
                if s == 0:
                    u = ubuf[a0:a0 + rsub, lb:lb + LANES]
                else:
                    u = ush[s - 1, a0:a0 + rsub, lb:lb + LANES]
                acc = acc + w_ref[j:j + 1, lb:lb + LANES] * u
            cbuf[rb:rb + rsub, lb:lb + LANES] = acc

    uf = cbuf[...] + bdw_ref[...]
    mu = jnp.mean(uf, axis=-1, keepdims=True)
    d = uf - mu
    var = jnp.mean(d * d, axis=-1, keepdims=True)
    y = d * lax.rsqrt(var + LN_EPS) * lg_ref[...] + lb_ref[...]
    o_ref[...] = _silu(y).astype(o_ref.dtype)
    ubuf[0:H, :] = ubuf[R:R + H, :]


def _conv(cab, conv_w, conv_b, ln_g, ln_b, batch, seq):
    T = cab.shape[0]
    C = cab.shape[1] // 2
    W = conv_w.shape[0]
    R = CONV_ROWS
    nc = seq // R
    vec = pl.BlockSpec((1, C), lambda b, c: (0, 0))
    return pl.pallas_call(
        functools.partial(_conv_body, rows=R, width=W, chans=C),
        grid=(batch, nc),
        in_specs=[pl.BlockSpec((R, 2 * C), lambda b, c: (b * nc + c, 0)),
                  pl.BlockSpec((W, C), lambda b, c: (0, 0)), vec, vec, vec],
        out_specs=pl.BlockSpec((R, C), lambda b, c: (b * nc + c, 0)),
        out_shape=jax.ShapeDtypeStruct((T, C), BF16),
        scratch_shapes=[pltpu.VMEM((R + CONV_HALO, C), F32), pltpu.VMEM((R, C), F32),
                        pltpu.VMEM((SUBLANES - 1, R + CONV_HALO - SUBLANES, C), F32)],
        compiler_params=_params(2, VMEM_LIMIT),
        name="conformer_conv",
    )(cab, conv_w.astype(F32), conv_b.astype(F32).reshape(1, C), ln_g.astype(F32).reshape(1, C),
      ln_b.astype(F32).reshape(1, C))


def _merge_body(att_ref, rec_ref, cv_ref, gl_ref, bg_ref, x_ref, wa_ref, wh_ref, wc_ref, wo_ref,
                fg_ref, wr_ref, br_ref, before_ref, xo_ref, route_ref, cnt_out_ref, mg_ref, cnt_ref, *,
                n_groups, per_group):
    D = x_ref.shape[1]
    att, rec, cv = att_ref[...], rec_ref[...], cv_ref[...]
    cw = 256
    for c in range(0, D, cw):
        def gate(k):
            return _sigmoid(gl_ref[:, k * D + c:k * D + c + cw].astype(F32) + bg_ref[:, k * D + c:k * D + c + cw])
        merged = (gate(0) * _dot(att, wa_ref[:, c:c + cw]) + gate(1) * _dot(rec, wh_ref[:, c:c + cw])
                  + gate(2) * _dot(cv, wc_ref[:, c:c + cw]))
        mg_ref[:, c:c + cw] = merged.astype(BF16)
    xn = x_ref[...] + _dot(mg_ref[...], wo_ref[...])
    xo_ref[...] = xn

    h = xn * lax.rsqrt(jnp.mean(xn * xn, axis=-1, keepdims=True) + RMS_EPS) * fg_ref[...]
    h_hi = h.astype(BF16)
    h_lo = (h - h_hi.astype(F32)).astype(BF16)
    wr = wr_ref[...]
    w_hi = wr.astype(BF16)
    w_lo = (wr - w_hi.astype(F32)).astype(BF16)
    logits = _dot(h_hi, w_hi) + _dot(h_hi, w_lo) + _dot(h_lo, w_hi) + br_ref[...]

    G, E = n_groups, per_group
    lane = lax.broadcasted_iota(jnp.int32, logits.shape, 1)
    big = jnp.int32(LANES)
    is_c = lane < G
    cl = jnp.where(is_c, logits, NEG_INF)
    cm = jnp.max(cl, axis=-1, keepdims=True)
    grp = jnp.min(jnp.where(cl == cm, lane, big), axis=-1, keepdims=True)
    se = jnp.sum(jnp.exp(jnp.where(is_c, logits - cm, NEG_INF)), axis=-1, keepdims=True)
    p_top = 1.0 / se
    lo = G + grp * E
    fl = jnp.where((lane >= lo) & (lane < lo + E), logits, NEG_INF)
    m1 = jnp.max(fl, axis=-1, keepdims=True)
    i1 = jnp.min(jnp.where(fl == m1, lane, big), axis=-1, keepdims=True)
    fl2 = jnp.where(lane == i1, NEG_INF, fl)
    m2 = jnp.max(fl2, axis=-1, keepdims=True)
    i2 = jnp.min(jnp.where(fl2 == m2, lane, big), axis=-1, keepdims=True)
    t = jnp.exp(m2 - m1)
    w1 = p_top / (1.0 + t)
    w2 = p_top * t / (1.0 + t)
    e1 = (i1 - G).astype(F32)
    e2 = (i2 - G).astype(F32)

    @pl.when(pl.program_id(0) == 0)
    def _():
        cnt_ref[...] = jnp.zeros(cnt_ref.shape, F32)

    hit1 = lane == (i1 - G)
    hit2 = lane == (i2 - G)
    hits = jnp.where(hit1 | hit2, 1.0, 0.0)
    prefix = _dot(before_ref[...], hits.astype(BF16)) + cnt_ref[...]
    rank1 = jnp.sum(jnp.where(hit1, prefix, 0.0), axis=-1, keepdims=True)
    rank2 = jnp.sum(jnp.where(hit2, prefix, 0.0), axis=-1, keepdims=True)
    cnt = cnt_ref[...] + jnp.sum(hits, axis=0, keepdims=True)
    cnt_ref[...] = cnt
    cnt_out_ref[...] = cnt

    route_ref[...] = jnp.where(lane == 0, e1, jnp.where(lane == 1, e2,
                               jnp.where(lane == 2, w1, jnp.where(lane == 3, w2,
                               jnp.where(lane == 4, rank1, jnp.where(lane == 5, rank2, 0.0))))))


def _merge(att, rec, cv, gl, b_gate, x2d, wa, wh, wc, wo, ffn_g, w_route, b_route, layer):
    T, D = x2d.shape
    tm = MERGE_TILE
    row = lambda i: (i, 0)
    const2 = lambda i: (0, 0)
    wsel = lambda i: (layer, 0, 0)
    W = att.shape[1]
    before = jnp.asarray(np.tril(np.ones((tm, tm), np.float32), -1), dtype=BF16)
    return pl.pallas_call(
        functools.partial(_merge_body, n_groups=N_GROUPS, per_group=EXPERTS_PER_GROUP),
        grid=(T // tm,),
        in_specs=[pl.BlockSpec((tm, W), row), pl.BlockSpec((tm, W), row), pl.BlockSpec((tm, W), row),
                  pl.BlockSpec((tm, 3 * D), row), pl.BlockSpec((1, 3 * D), const2),
                  pl.BlockSpec((tm, D), row),
                  pl.BlockSpec((None, W, D), wsel), pl.BlockSpec((None, W, D), wsel),
                  pl.BlockSpec((None, W, D), wsel), pl.BlockSpec((None, D, D), wsel),
                  pl.BlockSpec((1, D), const2), pl.BlockSpec((D, LANES), const2),
                  pl.BlockSpec((1, LANES), const2), pl.BlockSpec((tm, tm), const2)],
        out_specs=[pl.BlockSpec((tm, D), row), pl.BlockSpec((tm, LANES), row),
                   pl.BlockSpec((1, LANES), const2)],
        out_shape=[jax.ShapeDtypeStruct((T, D), F32), jax.ShapeDtypeStruct((T, LANES), F32),
                   jax.ShapeDtypeStruct((1, LANES), F32)],
        scratch_shapes=[pltpu.VMEM((tm, D), BF16), pltpu.VMEM((1, LANES), F32)],
        compiler_params=_params(1, VMEM_LIMIT),
        name="merge_route",
    )(att, rec, cv, gl, b_gate.astype(F32).reshape(1, 3 * D), x2d, wa, wh, wc, wo,
      ffn_g.astype(F32).reshape(1, D), w_route, b_route, before)


def _bulk_wait(view, sem):
    pltpu.make_async_copy(view, view, sem).wait()


def _dispatch_body(ps_ref, pn_ref, nu_ref, pos_ref, x_ref, fg_ref, xs_hbm, hbuf, zbuf, sem, zsem, *,
                   tile, n_tiles, n_experts, n_sorted_tiles):
    i = pl.program_id(0)
    slot = i % 2
    ztile = zbuf.shape[0]

    def pad_copies(e):
        out = []
        for k in range(ztile.bit_length() - 1):
            first = ps_ref[e] + ((pn_ref[e] >> (k + 1)) << (k + 1))
            copy = pltpu.make_async_copy(zbuf.at[pl.ds(0, 1 << k)], xs_hbm.at[pl.ds(first, 1 << k)], zsem)
            out.append((((pn_ref[e] >> k) & 1) == 1, copy))
        return out

    def tail_copy(t):
        return pltpu.make_async_copy(zbuf, xs_hbm.at[pl.ds(t * ztile, ztile)], zsem)

    def wait_slot(s):
        _bulk_wait(hbuf.at[s], sem.at[s])
        _bulk_wait(hbuf.at[s], sem.at[s])

    @pl.when(i == 0)
    def _():
        zbuf[...] = jnp.zeros(zbuf.shape, F32)
        for e in range(n_experts):
            for bit_set, copy in pad_copies(e):
                pl.when(bit_set)(copy.start)

        def start_tail(t, carry):
            tail_copy(t).start()
            return carry
        lax.fori_loop(nu_ref[0], n_sorted_tiles, start_tail, 0)

    def scatter_tile(s):
        @pl.when(i >= 2)
        def _():
            wait_slot(s)

        x = x_ref[...]
        h = x * lax.rsqrt(jnp.mean(x * x, axis=-1, keepdims=True) + RMS_EPS) * fg_ref[...]
        hbuf[s] = h.reshape(h.shape[0], 1, h.shape[1])

        def issue(g, carry):
            for k in range(DMA_UNROLL):
                r = g * DMA_UNROLL + k
                src = hbuf.at[s, pl.ds(r, 1)]
                pltpu.make_async_copy(src, xs_hbm.at[pl.ds(pos_ref[0, 0, r], 1)],
                                      sem.at[s]).start(priority=0)
                pltpu.make_async_copy(src, xs_hbm.at[pl.ds(pos_ref[0, 0, tile + r], 1)],
                                      sem.at[s]).start(priority=1)
            return carry

        lax.fori_loop(0, tile // DMA_UNROLL, issue, 0)

    for s in range(2):
        pl.when(slot == s)(functools.partial(scatter_tile, s))

    @pl.when(i == n_tiles - 1)
    def _():
        wait_slot((n_tiles - 1) % 2)
        if n_tiles >= 2:
            wait_slot(n_tiles % 2)
        for e in range(n_experts):
            for bit_set, copy in pad_copies(e):
                pl.when(bit_set)(copy.wait)

        def done_tail(t, carry):
            tail_copy(t).wait()
            return carry
        lax.fori_loop(nu_ref[0], n_sorted_tiles, done_tail, 0)


def _dispatch(x2d, ffn_g, pos_tiles, pad_start, pad_len, n_used, cap):
    T, D = x2d.shape
    tile = TOKEN_TILE
    nt = T // tile
    n_experts = pad_start.shape[0]
    grid_spec = pltpu.PrefetchScalarGridSpec(
        num_scalar_prefetch=3,
        grid=(nt,),
        in_specs=[pl.BlockSpec((1, 1, 2 * tile), lambda i, ps, pn, nu: (i, 0, 0), memory_space=pltpu.SMEM),
                  pl.BlockSpec((tile, D), lambda i, ps, pn, nu: (i, 0)),
                  pl.BlockSpec((1, D), lambda i, ps, pn, nu: (0, 0))],
        out_specs=pl.BlockSpec(memory_space=pl.ANY),
        scratch_shapes=[pltpu.VMEM((2, tile, 1, D), F32), pltpu.VMEM((EXPERT_TILE, 1, D), F32),
                        pltpu.SemaphoreType.DMA((2,)), pltpu.SemaphoreType.DMA(())],
    )
    return pl.pallas_call(
        functools.partial(_dispatch_body, tile=tile, n_tiles=nt, n_experts=n_experts,
                          n_sorted_tiles=cap // EXPERT_TILE),
        grid_spec=grid_spec,
        out_shape=jax.ShapeDtypeStruct((cap, 1, D), F32),
        compiler_params=_params(1, VMEM_LIMIT),
        name="moe_dispatch",
    )(pad_start, pad_len, n_used, pos_tiles, x2d, ffn_g.astype(F32).reshape(1, D))


def _expert_body(te_ref, nu_ref, nx_ref, sl_ref, x_ref, win_hbm, wout_hbm, y_ref, win_bf, wout_bf, x2_ref,
                 win_st, wout_st, wsem, *, layer):
    i = pl.program_id(0)
    ff = wout_bf.shape[0]
    used = i < nu_ref[0]
    new_expert = jnp.logical_or(i == 0, te_ref[i] != te_ref[jnp.maximum(i - 1, 0)])

    def weight_copies(e, slot):
        return (pltpu.make_async_copy(win_hbm.at[layer, e], win_st.at[slot], wsem.at[0, slot]),
                pltpu.make_async_copy(wout_hbm.at[layer, e], wout_st.at[slot], wsem.at[1, slot]))

    @pl.when(i == 0)
    def _():
        for c in weight_copies(te_ref[0], 0):
            c.start()

    def begin_run(s):
        for c in weight_copies(te_ref[i], s):
            c.wait()
        win_bf[...] = win_st[s].astype(BF16)
        wout_bf[...] = wout_st[s].astype(BF16)

        @pl.when(nx_ref[i] >= 0)
        def _():
            for c in weight_copies(nx_ref[i], 1 - s):
                c.start()

    for s in range(2):
        pl.when(jnp.logical_and(jnp.logical_and(used, new_expert), sl_ref[i] == s))(
            functools.partial(begin_run, s))

    @pl.when(used)
    def _():
        x2_ref[...] = x_ref[...].reshape(x2_ref.shape)
        gu = _dot(x2_ref[...].astype(BF16), win_bf[...])
        act = (_silu(gu[:, 0:ff]) * gu[:, ff:2 * ff]).astype(BF16)
        y = _dot(act, wout_bf[...])
        y_ref[...] = y.reshape(y_ref.shape)

    @pl.when(jnp.logical_not(used))
    def _():
        y_ref[...] = jnp.zeros(y_ref.shape, F32)


def _experts(x_sorted, w_exp_in, w_exp_out, tile_expert, n_used, next_expert, run_slot, layer):
    cap, _, D = x_sorted.shape
    tile = EXPERT_TILE
    nt = cap // tile
    ff = w_exp_out.shape[2]
    rows = lambda i, te, nu, nx, sl: (i, 0, 0)
    grid_spec = pltpu.PrefetchScalarGridSpec(
        num_scalar_prefetch=4,
        grid=(nt,),
        in_specs=[pl.BlockSpec((tile, 1, D), rows),
                  pl.BlockSpec(memory_space=pl.ANY), pl.BlockSpec(memory_space=pl.ANY)],
        out_specs=pl.BlockSpec((tile, 1, D), rows),
        scratch_shapes=[pltpu.VMEM((D, 2 * ff), BF16), pltpu.VMEM((ff, D), BF16),
                        pltpu.VMEM((tile, D), F32),
                        pltpu.VMEM((2, D, 2 * ff), F32), pltpu.VMEM((2, ff, D), F32),
                        pltpu.SemaphoreType.DMA((2, 2))],
    )
    return pl.pallas_call(
        functools.partial(_expert_body, layer=layer),
        grid_spec=grid_spec,
        out_shape=jax.ShapeDtypeStruct((cap, 1, D), F32),
        compiler_params=_params(1, VMEM_LIMIT),
        name="expert_ffn",
    )(tile_expert, n_used, next_expert, run_slot, x_sorted, w_exp_in, w_exp_out)


def _combine_body(pos0_ref, posn_ref, y_hbm, x_ref, route_ref, o_ref, ybuf, y2_ref, sem, *, tile, n_tiles):
    i = pl.program_id(0)
    slot = i % 2

    def gather(pos_ref, s):
        def issue(g, carry):
            for k in range(DMA_UNROLL):
                r = g * DMA_UNROLL + k
                pltpu.make_async_copy(y_hbm.at[pl.ds(pos_ref[0, 0, r], 1)],
                                      ybuf.at[s, pl.ds(r, 1)], sem.at[s]).start(priority=k % 2)
            return carry
        lax.fori_loop(0, 2 * tile // DMA_UNROLL, issue, 0)

    @pl.when(i == 0)
    def _():
        gather(pos0_ref, 0)

    def combine_tile(s):
        @pl.when(i + 1 < n_tiles)
        def _():
            gather(posn_ref, 1 - s)

        _bulk_wait(ybuf.at[s], sem.at[s])
        y2_ref[...] = ybuf[s].reshape(y2_ref.shape)
        route = route_ref[...]
        o_ref[...] = (x_ref[...] + route[:, 2:3] * y2_ref[0:tile, :]
                      + route[:, 3:4] * y2_ref[tile:2 * tile, :])

    for s in range(2):
        pl.when(slot == s)(functools.partial(combine_tile, s))


def _combine(x2d, route, y_sorted, pos_tiles):
    T, D = x2d.shape
    tile = TOKEN_TILE
    nt = T // tile
    return pl.pallas_call(
        functools.partial(_combine_body, tile=tile, n_tiles=nt),
        grid=(nt,),
        in_specs=[pl.BlockSpec((1, 1, 2 * tile), lambda i: (0, 0, 0), memory_space=pltpu.SMEM),
                  pl.BlockSpec((1, 1, 2 * tile), lambda i: (jnp.minimum(i + 1, nt - 1), 0, 0),
                               memory_space=pltpu.SMEM),
                  pl.BlockSpec(memory_space=pl.ANY),
                  pl.BlockSpec((tile, D), lambda i: (i, 0)),
                  pl.BlockSpec((tile, LANES), lambda i: (i, 0))],
        out_specs=pl.BlockSpec((tile, D), lambda i: (i, 0)),
        out_shape=jax.ShapeDtypeStruct((T, D), F32),
        scratch_shapes=[pltpu.VMEM((2, 2 * tile, 1, D), F32), pltpu.VMEM((2 * tile, D), F32),
                        pltpu.SemaphoreType.DMA((2,))],
        compiler_params=_params(1, VMEM_LIMIT),
        name="moe_combine",
    )(pos_tiles, pos_tiles, y_sorted, x2d, route)


def _positions_body(route_ref, offs_ref, pos_ref, *, tile):
    r = route_ref[...]
    lane = lax.broadcasted_iota(jnp.int32, r.shape, 1)
    lane_f = lane.astype(F32)
    offs = offs_ref[...]
    pos1 = jnp.sum(jnp.where(lane_f == r[:, 0:1], offs, 0.0), axis=-1, keepdims=True) + r[:, 4:5]
    pos2 = jnp.sum(jnp.where(lane_f == r[:, 1:2], offs, 0.0), axis=-1, keepdims=True) + r[:, 5:6]
    both = jnp.where(lane == 0, pos1, jnp.where(lane == 1, pos2, 0.0))
    for k in range(r.shape[0] // tile):
        t = both[k * tile:(k + 1) * tile, :].T
        pos_ref[k, :, 0:tile] = t[0:1, :].astype(jnp.int32)
        pos_ref[k, :, tile:2 * tile] = t[1:2, :].astype(jnp.int32)


def _positions(route, offs_row):
    T = route.shape[0]
    tile = TOKEN_TILE
    step = 4 * tile
    return pl.pallas_call(
        functools.partial(_positions_body, tile=tile),
        grid=(T // step,),
        in_specs=[pl.BlockSpec((step, LANES), lambda i: (i, 0)),
                  pl.BlockSpec((1, LANES), lambda i: (0, 0))],
        out_specs=pl.BlockSpec((step // tile, 1, 2 * tile), lambda i: (i, 0, 0)),
        out_shape=jax.ShapeDtypeStruct((T // tile, 1, 2 * tile), jnp.int32),
        compiler_params=_params(1, VMEM_LIMIT),
        name="moe_positions",
    )(route, offs_row)


def _dispatch_plan(route, counts, n_experts, tile, cap):
    cnt = counts[0, :n_experts].astype(jnp.int32)
    padded = ((cnt + tile - 1) // tile) * tile
    ends = jnp.cumsum(padded)
    offs = ends - padded
    nt = cap // tile
    n_used = (ends[-1] // tile).astype(jnp.int32)
    first_row = jnp.minimum(jnp.arange(nt, dtype=jnp.int32), n_used - 1) * tile
    te = jnp.sum((ends[None, :] <= first_row[:, None]).astype(jnp.int32), axis=1)
    te = jnp.minimum(te, n_experts - 1).astype(jnp.int32)
    run_start = jnp.concatenate([jnp.ones((1,), bool), te[1:] != te[:-1]])
    run_slot = ((jnp.cumsum(run_start.astype(jnp.int32)) - 1) % 2).astype(jnp.int32)
    ids = jnp.arange(n_experts, dtype=jnp.int32)
    later = (ids[None, :] > ids[:, None]) & (cnt > 0)[None, :]
    following = jnp.min(jnp.where(later, ids[None, :], n_experts), axis=1)
    following = jnp.where(following >= n_experts, -1, following).astype(jnp.int32)
    next_expert = jnp.sum(jnp.where(te[:, None] == ids[None, :], following[None, :], 0), axis=1).astype(jnp.int32)
    offs_row = jnp.pad(offs.astype(F32), (0, LANES - n_experts)).reshape(1, LANES)
    pos_tiles = _positions(route, offs_row)
    return (te, n_used.reshape(1), pos_tiles, (offs + cnt).astype(jnp.int32), (padded - cnt).astype(jnp.int32),
            next_expert, run_slot)


def kernel(x, mix_norm_g, w_in, b_gate, q_norm_g, k_norm_g, lb_logits, hgrn_norm_g, conv_w, conv_b,
           conv_ln_g, conv_ln_b, w_att_o, w_hgrn_o, w_conv_o, w_out, ffn_norm_g, w_coarse, b_coarse,
           w_fine, b_fine, w_exp_in, w_exp_out):
    B, S, D = x.shape
    L = w_in.shape[0]
    T = B * S
    n_experts = w_exp_in.shape[1]
    cap = 2 * T + n_experts * EXPERT_TILE

    w_in_bf = w_in.astype(BF16)
    wa_bf, wh_bf, wc_bf, wo_bf = (w.astype(BF16) for w in (w_att_o, w_hgrn_o, w_conv_o, w_out))
    pad = LANES - N_GROUPS - n_experts
    w_route = jnp.concatenate([w_coarse, w_fine, jnp.zeros((L, D, pad), F32)], axis=-1).astype(F32)
    b_route = jnp.concatenate([b_coarse, b_fine, jnp.zeros((L, pad), F32)], axis=-1).astype(F32)

    x2d = x.reshape(T, D)
    for l in range(L):
        qkv, hf, hqig, cab, gl = _inproj(x2d, mix_norm_g[l], w_in_bf, l)
        att = _attention(qkv, q_norm_g[l], k_norm_g[l], B, S)
        rec = _hgrn(hqig, hf, lb_logits, hgrn_norm_g[l], l, B, S)
        cv = _conv(cab, conv_w[l], conv_b[l], conv_ln_g[l], conv_ln_b[l], B, S)
        x_mid, route, counts = _merge(att, rec, cv, gl, b_gate[l], x2d, wa_bf, wh_bf, wc_bf, wo_bf,
                                      ffn_norm_g[l], w_route[l], b_route[l].reshape(1, LANES), l)
        te, n_used, pos_tiles, pad_start, pad_len, next_expert, run_slot = _dispatch_plan(
            route, counts, n_experts, EXPERT_TILE, cap)
        x_sorted = _dispatch(x_mid, ffn_norm_g[l], pos_tiles, pad_start, pad_len, n_used, cap)
        y_sorted = _experts(x_sorted, w_exp_in, w_exp_out, te, n_used, next_expert, run_slot, l)
        x2d = _combine(x_mid, route, y_sorted, pos_tiles)
    return x2d.reshape(B, S, D)
```

```python
import functools

import jax
import jax.numpy as jnp
import numpy as np
from jax import lax
from jax.experimental import pallas as pl
from jax.experimental.pallas import tpu as pltpu

F32 = jnp.float32
BF16 = jnp.bfloat16

LANES = 128
SUBLANES = 8
RMS_EPS = 1e-6
LN_EPS = 1e-5
NEG_INF = -1e30
LOG2_E = 1.4426950408889634

ATT_HEADS = 8
ATT_HEAD_DIM = 64
MOBA_BLOCK = 256
MOBA_TOPK = 3
HGRN_HEADS = 4
HGRN_DIM = 128
HGRN_ROWS = 256
HGRN_STEP_CHUNKS = 4
CONV_WIDTH = 31
CONV_ROWS = 512
CONV_HALO = 32
N_GROUPS = 4
EXPERTS_PER_GROUP = 8
EXPERT_TILE = 512
TOKEN_TILE = 512
INPROJ_TILE = 512
MERGE_TILE = 512
DMA_UNROLL = 8
VMEM_LIMIT = 56 * 1024 * 1024


def _params(n_axes, vmem=None):
    return pltpu.CompilerParams(dimension_semantics=("arbitrary",) * n_axes,
                                vmem_limit_bytes=vmem)


def _dot(a, b):
    return jnp.dot(a, b, preferred_element_type=F32)


def _dot_nt(a, b):
    return lax.dot_general(a, b, (((1,), (1,)), ((), ())), preferred_element_type=F32)


def _dot_tn(a, b):
    return lax.dot_general(a, b, (((0,), (0,)), ((), ())), preferred_element_type=F32)


def _split3(x):
    x1 = x.astype(BF16)
    r1 = x - x1.astype(F32)
    x2 = r1.astype(BF16)
    x3 = (r1 - x2.astype(F32)).astype(BF16)
    return x1, x2, x3


def _sigmoid(x):
    return 1.0 / (1.0 + jnp.exp(-x))


def _silu(x):
    return x * _sigmoid(x)


def _inproj_body(x_ref, g_ref, w_ref, qkv_ref, hf_ref, hqig_ref, cv_ref, gl_ref, *, segs, chunk):
    x = x_ref[...]
    ms = jnp.mean(x * x, axis=-1, keepdims=True)
    h = (x * lax.rsqrt(ms + RMS_EPS) * g_ref[...]).astype(BF16)
    refs = (qkv_ref, hf_ref, hqig_ref, cv_ref, gl_ref)
    for ridx, dst0, src0, width in segs:
        ref = refs[ridx]
        for c in range(0, width, chunk):
            y = _dot(h, w_ref[:, src0 + c:src0 + c + chunk])
            ref[:, dst0 + c:dst0 + c + chunk] = y.astype(ref.dtype)


def _inproj(x2d, norm_g, w_in_bf, layer):
    T, D = x2d.shape
    A = ATT_HEADS * ATT_HEAD_DIM
    K = HGRN_HEADS * HGRN_DIM
    C = A
    segs = ((0, 0, 0, 3 * A),
            (2, 0, 3 * A, K),
            (1, 0, 3 * A + K, K),
            (2, K, 3 * A + 2 * K, K),
            (2, 2 * K, 3 * A + 3 * K, K),
            (3, 0, 3 * A + 4 * K, 2 * C),
            (4, 0, 3 * A + 4 * K + 2 * C, 3 * D))
    ncols = w_in_bf.shape[-1]
    tm = INPROJ_TILE
    row = lambda i: (i, 0)
    return pl.pallas_call(
        functools.partial(_inproj_body, segs=segs, chunk=512),
        grid=(T // tm,),
        in_specs=[pl.BlockSpec((tm, D), row),
                  pl.BlockSpec((1, D), lambda i: (0, 0)),
                  pl.BlockSpec((None, D, ncols), lambda i: (layer, 0, 0))],
        out_specs=[pl.BlockSpec((tm, 3 * A), row), pl.BlockSpec((tm, K), row),
                   pl.BlockSpec((tm, 3 * K), row), pl.BlockSpec((tm, 2 * C), row),
                   pl.BlockSpec((tm, 3 * D), row)],
        out_shape=[jax.ShapeDtypeStruct((T, 3 * A), BF16), jax.ShapeDtypeStruct((T, K), F32),
                   jax.ShapeDtypeStruct((T, 3 * K), BF16), jax.ShapeDtypeStruct((T, 2 * C), BF16),
                   jax.ShapeDtypeStruct((T, 3 * D), BF16)],
        compiler_params=_params(1, VMEM_LIMIT),
        name="inproj",
    )(x2d, norm_g.reshape(1, D), w_in_bf)


def _attn_body(q_ref, k_ref, v_ref, gq_ref, gk_ref, o_ref, kb_ref, vt_ref, s_ref, *, seq, blk, topk):
    nb = seq // blk
    dh = ATT_HEAD_DIM
    dh_sh = dh.bit_length() - 1
    scale = dh ** -0.5 * LOG2_E
    lane = lax.broadcasted_iota(jnp.int32, (1, LANES), 1)
    hr = lax.broadcasted_iota(jnp.int32, (LANES, LANES), 0) >> dh_sh
    hc = lax.broadcasted_iota(jnp.int32, (LANES, LANES), 1) >> dh_sh
    same_head = jnp.where(hr == hc, 1.0, 0.0).astype(BF16)

    def head_norm(x, g):
        x2 = x * x
        hi = x2.astype(BF16)
        lo = (x2 - hi.astype(F32)).astype(BF16)
        ssq = _dot(hi, same_head) + _dot(lo, same_head)
        return x * lax.rsqrt(ssq * (1.0 / dh) + RMS_EPS) * g

    kmeans = []
    for j in range(nb):
        kn = head_norm(k_ref[j * blk:(j + 1) * blk, :].astype(F32), gk_ref[...])
        kb_ref[j * blk:(j + 1) * blk, :] = kn.astype(BF16)
        kmeans.append(jnp.mean(kn, axis=0, keepdims=True))
        vt_ref[:, j * blk:(j + 1) * blk] = v_ref[j * blk:(j + 1) * blk, :].astype(F32).T.astype(BF16)
    kmean = jnp.concatenate(kmeans, axis=0)
    km_hi = kmean.astype(BF16)
    km_lo = (kmean - km_hi.astype(F32)).astype(BF16)
    first_head = lane < dh
    first_head_t = lax.broadcasted_iota(jnp.int32, (LANES, 1), 0) < dh
    blk_n = lax.broadcasted_iota(jnp.int32, (nb, 1), 0)

    key_l = lax.broadcasted_iota(jnp.int32, (blk, 2 * blk), 0)
    qry_l = lax.broadcasted_iota(jnp.int32, (blk, 2 * blk), 1) & (blk - 1)
    causal = key_l <= qry_l

    for i in range(nb):
        rows = head_norm(q_ref[i * blk:(i + 1) * blk, :].astype(F32), gq_ref[...])
        qi = jnp.concatenate([jnp.where(first_head, rows, 0.0), jnp.where(first_head, 0.0, rows)], axis=0)
        qs = (qi * scale).astype(BF16)

        sel = None
        if i > topk:
            q_hi = qi.astype(BF16)
            q_lo = (qi - q_hi.astype(F32)).astype(BF16)
            gate = _dot_nt(km_hi, q_hi) + _dot_nt(km_lo, q_hi) + _dot_nt(km_hi, q_lo)
            valid = blk_n < i
            gm = jnp.where(valid, gate, NEG_INF)
            rank = jnp.zeros(gm.shape, F32)
            for r in range(1, nb):
                gr = pltpu.roll(gm, r, 0)
                ahead = (gr > gm) | ((gr == gm) & (blk_n >= r))
                rank = rank + jnp.where(ahead, 1.0, 0.0)
            sel = jnp.where(valid & (rank < topk), 1.0, 0.0)

        m = None
        for j in range(i + 1):
            s = _dot_nt(kb_ref[j * blk:(j + 1) * blk, :], qs)
            if j == i:
                s = jnp.where(causal, s, NEG_INF)
            elif sel is not None:
                s = jnp.where(sel[j:j + 1, :] > 0.5, s, NEG_INF)
            s_ref[j * blk:(j + 1) * blk, :] = s
            mj = jnp.max(s, axis=0, keepdims=True)
            m = mj if m is None else jnp.maximum(m, mj)

        l = None
        acc = None
        for j in range(i + 1):
            p = jnp.exp2(s_ref[j * blk:(j + 1) * blk, :] - m)
            lj = jnp.sum(p, axis=0, keepdims=True)
            aj = _dot(vt_ref[:, j * blk:(j + 1) * blk], p.astype(BF16))
            l = lj if l is None else l + lj
            acc = aj if acc is None else acc + aj

        ot = acc / l
        merged = jnp.where(first_head_t, ot[:, 0:blk], ot[:, blk:2 * blk])
        o_ref[i * blk:(i + 1) * blk, :] = merged.T.astype(o_ref.dtype)


def _attention(qkv, q_norm_g, k_norm_g, batch, seq):
    T = qkv.shape[0]
    A = ATT_HEADS * ATT_HEAD_DIM
    npair = A // LANES
    reps = LANES // ATT_HEAD_DIM
    gq = jnp.tile(q_norm_g.astype(F32), reps).reshape(1, LANES)
    gk = jnp.tile(k_norm_g.astype(F32), reps).reshape(1, LANES)
    blk_spec = lambda off: pl.BlockSpec((seq, LANES), lambda b, p: (b, off + p))
    vec = pl.BlockSpec((1, LANES), lambda b, p: (0, 0))
    return pl.pallas_call(
        functools.partial(_attn_body, seq=seq, blk=MOBA_BLOCK, topk=MOBA_TOPK),
        grid=(batch, npair),
        in_specs=[blk_spec(0), blk_spec(npair), blk_spec(2 * npair), vec, vec],
        out_specs=pl.BlockSpec((seq, LANES), lambda b, p: (b, p)),
        out_shape=jax.ShapeDtypeStruct((T, A), BF16),
        scratch_shapes=[pltpu.VMEM((seq, LANES), BF16), pltpu.VMEM((LANES, seq), BF16),
                        pltpu.VMEM((seq, 2 * MOBA_BLOCK), F32)],
        compiler_params=_params(2, VMEM_LIMIT),
        name="moba_attn",
    )(qkv, qkv, qkv, gq, gk)


def _hgrn_head(q_ref, f_ref, i_ref, g_ref, lbl_ref, ng_ref, tri_ref, lvl_ref, o_ref, st_ref, *,
               rows, layer, head, row0):
    R = rows
    cols = slice(head * HGRN_DIM, (head + 1) * HGRN_DIM)
    rws = slice(row0, row0 + R)

    lbl = lbl_ref[:, cols]
    e = jnp.exp(lbl - jnp.max(lbl, axis=0, keepdims=True))
    p = e / jnp.sum(e, axis=0, keepdims=True)
    lb = jnp.maximum(jnp.sum(p[0:layer + 1], axis=0, keepdims=True) - p[0:1], 0.0)

    sig = _sigmoid(f_ref[rws, cols])
    g = jnp.log(lb + (1.0 - lb) * sig)
    kin = (1.0 - lb) * (1.0 - sig)
    qa = _silu(q_ref[rws, cols].astype(F32))
    vb = i_ref[rws, cols]

    t_col = lax.broadcasted_iota(jnp.int32, (R, 1), 0)

    tri = tri_ref[...]
    g1, g2, g3 = _split3(g)
    b = (_dot(tri, g1) + _dot(tri, g2) + _dot(tri, g3)) * LOG2_E

    half = R // 2
    scores = [None, None]
    cross = None
    m = half
    while m >= 1:
        w = 2 * m
        if w >= SUBLANES:
            b3 = b.reshape(R // w, w, LANES)
            bref = jnp.broadcast_to(b3[:, m - 1:m, :], b3.shape).reshape(R, LANES)
        else:
            tl = t_col & (w - 1)
            bref = None
            for resid in range(w):
                shift = resid - (m - 1)
                cand = b if shift == 0 else pltpu.roll(b, shift % R, 0)
                bref = cand if bref is None else jnp.where(tl == resid, cand, bref)
        e = jnp.exp2(-jnp.abs(b - bref))
        qt = (qa * e).astype(BF16)
        kt = (kin * e).astype(BF16)
        if w == R:
            cross = _dot_nt(qt[half:R], kt[0:half])
        else:
            keep = lvl_ref[...] == (m.bit_length() - 1)
            for hh in range(2):
                part = _dot_nt(qt[hh * half:(hh + 1) * half], kt[hh * half:(hh + 1) * half])
                scores[hh] = jnp.where(keep, part, 0.0 if scores[hh] is None else scores[hh])
        m //= 2

    st = st_ref[head]
    o = _dot_nt((qa * jnp.exp2(b)).astype(BF16), st.astype(BF16))
    o = o + jnp.sum(qa * kin, axis=-1, keepdims=True) * vb.astype(F32)
    o_lo = o[0:half] + _dot(scores[0].astype(BF16), vb[0:half])
    o_hi = o[half:R] + _dot(scores[1].astype(BF16), vb[half:R]) + _dot(cross.astype(BF16), vb[0:half])
    o = jnp.concatenate([o_lo, o_hi], axis=0)

    b_end = b[R - 1:R, :]
    k_end = (kin * jnp.exp2(b_end - b)).astype(BF16)
    st_ref[head] = st * jnp.exp2(b_end) + _dot_tn(vb, k_end)

    on = o * lax.rsqrt(jnp.mean(o * o, axis=-1, keepdims=True) + RMS_EPS) * ng_ref[...]
    o_ref[rws, cols] = (on * _silu(g_ref[rws, cols].astype(F32))).astype(o_ref.dtype)


def _hgrn_body(q_ref, f_ref, i_ref, g_ref, lbl_ref, ng_ref, tri_ref, lvl_ref, o_ref, st_ref, *, rows, layer):
    @pl.when(pl.program_id(1) == 0)
    def _():
        st_ref[...] = jnp.zeros(st_ref.shape, F32)

    for chunk in range(q_ref.shape[0] // rows):
        for head in range(st_ref.shape[0]):
            _hgrn_head(q_ref, f_ref, i_ref, g_ref, lbl_ref, ng_ref, tri_ref, lvl_ref, o_ref, st_ref,
                       rows=rows, layer=layer, head=head, row0=chunk * rows)


def _level_table(n):
    t = np.arange(n, dtype=np.int32)[:, None]
    s = np.arange(n, dtype=np.int32)[None, :]
    x = t ^ s
    hb = np.zeros((n, n), np.int32)
    for k in range(1, n.bit_length()):
        hb = hb + (x >= (1 << k)).astype(np.int32)
    return jnp.asarray(np.where(t > s, hb, -1).astype(np.int32))


def _hgrn(hqig, hf, lb_logits, norm_g, layer, batch, seq):
    T = hf.shape[0]
    H = HGRN_HEADS
    R = HGRN_ROWS
    W = H * HGRN_DIM
    nc = seq // (R * HGRN_STEP_CHUNKS)
    L = lb_logits.shape[0]
    blk = lambda off: pl.BlockSpec((R * HGRN_STEP_CHUNKS, W), lambda b, c: (b * nc + c, off))
    const = lambda b, c: (0, 0)
    tri = jnp.asarray(np.tril(np.ones((R, R), np.float32)), dtype=BF16)
    return pl.pallas_call(
        functools.partial(_hgrn_body, rows=R, layer=layer),
        grid=(batch, nc),
        in_specs=[blk(0), blk(0), blk(1), blk(2),
                  pl.BlockSpec((L, W), const), pl.BlockSpec((1, LANES), const),
                  pl.BlockSpec((R, R), const), pl.BlockSpec((R // 2, R // 2), const)],
        out_specs=blk(0),
        out_shape=jax.ShapeDtypeStruct((T, W), BF16),
        scratch_shapes=[pltpu.VMEM((H, HGRN_DIM, HGRN_DIM), F32)],
        compiler_params=_params(2, VMEM_LIMIT),
        name="hgrn2",
    )(hqig, hf, hqig, hqig, lb_logits.astype(F32), norm_g.astype(F32).reshape(1, LANES), tri,
      _level_table(R // 2))


def _conv_body(ab_ref, w_ref, bdw_ref, lg_ref, lb_ref, o_ref, ubuf, cbuf, ush, *, rows, width, chans):
    R, W, C, H = rows, width, chans, CONV_HALO
    c = pl.program_id(1)

    @pl.when(c == 0)
    def _():
        ubuf[0:H, :] = jnp.zeros((H, C), F32)

    a = ab_ref[:, 0:C].astype(F32)
    gate = ab_ref[:, C:2 * C].astype(F32)
    ubuf[H:H + R, :] = a * _sigmoid(gate)

    n_sh = ush.shape[1]
    for s in range(1, SUBLANES):
        ush[s - 1] = ubuf[s:s + n_sh, :]

    rsub = 64
    for lb in range(0, C, LANES):
        for rb in range(0, R, rsub):
            acc = jnp.zeros((rsub, LANES), F32)
            for j in range(W):
                start = H - (W - 1) + j + rb
                s = start % SUBLANES
                a0 = start - s
                if s == 0:
                    u = ubuf[a0:a0 + rsub, lb:lb + LANES]
                else:
                    u = ush[s - 1, a0:a0 + rsub, lb:lb + LANES]
                acc = acc + w_ref[j:j + 1, lb:lb + LANES] * u
            cbuf[rb:rb + rsub, lb:lb + LANES] = acc

    uf = cbuf[...] + bdw_ref[...]
    mu = jnp.mean(uf, axis=-1, keepdims=True)
    d = uf - mu
    var = jnp.mean(d * d, axis=-1, keepdims=True)
    y = d * lax.rsqrt(var + LN_EPS) * lg_ref[...] + lb_ref[...]
    o_ref[...] = _silu(y).astype(o_ref.dtype)
    ubuf[0:H, :] = ubuf[R:R + H, :]


def _conv(cab, conv_w, conv_b, ln_g, ln_b, batch, seq):
    T = cab.shape[0]
    C = cab.shape[1] // 2
    W = conv_w.shape[0]
    R = CONV_ROWS
    nc = seq // R
    vec = pl.BlockSpec((1, C), lambda b, c: (0, 0))
    return pl.pallas_call(
        functools.partial(_conv_body, rows=R, width=W, chans=C),
        grid=(batch, nc),
        in_specs=[pl.BlockSpec((R, 2 * C), lambda b, c: (b * nc + c, 0)),
                  pl.BlockSpec((W, C), lambda b, c: (0, 0)), vec, vec, vec],
        out_specs=pl.BlockSpec((R, C), lambda b, c: (b * nc + c, 0)),
        out_shape=jax.ShapeDtypeStruct((T, C), BF16),
        scratch_shapes=[pltpu.VMEM((R + CONV_HALO, C), F32), pltpu.VMEM((R, C), F32),
                        pltpu.VMEM((SUBLANES - 1, R + CONV_HALO - SUBLANES, C), F32)],
        compiler_params=_params(2, VMEM_LIMIT),
        name="conformer_conv",
    )(cab, conv_w.astype(F32), conv_b.astype(F32).reshape(1, C), ln_g.astype(F32).reshape(1, C),
      ln_b.astype(F32).reshape(1, C))


def _merge_body(att_ref, rec_ref, cv_ref, gl_ref, bg_ref, x_ref, wa_ref, wh_ref, wc_ref, wo_ref,
                fg_ref, wr_ref, br_ref, before_ref, xo_ref, route_ref, cnt_out_ref, mg_ref, cnt_ref, *,
                n_groups, per_group):
    D = x_ref.shape[1]
    att, rec, cv = att_ref[...], rec_ref[...], cv_ref[...]
    cw = 256
    for c in range(0, D, cw):
        def gate(k):
            return _sigmoid(gl_ref[:, k * D + c:k * D + c + cw].astype(F32) + bg_ref[:, k * D + c:k * D + c + cw])
        merged = (gate(0) * _dot(att, wa_ref[:, c:c + cw]) + gate(1) * _dot(rec, wh_ref[:, c:c + cw])
                  + gate(2) * _dot(cv, wc_ref[:, c:c + cw]))
        mg_ref[:, c:c + cw] = merged.astype(BF16)
    xn = x_ref[...] + _dot(mg_ref[...], wo_ref[...])
    xo_ref[...] = xn

    h = xn * lax.rsqrt(jnp.mean(xn * xn, axis=-1, keepdims=True) + RMS_EPS) * fg_ref[...]
    h_hi = h.astype(BF16)
    h_lo = (h - h_hi.astype(F32)).astype(BF16)
    wr = wr_ref[...]
    w_hi = wr.astype(BF16)
    w_lo = (wr - w_hi.astype(F32)).astype(BF16)
    logits = _dot(h_hi, w_hi) + _dot(h_hi, w_lo) + _dot(h_lo, w_hi) + br_ref[...]

    G, E = n_groups, per_group
    lane = lax.broadcasted_iota(jnp.int32, logits.shape, 1)
    big = jnp.int32(LANES)
    is_c = lane < G
    cl = jnp.where(is_c, logits, NEG_INF)
    cm = jnp.max(cl, axis=-1, keepdims=True)
    grp = jnp.min(jnp.where(cl == cm, lane, big), axis=-1, keepdims=True)
    se = jnp.sum(jnp.exp(jnp.where(is_c, logits - cm, NEG_INF)), axis=-1, keepdims=True)
    p_top = 1.0 / se
    lo = G + grp * E
    fl = jnp.where((lane >= lo) & (lane < lo + E), logits, NEG_INF)
    m1 = jnp.max(fl, axis=-1, keepdims=True)
    i1 = jnp.min(jnp.where(fl == m1, lane, big), axis=-1, keepdims=True)
    fl2 = jnp.where(lane == i1, NEG_INF, fl)
    m2 = jnp.max(fl2, axis=-1, keepdims=True)
    i2 = jnp.min(jnp.where(fl2 == m2, lane, big), axis=-1, keepdims=True)
    t = jnp.exp(m2 - m1)
    w1 = p_top / (1.0 + t)
    w2 = p_top * t / (1.0 + t)
    e1 = (i1 - G).astype(F32)
    e2 = (i2 - G).astype(F32)

    @pl.when(pl.program_id(0) == 0)
    def _():
        cnt_ref[...] = jnp.zeros(cnt_ref.shape, F32)

    hit1 = lane == (i1 - G)
    hit2 = lane == (i2 - G)
    hits = jnp.where(hit1 | hit2, 1.0, 0.0)
    prefix = _dot(before_ref[...], hits.astype(BF16)) + cnt_ref[...]
    rank1 = jnp.sum(jnp.where(hit1, prefix, 0.0), axis=-1, keepdims=True)
    rank2 = jnp.sum(jnp.where(hit2, prefix, 0.0), axis=-1, keepdims=True)
    cnt = cnt_ref[...] + jnp.sum(hits, axis=0, keepdims=True)
    cnt_ref[...] = cnt
    cnt_out_ref[...] = cnt

    route_ref[...] = jnp.where(lane == 0, e1, jnp.where(lane == 1, e2,
                               jnp.where(lane == 2, w1, jnp.where(lane == 3, w2,
                               jnp.where(lane == 4, rank1, jnp.where(lane == 5, rank2, 0.0))))))


def _merge(att, rec, cv, gl, b_gate, x2d, wa, wh, wc, wo, ffn_g, w_route, b_route, layer):
    T, D = x2d.shape
    tm = MERGE_TILE
    row = lambda i: (i, 0)
    const2 = lambda i: (0, 0)
    wsel = lambda i: (layer, 0, 0)
    W = att.shape[1]
    before = jnp.asarray(np.tril(np.ones((tm, tm), np.float32), -1), dtype=BF16)
    return pl.pallas_call(
        functools.partial(_merge_body, n_groups=N_GROUPS, per_group=EXPERTS_PER_GROUP),
        grid=(T // tm,),
        in_specs=[pl.BlockSpec((tm, W), row), pl.BlockSpec((tm, W), row), pl.BlockSpec((tm, W), row),
                  pl.BlockSpec((tm, 3 * D), row), pl.BlockSpec((1, 3 * D), const2),
                  pl.BlockSpec((tm, D), row),
                  pl.BlockSpec((None, W, D), wsel), pl.BlockSpec((None, W, D), wsel),
                  pl.BlockSpec((None, W, D), wsel), pl.BlockSpec((None, D, D), wsel),
                  pl.BlockSpec((1, D), const2), pl.BlockSpec((D, LANES), const2),
                  pl.BlockSpec((1, LANES), const2), pl.BlockSpec((tm, tm), const2)],
        out_specs=[pl.BlockSpec((tm, D), row), pl.BlockSpec((tm, LANES), row),
                   pl.BlockSpec((1, LANES), const2)],
        out_shape=[jax.ShapeDtypeStruct((T, D), F32), jax.ShapeDtypeStruct((T, LANES), F32),
                   jax.ShapeDtypeStruct((1, LANES), F32)],
        scratch_shapes=[pltpu.VMEM((tm, D), BF16), pltpu.VMEM((1, LANES), F32)],
        compiler_params=_params(1, VMEM_LIMIT),
        name="merge_route",
    )(att, rec, cv, gl, b_gate.astype(F32).reshape(1, 3 * D), x2d, wa, wh, wc, wo,
      ffn_g.astype(F32).reshape(1, D), w_route, b_route, before)


def _bulk_wait(view, sem):
    pltpu.make_async_copy(view, view, sem).wait()


def _dispatch_body(ps_ref, pn_ref, nu_ref, pos_ref, x_ref, fg_ref, xs_hbm, hbuf, zbuf, sem, zsem, *,
                   tile, n_tiles, n_experts, n_sorted_tiles):
    i = pl.program_id(0)
    slot = i % 2
    ztile = zbuf.shape[0]

    def pad_copies(e):
        out = []
        for k in range(ztile.bit_length() - 1):
            first = ps_ref[e] + ((pn_ref[e] >> (k + 1)) << (k + 1))
            copy = pltpu.make_async_copy(zbuf.at[pl.ds(0, 1 << k)], xs_hbm.at[pl.ds(first, 1 << k)], zsem)
            out.append((((pn_ref[e] >> k) & 1) == 1, copy))
        return out

    def tail_copy(t):
        return pltpu.make_async_copy(zbuf, xs_hbm.at[pl.ds(t * ztile, ztile)], zsem)

    def wait_slot(s):
        _bulk_wait(hbuf.at[s], sem.at[s])
        _bulk_wait(hbuf.at[s], sem.at[s])

    @pl.when(i == 0)
    def _():
        zbuf[...] = jnp.zeros(zbuf.shape, F32)
        for e in range(n_experts):
            for bit_set, copy in pad_copies(e):
                pl.when(bit_set)(copy.start)

        def start_tail(t, carry):
            tail_copy(t).start()
            return carry
        lax.fori_loop(nu_ref[0], n_sorted_tiles, start_tail, 0)

    def scatter_tile(s):
        @pl.when(i >= 2)
        def _():
            wait_slot(s)

        x = x_ref[...]
        h = x * lax.rsqrt(jnp.mean(x * x, axis=-1, keepdims=True) + RMS_EPS) * fg_ref[...]
        hbuf[s] = h.reshape(h.shape[0], 1, h.shape[1])

        def issue(g, carry):
            for k in range(DMA_UNROLL):
                r = g * DMA_UNROLL + k
                src = hbuf.at[s, pl.ds(r, 1)]
                pltpu.make_async_copy(src, xs_hbm.at[pl.ds(pos_ref[0, 0, r], 1)],
                                      sem.at[s]).start(priority=0)
                pltpu.make_async_copy(src, xs_hbm.at[pl.ds(pos_ref[0, 0, tile + r], 1)],
                                      sem.at[s]).start(priority=1)
            return carry

        lax.fori_loop(0, tile // DMA_UNROLL, issue, 0)

    for s in range(2):
        pl.when(slot == s)(functools.partial(scatter_tile, s))

    @pl.when(i == n_tiles - 1)
    def _():
        wait_slot((n_tiles - 1) % 2)
        if n_tiles >= 2:
            wait_slot(n_tiles % 2)
        for e in range(n_experts):
            for bit_set, copy in pad_copies(e):
                pl.when(bit_set)(copy.wait)

        def done_tail(t, carry):
            tail_copy(t).wait()
            return carry
        lax.fori_loop(nu_ref[0], n_sorted_tiles, done_tail, 0)


def _dispatch(x2d, ffn_g, pos_tiles, pad_start, pad_len, n_used, cap):
    T, D = x2d.shape
    tile = TOKEN_TILE
    nt = T // tile
    n_experts = pad_start.shape[0]
    grid_spec = pltpu.PrefetchScalarGridSpec(
        num_scalar_prefetch=3,
        grid=(nt,),
        in_specs=[pl.BlockSpec((1, 1, 2 * tile), lambda i, ps, pn, nu: (i, 0, 0), memory_space=pltpu.SMEM),
                  pl.BlockSpec((tile, D), lambda i, ps, pn, nu: (i, 0)),
                  pl.BlockSpec((1, D), lambda i, ps, pn, nu: (0, 0))],
        out_specs=pl.BlockSpec(memory_space=pl.ANY),
        scratch_shapes=[pltpu.VMEM((2, tile, 1, D), F32), pltpu.VMEM((EXPERT_TILE, 1, D), F32),
                        pltpu.SemaphoreType.DMA((2,)), pltpu.SemaphoreType.DMA(())],
    )
    return pl.pallas_call(
        functools.partial(_dispatch_body, tile=tile, n_tiles=nt, n_experts=n_experts,
                          n_sorted_tiles=cap // EXPERT_TILE),
        grid_spec=grid_spec,
        out_shape=jax.ShapeDtypeStruct((cap, 1, D), F32),
        compiler_params=_params(1, VMEM_LIMIT),
        name="moe_dispatch",
    )(pad_start, pad_len, n_used, pos_tiles, x2d, ffn_g.astype(F32).reshape(1, D))


def _expert_body(te_ref, nu_ref, nx_ref, sl_ref, x_ref, win_hbm, wout_hbm, y_ref, win_bf, wout_bf, x2_ref,
                 win_st, wout_st, wsem, *, layer):
    i = pl.program_id(0)
    ff = wout_bf.shape[0]
    used = i < nu_ref[0]
    new_expert = jnp.logical_or(i == 0, te_ref[i] != te_ref[jnp.maximum(i - 1, 0)])

    def weight_copies(e, slot):
        return (pltpu.make_async_copy(win_hbm.at[layer, e], win_st.at[slot], wsem.at[0, slot]),
                pltpu.make_async_copy(wout_hbm.at[layer, e], wout_st.at[slot], wsem.at[1, slot]))

    @pl.when(i == 0)
    def _():
        for c in weight_copies(te_ref[0], 0):
            c.start()

    def begin_run(s):
        for c in weight_copies(te_ref[i], s):
            c.wait()
        win_bf[...] = win_st[s].astype(BF16)
        wout_bf[...] = wout_st[s].astype(BF16)

        @pl.when(nx_ref[i] >= 0)
        def _():
            for c in weight_copies(nx_ref[i], 1 - s):
                c.start()

    for s in range(2):
        pl.when(jnp.logical_and(jnp.logical_and(used, new_expert), sl_ref[i] == s))(
            functools.partial(begin_run, s))

    @pl.when(used)
    def _():
        x2_ref[...] = x_ref[...].reshape(x2_ref.shape)
        gu = _dot(x2_ref[...].astype(BF16), win_bf[...])
        act = (_silu(gu[:, 0:ff]) * gu[:, ff:2 * ff]).astype(BF16)
        y = _dot(act, wout_bf[...])
        y_ref[...] = y.reshape(y_ref.shape)

    @pl.when(jnp.logical_not(used))
    def _():
        y_ref[...] = jnp.zeros(y_ref.shape, F32)


def _experts(x_sorted, w_exp_in, w_exp_out, tile_expert, n_used, next_expert, run_slot, layer):
    cap, _, D = x_sorted.shape
    tile = EXPERT_TILE
    nt = cap // tile
    ff = w_exp_out.shape[2]
    rows = lambda i, te, nu, nx, sl: (i, 0, 0)
    grid_spec = pltpu.PrefetchScalarGridSpec(
        num_scalar_prefetch=4,
        grid=(nt,),
        in_specs=[pl.BlockSpec((tile, 1, D), rows),
                  pl.BlockSpec(memory_space=pl.ANY), pl.BlockSpec(memory_space=pl.ANY)],
        out_specs=pl.BlockSpec((tile, 1, D), rows),
        scratch_shapes=[pltpu.VMEM((D, 2 * ff), BF16), pltpu.VMEM((ff, D), BF16),
                        pltpu.VMEM((tile, D), F32),
                        pltpu.VMEM((2, D, 2 * ff), F32), pltpu.VMEM((2, ff, D), F32),
                        pltpu.SemaphoreType.DMA((2, 2))],
    )
    return pl.pallas_call(
        functools.partial(_expert_body, layer=layer),
        grid_spec=grid_spec,
        out_shape=jax.ShapeDtypeStruct((cap, 1, D), F32),
        compiler_params=_params(1, VMEM_LIMIT),
        name="expert_ffn",
    )(tile_expert, n_used, next_expert, run_slot, x_sorted, w_exp_in, w_exp_out)


def _combine_body(pos0_ref, posn_ref, y_hbm, x_ref, route_ref, o_ref, ybuf, y2_ref, sem, *, tile, n_tiles):
    i = pl.program_id(0)
    slot = i % 2

    def gather(pos_ref, s):
        def issue(g, carry):
            for k in range(DMA_UNROLL):
                r = g * DMA_UNROLL + k
                pltpu.make_async_copy(y_hbm.at[pl.ds(pos_ref[0, 0, r], 1)],
                                      ybuf.at[s, pl.ds(r, 1)], sem.at[s]).start(priority=k % 2)
            return carry
        lax.fori_loop(0, 2 * tile // DMA_UNROLL, issue, 0)

    @pl.when(i == 0)
    def _():
        gather(pos0_ref, 0)

    def combine_tile(s):
        @pl.when(i + 1 < n_tiles)
        def _():
            gather(posn_ref, 1 - s)

        _bulk_wait(ybuf.at[s], sem.at[s])
        y2_ref[...] = ybuf[s].reshape(y2_ref.shape)
        route = route_ref[...]
        o_ref[...] = (x_ref[...] + route[:, 2:3] * y2_ref[0:tile, :]
                      + route[:, 3:4] * y2_ref[tile:2 * tile, :])

    for s in range(2):
        pl.when(slot == s)(functools.partial(combine_tile, s))


def _combine(x2d, route, y_sorted, pos_tiles):
    T, D = x2d.shape
    tile = TOKEN_TILE
    nt = T // tile
    return pl.pallas_call(
        functools.partial(_combine_body, tile=tile, n_tiles=nt),
        grid=(nt,),
        in_specs=[pl.BlockSpec((1, 1, 2 * tile), lambda i: (0, 0, 0), memory_space=pltpu.SMEM),
                  pl.BlockSpec((1, 1, 2 * tile), lambda i: (jnp.minimum(i + 1, nt - 1), 0, 0),
                               memory_space=pltpu.SMEM),
                  pl.BlockSpec(memory_space=pl.ANY),
                  pl.BlockSpec((tile, D), lambda i: (i, 0)),
                  pl.BlockSpec((tile, LANES), lambda i: (i, 0))],
        out_specs=pl.BlockSpec((tile, D), lambda i: (i, 0)),
        out_shape=jax.ShapeDtypeStruct((T, D), F32),
        scratch_shapes=[pltpu.VMEM((2, 2 * tile, 1, D), F32), pltpu.VMEM((2 * tile, D), F32),
                        pltpu.SemaphoreType.DMA((2,))],
        compiler_params=_params(1, VMEM_LIMIT),
        name="moe_combine",
    )(pos_tiles, pos_tiles, y_sorted, x2d, route)


def _positions_body(route_ref, offs_ref, pos_ref, *, tile):
    r = route_ref[...]
    lane = lax.broadcasted_iota(jnp.int32, r.shape, 1)
    lane_f = lane.astype(F32)
    offs = offs_ref[...]
    pos1 = jnp.sum(jnp.where(lane_f == r[:, 0:1], offs, 0.0), axis=-1, keepdims=True) + r[:, 4:5]
    pos2 = jnp.sum(jnp.where(lane_f == r[:, 1:2], offs, 0.0), axis=-1, keepdims=True) + r[:, 5:6]
    both = jnp.where(lane == 0, pos1, jnp.where(lane == 1, pos2, 0.0))
    for k in range(r.shape[0] // tile):
        t = both[k * tile:(k + 1) * tile, :].T
        pos_ref[k, :, 0:tile] = t[0:1, :].astype(jnp.int32)
        pos_ref[k, :, tile:2 * tile] = t[1:2, :].astype(jnp.int32)


def _positions(route, offs_row):
    T = route.shape[0]
    tile = TOKEN_TILE
    step = 4 * tile
    return pl.pallas_call(
        functools.partial(_positions_body, tile=tile),
        grid=(T // step,),
        in_specs=[pl.BlockSpec((step, LANES), lambda i: (i, 0)),
                  pl.BlockSpec((1, LANES), lambda i: (0, 0))],
        out_specs=pl.BlockSpec((step // tile, 1, 2 * tile), lambda i: (i, 0, 0)),
        out_shape=jax.ShapeDtypeStruct((T // tile, 1, 2 * tile), jnp.int32),
        compiler_params=_params(1, VMEM_LIMIT),
        name="moe_positions",
    )(route, offs_row)


def _dispatch_plan(route, counts, n_experts, tile, cap):
    cnt = counts[0, :n_experts].astype(jnp.int32)
    padded = ((cnt + tile - 1) // tile) * tile
    ends = jnp.cumsum(padded)
    offs = ends - padded
    nt = cap // tile
    n_used = (ends[-1] // tile).astype(jnp.int32)
    first_row = jnp.minimum(jnp.arange(nt, dtype=jnp.int32), n_used - 1) * tile
    te = jnp.sum((ends[None, :] <= first_row[:, None]).astype(jnp.int32), axis=1)
    te = jnp.minimum(te, n_experts - 1).astype(jnp.int32)
    run_start = jnp.concatenate([jnp.ones((1,), bool), te[1:] != te[:-1]])
    run_slot = ((jnp.cumsum(run_start.astype(jnp.int32)) - 1) % 2).astype(jnp.int32)
    ids = jnp.arange(n_experts, dtype=jnp.int32)
    later = (ids[None, :] > ids[:, None]) & (cnt > 0)[None, :]
    following = jnp.min(jnp.where(later, ids[None, :], n_experts), axis=1)
    following = jnp.where(following >= n_experts, -1, following).astype(jnp.int32)
    next_expert = jnp.sum(jnp.where(te[:, None] == ids[None, :], following[None, :], 0), axis=1).astype(jnp.int32)
    offs_row = jnp.pad(offs.astype(F32), (0, LANES - n_experts)).reshape(1, LANES)
    pos_tiles = _positions(route, offs_row)
    return (te, n_used.reshape(1), pos_tiles, (offs + cnt).astype(jnp.int32), (padded - cnt).astype(jnp.int32),
            next_expert, run_slot)


def kernel(x, mix_norm_g, w_in, b_gate, q_norm_g, k_norm_g, lb_logits, hgrn_norm_g, conv_w, conv_b,
           conv_ln_g, conv_ln_b, w_att_o, w_hgrn_o, w_conv_o, w_out, ffn_norm_g, w_coarse, b_coarse,
           w_fine, b_fine, w_exp_in, w_exp_out):
    B, S, D = x.shape
    L = w_in.shape[0]
    T = B * S
    n_experts = w_exp_in.shape[1]
    cap = 2 * T + n_experts * EXPERT_TILE

    w_in_bf = w_in.astype(BF16)
    wa_bf, wh_bf, wc_bf, wo_bf = (w.astype(BF16) for w in (w_att_o, w_hgrn_o, w_conv_o, w_out))
    pad = LANES - N_GROUPS - n_experts
    w_route = jnp.concatenate([w_coarse, w_fine, jnp.zeros((L, D, pad), F32)], axis=-1).astype(F32)
    b_route = jnp.concatenate([b_coarse, b_fine, jnp.zeros((L, pad), F32)], axis=-1).astype(F32)

    x2d = x.reshape(T, D)
    for l in range(L):
        qkv, hf, hqig, cab, gl = _inproj(x2d, mix_norm_g[l], w_in_bf, l)
        att = _attention(qkv, q_norm_g[l], k_norm_g[l], B, S)
        rec = _hgrn(hqig, hf, lb_logits, hgrn_norm_g[l], l, B, S)
        cv = _conv(cab, conv_w[l], conv_b[l], conv_ln_g[l], conv_ln_b[l], B, S)
        x_mid, route, counts = _merge(att, rec, cv, gl, b_gate[l], x2d, wa_bf, wh_bf, wc_bf, wo_bf,
                                      ffn_norm_g[l], w_route[l], b_route[l].reshape(1, LANES), l)
        te, n_used, pos_tiles, pad_start, pad_len, next_expert, run_slot = _dispatch_plan(
            route, counts, n_experts, EXPERT_TILE, cap)
        x_sorted = _dispatch(x_mid, ffn_norm_g[l], pos_tiles, pad_start, pad_len, n_used, cap)
        y_sorted = _experts(x_sorted, w_exp_in, w_exp_out, te, n_used, next_expert, run_slot, l)
        x2d = _combine(x_mid, route, y_sorted, pos_tiles)
    return x2d.reshape(B, S, D)
```
